```python
import jax, jax.numpy as jnp
from jax import lax
import numpy as np

D_MODEL = 1024
BATCH = 8
SEQ = 4096
DEPTH = 1

CHUNK = 64
N_LEFT_CHUNKS = 8
BAND = (N_LEFT_CHUNKS + 1) * CHUNK
ATT_HEAD_DIM = 64
ATT_HEADS = (D_MODEL // 2) // ATT_HEAD_DIM
ATT_WIDTH = ATT_HEADS * ATT_HEAD_DIM
REL_CLIP = 128
GMLP_GROUP_DIM = 128
GMLP_GROUPS = (D_MODEL // 2) // GMLP_GROUP_DIM
GMLP_WIDTH = GMLP_GROUPS * GMLP_GROUP_DIM
GMLP_BLOCK = 128
MIX_WIDTH = ATT_WIDTH + GMLP_WIDTH
IN_PROJ_WIDTH = 3 * ATT_WIDTH + 2 * GMLP_WIDTH
N_EXPERTS = 32
TOP_K = 4
D_EXPERT = D_MODEL
SWIGLU_LIMIT = 7.0
SWIGLU_ALPHA = 1.702
MOE_BLOCK = 256
EPS = 1e-6
NEG_INF = -1e30

kernel_name = 'hybrid_chunked_attn_gmlp_moe'


def rms_norm(x, g):
    x32 = x.astype(jnp.float32)
    y = x32 * lax.rsqrt(jnp.mean(x32 * x32, axis=-1, keepdims=True) + EPS)
    return y.astype(x.dtype) * g


def chunked_attention(q, k, v, rel_bias):
    b, s, h, dh = q.shape
    n_chunks = s // CHUNK
    left = BAND - CHUNK
    k_pad = jnp.pad(k, ((0, 0), (left, 0), (0, 0), (0, 0)))
    v_pad = jnp.pad(v, ((0, 0), (left, 0), (0, 0), (0, 0)))
    q_idx = jnp.arange(CHUNK)
    k_idx = jnp.arange(BAND)
    dist = left + q_idx[:, None] - k_idx[None, :]
    bias = rel_bias[:, jnp.clip(dist, -REL_CLIP, REL_CLIP) + REL_CLIP].astype(jnp.float32)
    scale = ATT_HEAD_DIM ** -0.5

    def attend_chunk(c):
        start = c * CHUNK
        qc = lax.dynamic_slice_in_dim(q, start, CHUNK, axis=1)
        kc = lax.dynamic_slice_in_dim(k_pad, start, BAND, axis=1)
        vc = lax.dynamic_slice_in_dim(v_pad, start, BAND, axis=1)
        sc = jnp.einsum('bqhd,bkhd->bhqk', qc, kc).astype(jnp.float32) * scale + bias
        valid = (start - left + k_idx) >= 0
        sc = jnp.where(valid[None, None, None, :], sc, NEG_INF)
        p = jax.nn.softmax(sc, axis=-1).astype(vc.dtype)
        return jnp.einsum('bhqk,bkhd->bqhd', p, vc)

    out = lax.map(attend_chunk, jnp.arange(n_chunks))
    return jnp.transpose(out, (1, 0, 2, 3, 4)).reshape(b, s, h * dh)


def gmlp_spatial_gating(u, v, v_norm_g, w_s, b_s):
    b, s, _ = u.shape
    n_blocks = s // GMLP_BLOCK
    u = jax.nn.gelu(u, approximate=False)
    v = jax.nn.gelu(v, approximate=False).reshape(b, n_blocks, GMLP_BLOCK, GMLP_GROUPS, GMLP_GROUP_DIM)
    v = rms_norm(v, v_norm_g)
    pos_chunk = jnp.arange(GMLP_BLOCK) // CHUNK
    mask = (pos_chunk[:, None] >= pos_chunk[None, :]).astype(w_s.dtype)
    gate = jnp.einsum('gij,bnjgc->bnigc', w_s * mask[None], v) + b_s.T[:, :, None]
    return u * gate.reshape(b, s, GMLP_WIDTH)


def moe_ffn(xn, w_router, b_router, w_expert_in, b_expert_in, w_expert_out, b_expert_out):
    b, s, d = xn.shape
    t = b * s
    n_assign = t * TOP_K
    xt = xn.reshape(t, d)
    logits = (xt @ w_router + b_router).astype(jnp.float32)
    top_val, top_idx = lax.top_k(logits, TOP_K)
    gates = jax.nn.softmax(top_val, axis=-1).astype(xn.dtype)
    flat_e = top_idx.reshape(-1)
    flat_tok = jnp.arange(n_assign, dtype=jnp.int32) // TOP_K
    order = jnp.argsort(flat_e)
    sorted_e = flat_e[order]
    sorted_tok = flat_tok[order]
    sorted_gate = gates.reshape(-1)[order]
    counts = jnp.zeros((N_EXPERTS,), jnp.int32).at[flat_e].add(1)
    starts = jnp.cumsum(counts) - counts
    padded = ((counts + MOE_BLOCK - 1) // MOE_BLOCK) * MOE_BLOCK
    pad_ends = jnp.cumsum(padded)
    pad_starts = pad_ends - padded
    dest = pad_starts[sorted_e] + (jnp.arange(n_assign, dtype=jnp.int32) - starts[sorted_e])
    n_blocks = -(-n_assign // MOE_BLOCK) + N_EXPERTS
    x_pad = jnp.zeros((n_blocks * MOE_BLOCK, d), xt.dtype).at[dest].set(xt[sorted_tok])
    block_start = jnp.arange(n_blocks, dtype=jnp.int32) * MOE_BLOCK
    block_e = jnp.minimum(jnp.sum(pad_ends[None, :] <= block_start[:, None], axis=1), N_EXPERTS - 1)

    def expert_block(args):
        xb, e = args
        hdn = xb @ w_expert_in[e] + b_expert_in[e]
        g, lin = hdn[:, :D_EXPERT], hdn[:, D_EXPERT:]
        g = jnp.minimum(g, SWIGLU_LIMIT)
        lin = jnp.clip(lin, -SWIGLU_LIMIT, SWIGLU_LIMIT)
        act = g * jax.nn.sigmoid(SWIGLU_ALPHA * g) * (lin + 1)
        return act @ w_expert_out[e] + b_expert_out[e]

    y_pad = lax.map(expert_block, (x_pad.reshape(n_blocks, MOE_BLOCK, d), block_e)).reshape(-1, d)
    y = y_pad[dest] * sorted_gate[:, None]
    out = jnp.zeros((t, d), xn.dtype).at[sorted_tok].add(y)
    return out.reshape(b, s, d)


def setup_inputs(seed: int = 0) -> dict:
    key = jax.random.key(seed)
    ks = jax.random.split(key, 20)
    f32 = jnp.float32
    nrm = lambda k, shape, sc: jax.random.normal(k, shape, f32) * sc
    return {
        'x': nrm(ks[0], (BATCH, SEQ, D_MODEL), 1.0),
        'norm_mix_g': 1.0 + nrm(ks[1], (D_MODEL,), 0.05),
        'w_in_proj': nrm(ks[2], (D_MODEL, IN_PROJ_WIDTH), D_MODEL ** -0.5),
        'q_norm_g': 1.0 + nrm(ks[3], (ATT_HEAD_DIM,), 0.05),
        'k_norm_g': 1.0 + nrm(ks[4], (ATT_HEAD_DIM,), 0.05),
        'rel_bias': nrm(ks[5], (ATT_HEADS, 2 * REL_CLIP + 1), 0.5),
        'gmlp_v_norm_g': 1.0 + nrm(ks[6], (GMLP_GROUPS, GMLP_GROUP_DIM), 0.05),
        'gmlp_w_s': nrm(ks[7], (GMLP_GROUPS, GMLP_BLOCK, GMLP_BLOCK), GMLP_BLOCK ** -0.5),
        'gmlp_b_s': 1.0 + nrm(ks[8], (GMLP_GROUPS, GMLP_BLOCK), 0.1),
        'att_out_norm_g': 1.0 + nrm(ks[9], (ATT_WIDTH,), 0.05),
        'gmlp_out_norm_g': 1.0 + nrm(ks[10], (GMLP_WIDTH,), 0.05),
        'w_out_proj': nrm(ks[11], (MIX_WIDTH, D_MODEL), MIX_WIDTH ** -0.5),
        'norm_moe_g': 1.0 + nrm(ks[12], (D_MODEL,), 0.05),
        'w_router': nrm(ks[13], (D_MODEL, N_EXPERTS), D_MODEL ** -0.5),
        'b_router': nrm(ks[14], (N_EXPERTS,), 0.01),
        'w_expert_in': nrm(ks[15], (N_EXPERTS, D_MODEL, 2 * D_EXPERT), D_MODEL ** -0.5),
        'b_expert_in': nrm(ks[16], (N_EXPERTS, 2 * D_EXPERT), 0.02),
        'w_expert_out': nrm(ks[17], (N_EXPERTS, D_EXPERT, D_MODEL), D_EXPERT ** -0.5),
        'b_expert_out': nrm(ks[18], (N_EXPERTS, D_MODEL), 0.02),
    }


def reference(x, norm_mix_g, w_in_proj, q_norm_g, k_norm_g, rel_bias, gmlp_v_norm_g, gmlp_w_s,
              gmlp_b_s, att_out_norm_g, gmlp_out_norm_g, w_out_proj, norm_moe_g, w_router, b_router,
              w_expert_in, b_expert_in, w_expert_out, b_expert_out):
    b, s, _ = x.shape
    h = x
    for _ in range(DEPTH):
        z = rms_norm(h, norm_mix_g)
        proj = z @ w_in_proj
        q, k, v, gu, gv = jnp.split(proj, [ATT_WIDTH, 2 * ATT_WIDTH, 3 * ATT_WIDTH,
                                           3 * ATT_WIDTH + GMLP_WIDTH], axis=-1)
        q = rms_norm(q.reshape(b, s, ATT_HEADS, ATT_HEAD_DIM), q_norm_g)
        k = rms_norm(k.reshape(b, s, ATT_HEADS, ATT_HEAD_DIM), k_norm_g)
        v = v.reshape(b, s, ATT_HEADS, ATT_HEAD_DIM)
        att_out = chunked_attention(q, k, v, rel_bias)
        gmlp_out = gmlp_spatial_gating(gu, gv, gmlp_v_norm_g, gmlp_w_s, gmlp_b_s)
        mixed = jnp.concatenate([rms_norm(att_out, att_out_norm_g),
                                 rms_norm(gmlp_out, gmlp_out_norm_g)], axis=-1)
        h = h + mixed @ w_out_proj
        h = h + moe_ffn(rms_norm(h, norm_moe_g), w_router, b_router, w_expert_in, b_expert_in,
                        w_expert_out, b_expert_out)
    return h
```

```python
import functools

import jax
import jax.numpy as jnp
from jax import lax
from jax.experimental import pallas as pl
from jax.experimental.pallas import tpu as pltpu

D_MODEL = 1024
CHUNK = 64
LEFT = 8 * CHUNK
N_HEADS = 8
HEAD_DIM = 64
ATT_W = 512
REL_CLIP = 128
G_GROUPS = 4
G_DIM = 128
G_BLOCK = 128
G_W = 512
N_EXPERTS = 32
TOP_K = 4
D_EXPERT = 1024
SWIGLU_LIMIT = 7.0
SWIGLU_ALPHA = 1.702
EPS = 1e-6
NEG_INF = -1e30

TM = 512
TQ = 128
KWIN = TQ + LEFT
TME = 256
TT_DISPATCH = 256
TT_COMBINE = 128
VMEM_LIMIT = 56 * 1024 * 1024
LANES = 128
ROW_SUB_PACKED = D_MODEL // 2 // LANES
ROW_SUB = D_MODEL // LANES

F32 = jnp.float32
BF16 = jnp.bfloat16


def _rms(x, gain):
    ms = jnp.mean(x * x, axis=-1, keepdims=True)
    return x * lax.rsqrt(ms + EPS) * gain


def _gelu(x):
    return 0.5 * x * (1.0 + lax.erf(x * (2.0 ** -0.5)))


def _inproj_kernel(x_ref, g_ref, w_ref, gq_ref, gk_ref, bd_ref, gvn_ref, ws_ref, bst_ref, gout_ref,
                   q_ref, kt_ref, v_ref, gm_ref):
    j = pl.program_id(1)

    @pl.when(j == 0)
    def _():
        kt_ref[...] = jnp.zeros_like(kt_ref)
        v_ref[...] = jnp.zeros_like(v_ref)

    @pl.when(j > 0)
    def _():
        z = _rms(x_ref[0], g_ref[...]).astype(BF16)

        def proj(c0, width):
            return jnp.dot(z, w_ref[:, c0:c0 + width], preferred_element_type=F32)

        def head_norm(t, gain):
            ssum = jnp.dot((t * t).astype(BF16), bd_ref[...], preferred_element_type=F32)
            return t * lax.rsqrt(ssum * (1.0 / HEAD_DIM) + EPS) * gain

        q = head_norm(proj(0, ATT_W), gq_ref[...]) * (HEAD_DIM ** -0.5)
        q_ref[0] = q.astype(BF16)
        k = head_norm(proj(ATT_W, ATT_W), gk_ref[...])
        kt_ref[0] = k.T.astype(BF16)
        v_ref[0] = proj(2 * ATT_W, ATT_W).astype(BF16)

        gu = _gelu(proj(3 * ATT_W, G_W))
        gv = _gelu(proj(3 * ATT_W + G_W, G_W))
        row = lax.broadcasted_iota(jnp.int32, (G_BLOCK, G_BLOCK), 0) // CHUNK
        col = lax.broadcasted_iota(jnp.int32, (G_BLOCK, G_BLOCK), 1) // CHUNK
        tri = row >= col
        cols = []
        for g in range(G_GROUPS):
            sl = slice(g * G_DIM, (g + 1) * G_DIM)
            vn = _rms(gv[:, sl], gvn_ref[g:g + 1, :]).astype(BF16)
            wm = jnp.where(tri, ws_ref[g], 0.0).astype(BF16)
            blocks = []
            for n in range(TM // G_BLOCK):
                rs = slice(n * G_BLOCK, (n + 1) * G_BLOCK)
                gate = jnp.dot(wm, vn[rs], preferred_element_type=F32) + bst_ref[:, g:g + 1]
                blocks.append(gu[rs, sl] * gate)
            cols.append(jnp.concatenate(blocks, axis=0))
        gm = jnp.concatenate(cols, axis=1)
        gm_ref[0] = _rms(gm, gout_ref[...]).astype(BF16)


def _inproj(x, g_mix, w_in, gq, gk, bd, gvn, ws, bst, gout):
    b, s, d = x.shape
    nt = s // TM
    xmap = lambda bi, j: (bi, jnp.maximum(j - 1, 0), 0)
    const2 = lambda bi, j: (0, 0)
    return pl.pallas_call(
        _inproj_kernel,
        grid=(b, nt + 1),
        in_specs=[
            pl.BlockSpec((1, TM, d), xmap),
            pl.BlockSpec((1, d), const2),
            pl.BlockSpec(w_in.shape, const2),
            pl.BlockSpec((1, ATT_W), const2),
            pl.BlockSpec((1, ATT_W), const2),
            pl.BlockSpec((ATT_W, ATT_W), const2),
            pl.BlockSpec((G_GROUPS, G_DIM), const2),
            pl.BlockSpec((G_GROUPS, G_BLOCK, G_BLOCK), lambda bi, j: (0, 0, 0)),
            pl.BlockSpec((G_BLOCK, G_GROUPS), const2),
            pl.BlockSpec((1, G_W), const2),
        ],
        out_specs=[
            pl.BlockSpec((1, TM, ATT_W), xmap),
            pl.BlockSpec((1, ATT_W, TM), lambda bi, j: (bi, 0, j)),
            pl.BlockSpec((1, TM, ATT_W), lambda bi, j: (bi, j, 0)),
            pl.BlockSpec((1, TM, G_W), xmap),
        ],
        out_shape=[
            jax.ShapeDtypeStruct((b, s, ATT_W), BF16),
            jax.ShapeDtypeStruct((b, ATT_W, s + LEFT), BF16),
            jax.ShapeDtypeStruct((b, s + LEFT, ATT_W), BF16),
            jax.ShapeDtypeStruct((b, s, G_W), BF16),
        ],
        compiler_params=pltpu.CompilerParams(
            dimension_semantics=("arbitrary", "arbitrary"), vmem_limit_bytes=VMEM_LIMIT),
        name="inproj",
    )(x, g_mix, w_in, gq, gk, bd, gvn, ws, bst, gout)


def _attn_kernel(q_ref, kt_ref, v_ref, bias_ref, gout_ref, o_ref):
    i = pl.program_id(1)
    q0 = pl.multiple_of(i * TQ, TQ)
    win = pl.ds(q0, KWIN)
    colpos = lax.broadcasted_iota(jnp.int32, (1, KWIN), 1)
    pad_bias = jnp.where(colpos >= LEFT - q0, 0.0, NEG_INF).astype(F32)
    lane = lax.broadcasted_iota(jnp.int32, (1, 256), 1)
    outs = []
    for g in range(2):
        gs = slice(g * 256, (g + 1) * 256)
        qg = q_ref[0, :, gs]
        ktg = kt_ref[0, gs, win]
        vg = v_ref[0, win, gs]
        acc = jnp.zeros((TQ, 256), F32)
        for hh in range(4):
            h = 4 * g + hh
            hmask = (lane >= hh * HEAD_DIM) & (lane < (hh + 1) * HEAD_DIM)
            qh = jnp.where(hmask, qg, jnp.zeros_like(qg))
            s = jnp.dot(qh, ktg, preferred_element_type=F32)
            s = s + bias_ref[h] + pad_bias
            m = jnp.max(s, axis=-1, keepdims=True)
            e = jnp.exp(s - m)
            inv = 1.0 / jnp.sum(e, axis=-1, keepdims=True)
            pv = jnp.dot(e.astype(BF16), vg, preferred_element_type=F32)
            acc = jnp.where(hmask, pv * inv, acc)
        outs.append(acc)
    att = jnp.concatenate(outs, axis=1)
    o_ref[0] = _rms(att, gout_ref[...]).astype(BF16)


def _attention(q, kt, v, bias, gout):
    b, s, _ = q.shape
    return pl.pallas_call(
        _attn_kernel,
        grid=(b, s // TQ),
        in_specs=[
            pl.BlockSpec((1, TQ, ATT_W), lambda bi, i: (bi, i, 0)),
            pl.BlockSpec((1, ATT_W, s + LEFT), lambda bi, i: (bi, 0, 0)),
            pl.BlockSpec((1, s + LEFT, ATT_W), lambda bi, i: (bi, 0, 0)),
            pl.BlockSpec(bias.shape, lambda bi, i: (0, 0, 0)),
            pl.BlockSpec((1, ATT_W), lambda bi, i: (0, 0)),
        ],
        out_specs=pl.BlockSpec((1, TQ, ATT_W), lambda bi, i: (bi, i, 0)),
        out_shape=jax.ShapeDtypeStruct((b, s, ATT_W), BF16),
        compiler_params=pltpu.CompilerParams(
            dimension_semantics=("arbitrary", "arbitrary"), vmem_limit_bytes=VMEM_LIMIT),
        name="attention",
    )(q, kt, v, bias, gout)


def _outproj_kernel(a_ref, gm_ref, x_ref, wo_ref, gmoe_ref, wrh_ref, wrl_ref, br_ref,
                    h_ref, xp_ref, idx_ref, gate_ref, rank_ref, cnt_ref, carry_ref):
    i = pl.program_id(0)

    @pl.when(i == 0)
    def _():
        carry_ref[...] = jnp.zeros_like(carry_ref)

    h = (x_ref[...]
         + jnp.dot(a_ref[...], wo_ref[:ATT_W, :], preferred_element_type=F32)
         + jnp.dot(gm_ref[...], wo_ref[ATT_W:, :], preferred_element_type=F32))
    h_ref[...] = h
    xn = _rms(h, gmoe_ref[...])
    xhi = xn.astype(BF16)
    xlo = (xn - xhi.astype(F32)).astype(BF16)
    hi_bits = lax.bitcast_convert_type(xhi[:, :D_MODEL // 2].astype(F32), jnp.uint32)
    lo_bits = lax.bitcast_convert_type(xhi[:, D_MODEL // 2:].astype(F32), jnp.uint32)
    packed = hi_bits | (lo_bits >> 16)
    for sub in range(ROW_SUB_PACKED):
        xp_ref[:, sub, :] = packed[:, sub * LANES:(sub + 1) * LANES]

    nt = (((1,), (1,)), ((), ()))
    logits = (lax.dot_general(wrh_ref[...], xhi, nt, preferred_element_type=F32)
              + lax.dot_general(wrh_ref[...], xlo, nt, preferred_element_type=F32)
              + lax.dot_general(wrl_ref[...], xhi, nt, preferred_element_type=F32)
              + br_ref[...])
    eidx = lax.broadcasted_iota(jnp.int32, (N_EXPERTS, TM), 0)
    vals, idxs = [], []
    cur = logits
    for _ in range(TOP_K):
        m = jnp.max(cur, axis=0, keepdims=True)
        ik = jnp.min(jnp.where(cur == m, eidx, N_EXPERTS), axis=0, keepdims=True)
        vals.append(m)
        idxs.append(ik)
        cur = jnp.where(eidx == ik, -jnp.inf, cur)
    exps = [jnp.exp(v - vals[0]) for v in vals]
    tot = exps[0] + exps[1] + exps[2] + exps[3]
    idx_ref[...] = jnp.concatenate(idxs, axis=0)
    gate_ref[...] = jnp.concatenate([e / tot for e in exps], axis=0)

    onehots = [eidx == ik for ik in idxs]
    member = sum(oh.astype(F32) for oh in onehots)
    before = (lax.broadcasted_iota(jnp.int32, (TM, TM), 0)
              < lax.broadcasted_iota(jnp.int32, (TM, TM), 1)).astype(BF16)
    base = jnp.dot(member.astype(BF16), before, preferred_element_type=F32) + carry_ref[...]
    ranks = [jnp.sum(jnp.where(oh, base, 0.0), axis=0, keepdims=True) for oh in onehots]
    rank_ref[...] = jnp.concatenate(ranks, axis=0).astype(jnp.int32)
    carry_ref[...] += jnp.sum(member, axis=1, keepdims=True)
    cnt_ref[...] = carry_ref[...].astype(jnp.int32)


def _outproj(att, gm, x2, w_out, g_moe, wr_hi, wr_lo, br):
    t, d = x2.shape
    tile = lambda i: (i, 0)
    const = lambda i: (0, 0)
    lanes = lambda i: (0, i)
    return pl.pallas_call(
        _outproj_kernel,
        grid=(t // TM,),
        in_specs=[
            pl.BlockSpec((TM, ATT_W), tile),
            pl.BlockSpec((TM, G_W), tile),
            pl.BlockSpec((TM, d), tile),
            pl.BlockSpec(w_out.shape, const),
            pl.BlockSpec((1, d), const),
            pl.BlockSpec((N_EXPERTS, d), const),
            pl.BlockSpec((N_EXPERTS, d), const),
            pl.BlockSpec((N_EXPERTS, 1), const),
        ],
        out_specs=[
            pl.BlockSpec((TM, d), tile),
            pl.BlockSpec((TM, ROW_SUB_PACKED, LANES), lambda i: (i, 0, 0)),
            pl.BlockSpec((TOP_K, TM), lanes),
            pl.BlockSpec((TOP_K, TM), lanes),
            pl.BlockSpec((TOP_K, TM), lanes),
            pl.BlockSpec((N_EXPERTS, 1), const),
        ],
        out_shape=[
            jax.ShapeDtypeStruct((t, d), F32),
            jax.ShapeDtypeStruct((t, ROW_SUB_PACKED, LANES), jnp.uint32),
            jax.ShapeDtypeStruct((TOP_K, t), jnp.int32),
            jax.ShapeDtypeStruct((TOP_K, t), F32),
            jax.ShapeDtypeStruct((TOP_K, t), jnp.int32),
            jax.ShapeDtypeStruct((N_EXPERTS, 1), jnp.int32),
        ],
        scratch_shapes=[pltpu.VMEM((N_EXPERTS, 1), F32)],
        compiler_params=pltpu.CompilerParams(
            dimension_semantics=("arbitrary",), vmem_limit_bytes=VMEM_LIMIT),
        name="outproj_router",
    )(att, gm, x2, w_out, g_moe, wr_hi, wr_lo, br)


def _dispatch_kernel(dest_ref, pend_ref, xp_ref, o_ref, zero_ref, sem, zsem):
    i = pl.program_id(0)
    t_total = xp_ref.shape[0]

    @pl.when(i == 0)
    def _():
        zero_ref[...] = jnp.zeros_like(zero_ref)

        def zero_copy(e):
            return pltpu.make_async_copy(zero_ref, o_ref.at[pl.ds(pend_ref[e] - TME, TME)], zsem)

        def start(e, c):
            prev = jnp.where(e > 0, pend_ref[jnp.maximum(e - 1, 0)], 0)

            @pl.when(pend_ref[e] > prev)
            def _():
                zero_copy(e).start()
            return c

        def wait(e, c):
            prev = jnp.where(e > 0, pend_ref[jnp.maximum(e - 1, 0)], 0)

            @pl.when(pend_ref[e] > prev)
            def _():
                zero_copy(e).wait()
            return c

        lax.fori_loop(0, N_EXPERTS, start, 0)
        lax.fori_loop(0, N_EXPERTS, wait, 0)

        def tail_copy(blk):
            return pltpu.make_async_copy(zero_ref, o_ref.at[pl.ds(blk * TME, TME)], zsem)

        first_tail = pend_ref[N_EXPERTS - 1] // TME
        n_blocks = o_ref.shape[0] // TME
        lax.fori_loop(first_tail, n_blocks, lambda blk, c: (tail_copy(blk).start(), c)[1], 0)
        lax.fori_loop(first_tail, n_blocks, lambda blk, c: (tail_copy(blk).wait(), c)[1], 0)

    base = i * TT_DISPATCH

    def body(t, c):
        tok = base + t
        src = xp_ref.at[tok]
        for k in range(TOP_K):
            d = dest_ref[k * t_total + tok]
            pltpu.make_async_copy(src, o_ref.at[d], sem).start()
        return c

    lax.fori_loop(0, TT_DISPATCH, body, 0, unroll=8)
    n = TT_DISPATCH * TOP_K
    pltpu.make_async_copy(xp_ref.at[pl.ds(0, n)], o_ref.at[pl.ds(0, n)], sem).wait()


def _dispatch(dest_flat, pad_ends, xp, n_rows):
    t = xp.shape[0]
    return pl.pallas_call(
        _dispatch_kernel,
        grid_spec=pltpu.PrefetchScalarGridSpec(
            num_scalar_prefetch=2,
            grid=(t // TT_DISPATCH,),
            in_specs=[pl.BlockSpec(memory_space=pl.ANY)],
            out_specs=pl.BlockSpec(memory_space=pl.ANY),
            scratch_shapes=[pltpu.VMEM((TME,) + xp.shape[1:], jnp.uint32),
                            pltpu.SemaphoreType.DMA, pltpu.SemaphoreType.DMA],
        ),
        out_shape=jax.ShapeDtypeStruct((n_rows,) + xp.shape[1:], jnp.uint32),
        compiler_params=pltpu.CompilerParams(dimension_semantics=("arbitrary",)),
        name="dispatch",
    )(dest_flat, pad_ends, xp)


def _expert_kernel(be_ref, nvb_ref, x_ref, wi_ref, bi_ref, wo_ref, bo_ref, y_ref, wib_ref, wob_ref):
    i = pl.program_id(0)
    e = be_ref[i]
    prev = be_ref[jnp.maximum(i - 1, 0)]

    @pl.when((i == 0) | (e != prev))
    def _():
        wib_ref[...] = wi_ref[0].astype(BF16)
        wob_ref[...] = wo_ref[0].astype(BF16)

    @pl.when(i < nvb_ref[0])
    def _():
        xa, xb = [], []
        for sub in range(ROW_SUB_PACKED):
            w = x_ref[:, sub, :]
            xa.append(lax.bitcast_convert_type(w & jnp.uint32(0xFFFF0000), F32).astype(BF16))
            xb.append(lax.bitcast_convert_type(w << 16, F32).astype(BF16))
        x = jnp.concatenate(xa + xb, axis=1)
        hdn = jnp.dot(x, wib_ref[...], preferred_element_type=F32) + bi_ref[0]
        gl = jnp.minimum(hdn[:, :D_EXPERT], SWIGLU_LIMIT)
        lin = jnp.clip(hdn[:, D_EXPERT:], -SWIGLU_LIMIT, SWIGLU_LIMIT)
        act = gl * jax.nn.sigmoid(SWIGLU_ALPHA * gl) * (lin + 1.0)
        y = jnp.dot(act.astype(BF16), wob_ref[...], preferred_element_type=F32) + bo_ref[0]
        for sub in range(ROW_SUB):
            y_ref[:, sub, :] = y[:, sub * LANES:(sub + 1) * LANES]

    @pl.when(i >= nvb_ref[0])
    def _():
        y_ref[...] = jnp.zeros_like(y_ref)


def _experts(block_e, nvb, x_pad, w_in, b_in, w_out, b_out):
    n_rows = x_pad.shape[0]
    d = D_MODEL
    nb = n_rows // TME
    rows = lambda i, be, nv: (jnp.minimum(i, nv[0] - 1), 0, 0)
    wsel = lambda i, be, nv: (be[i], 0, 0)
    return pl.pallas_call(
        _expert_kernel,
        grid_spec=pltpu.PrefetchScalarGridSpec(
            num_scalar_prefetch=2,
            grid=(nb,),
            in_specs=[
                pl.BlockSpec((TME, ROW_SUB_PACKED, LANES), rows),
                pl.BlockSpec((1, d, 2 * D_EXPERT), wsel),
                pl.BlockSpec((1, 1, 2 * D_EXPERT), wsel),
                pl.BlockSpec((1, D_EXPERT, d), wsel),
                pl.BlockSpec((1, 1, d), wsel),
            ],
            out_specs=pl.BlockSpec((TME, ROW_SUB, LANES), lambda i, be, nv: (i, 0, 0)),
            scratch_shapes=[pltpu.VMEM((d, 2 * D_EXPERT), BF16), pltpu.VMEM((D_EXPERT, d), BF16)],
        ),
        out_shape=jax.ShapeDtypeStruct((n_rows, ROW_SUB, LANES), F32),
        compiler_params=pltpu.CompilerParams(
            dimension_semantics=("arbitrary",), vmem_limit_bytes=VMEM_LIMIT),
        name="experts",
    )(block_e, nvb, x_pad, w_in, b_in, w_out, b_out)


def _combine_kernel(dest_ref, y_ref, h_ref, gate_ref, o_ref, buf_ref, sem):
    i = pl.program_id(0)
    n = pl.num_programs(0)
    t_total = n * TT_COMBINE

    def issue(step, slot):
        base = step * TT_COMBINE

        def body(t, c):
            for k in range(TOP_K):
                d = dest_ref[k * t_total + base + t]
                pltpu.make_async_copy(y_ref.at[d], buf_ref.at[slot, k, t],
                                      sem.at[slot]).start()
            return c

        lax.fori_loop(0, TT_COMBINE, body, 0, unroll=8)

    @pl.when(i == 0)
    def _():
        issue(0, 0)

    @pl.when(i + 1 < n)
    def _():
        issue(i + 1, (i + 1) % 2)

    slot = i % 2
    for k in range(TOP_K):
        pltpu.make_async_copy(y_ref.at[pl.ds(0, TT_COMBINE)], buf_ref.at[slot, k], sem.at[slot]).wait()
    for sub in range(ROW_SUB):
        cols = slice(sub * LANES, (sub + 1) * LANES)
        acc = h_ref[:, cols]
        for k in range(TOP_K):
            acc = acc + gate_ref[:, k:k + 1] * buf_ref[slot, k, :, sub, :]
        o_ref[:, cols] = acc


def _combine(dest_flat, y_pad, h, gates_tk):
    t, d = h.shape
    tile = lambda i, dest: (i, 0)
    return pl.pallas_call(
        _combine_kernel,
        grid_spec=pltpu.PrefetchScalarGridSpec(
            num_scalar_prefetch=1,
            grid=(t // TT_COMBINE,),
            in_specs=[
                pl.BlockSpec(memory_space=pl.ANY),
                pl.BlockSpec((TT_COMBINE, d), tile),
                pl.BlockSpec((TT_COMBINE, TOP_K), tile),
            ],
            out_specs=pl.BlockSpec((TT_COMBINE, d), tile),
            scratch_shapes=[pltpu.VMEM((2, TOP_K, TT_COMBINE, ROW_SUB, LANES), F32),
                            pltpu.SemaphoreType.DMA((2,))],
        ),
        out_shape=jax.ShapeDtypeStruct((t, d), F32),
        compiler_params=pltpu.CompilerParams(
            dimension_semantics=("arbitrary",), vmem_limit_bytes=VMEM_LIMIT),
        name="combine",
    )(dest_flat, y_pad, h, gates_tk)


def _band_bias(rel_bias):
    r = jnp.arange(TQ)[:, None]
    j = jnp.arange(KWIN)[None, :]
    lo = (r // CHUNK) * CHUNK
    in_band = (j >= lo) & (j < lo + LEFT + CHUNK)
    dist = jnp.clip(LEFT + r - j, -REL_CLIP, REL_CLIP) + REL_CLIP
    return jnp.where(in_band[None], rel_bias[:, dist].astype(F32), NEG_INF)


def kernel(x, norm_mix_g, w_in_proj, q_norm_g, k_norm_g, rel_bias, gmlp_v_norm_g, gmlp_w_s, gmlp_b_s,
           att_out_norm_g, gmlp_out_norm_g, w_out_proj, norm_moe_g, w_router, b_router,
           w_expert_in, b_expert_in, w_expert_out, b_expert_out):
    b, s, d = x.shape
    t = b * s

    head_of = jnp.arange(ATT_W) // HEAD_DIM
    block_diag = (head_of[:, None] == head_of[None, :]).astype(BF16)
    q, kt, v, gm = _inproj(
        x, norm_mix_g.reshape(1, d), w_in_proj.astype(BF16),
        jnp.tile(q_norm_g, N_HEADS).reshape(1, ATT_W), jnp.tile(k_norm_g, N_HEADS).reshape(1, ATT_W),
        block_diag, gmlp_v_norm_g, gmlp_w_s, gmlp_b_s.T, gmlp_out_norm_g.reshape(1, G_W))
    att = _attention(q, kt, v, _band_bias(rel_bias), att_out_norm_g.reshape(1, ATT_W))

    wr_t = w_router.T
    wr_hi = wr_t.astype(BF16)
    wr_lo = (wr_t - wr_hi.astype(F32)).astype(BF16)
    h, xp, idx, gates, rank, cnt = _outproj(
        att.reshape(t, ATT_W), gm.reshape(t, G_W), x.reshape(t, d), w_out_proj.astype(BF16),
        norm_moe_g.reshape(1, d), wr_hi, wr_lo, b_router.reshape(N_EXPERTS, 1))

    counts = cnt[:, 0]
    padded = ((counts + TME - 1) // TME) * TME
    pad_ends = jnp.cumsum(padded).astype(jnp.int32)
    pad_starts = pad_ends - padded
    dest = (pad_starts[idx] + rank).reshape(-1)
    n_blocks = (t * TOP_K) // TME + N_EXPERTS
    nvb = pad_ends[-1] // TME
    block_start = jnp.arange(n_blocks, dtype=jnp.int32) * TME
    block_e = jnp.minimum(jnp.sum(pad_ends[None, :] <= block_start[:, None], axis=1), N_EXPERTS - 1)
    block_e = jnp.where(jnp.arange(n_blocks) < nvb, block_e, block_e[nvb - 1]).astype(jnp.int32)

    x_pad = _dispatch(dest, pad_ends, xp, n_blocks * TME)
    y_pad = _experts(block_e, nvb.reshape(1), x_pad, w_expert_in,
                     b_expert_in.reshape(N_EXPERTS, 1, 2 * D_EXPERT), w_expert_out,
                     b_expert_out.reshape(N_EXPERTS, 1, d))
    out = _combine(dest, y_pad, h, gates.T)
    return out.reshape(b, s, d)
```

```python
import functools

import jax
import jax.numpy as jnp
from jax import lax
from jax.experimental import pallas as pl
from jax.experimental.pallas import tpu as pltpu

D_MODEL = 1024
CHUNK = 64
LEFT = 8 * CHUNK
N_HEADS = 8
HEAD_DIM = 64
ATT_W = 512
REL_CLIP = 128
G_GROUPS = 4
G_DIM = 128
G_BLOCK = 128
G_W = 512
N_EXPERTS = 32
TOP_K = 4
D_EXPERT = 1024
SWIGLU_LIMIT = 7.0
SWIGLU_ALPHA = 1.702
EPS = 1e-6
NEG_INF = -1e30

TM = 512
TQ = 128
KWIN = TQ + LEFT
TME = 256
TT_DISPATCH = 256
TT_COMBINE = 128
VMEM_LIMIT = 56 * 1024 * 1024
LANES = 128
ROW_SUB = D_MODEL // LANES

F32 = jnp.float32
BF16 = jnp.bfloat16


def _rms(x, gain):
    ms = jnp.mean(x * x, axis=-1, keepdims=True)
    return x * lax.rsqrt(ms + EPS) * gain


def _gelu(x):
    return 0.5 * x * (1.0 + lax.erf(x * (2.0 ** -0.5)))


def _inproj_kernel(x_ref, g_ref, w_ref, gq_ref, gk_ref, bd_ref, gvn_ref, ws_ref, bst_ref, gout_ref,
                   q_ref, kt_ref, v_ref, gm_ref):
    j = pl.program_id(1)

    @pl.when(j == 0)
    def _():
        kt_ref[...] = jnp.zeros_like(kt_ref)
        v_ref[...] = jnp.zeros_like(v_ref)

    @pl.when(j > 0)
    def _():
        z = _rms(x_ref[0], g_ref[...]).astype(BF16)

        def proj(c0, width):
            return jnp.dot(z, w_ref[:, c0:c0 + width], preferred_element_type=F32)

        def head_norm(t, gain):
            ssum = jnp.dot((t * t).astype(BF16), bd_ref[...], preferred_element_type=F32)
            return t * lax.rsqrt(ssum * (1.0 / HEAD_DIM) + EPS) * gain

        q = head_norm(proj(0, ATT_W), gq_ref[...]) * (HEAD_DIM ** -0.5)
        q_ref[0] = q.astype(BF16)
        k = head_norm(proj(ATT_W, ATT_W), gk_ref[...])
        kt_ref[0] = k.T.astype(BF16)
        v_ref[0] = proj(2 * ATT_W, ATT_W).astype(BF16)

        gu = _gelu(proj(3 * ATT_W, G_W))
        gv = _gelu(proj(3 * ATT_W + G_W, G_W))
        row = lax.broadcasted_iota(jnp.int32, (G_BLOCK, G_BLOCK), 0) // CHUNK
        col = lax.broadcasted_iota(jnp.int32, (G_BLOCK, G_BLOCK), 1) // CHUNK
        tri = row >= col
        cols = []
        for g in range(G_GROUPS):
            sl = slice(g * G_DIM, (g + 1) * G_DIM)
            vn = _rms(gv[:, sl], gvn_ref[g:g + 1, :]).astype(BF16)
            wm = jnp.where(tri, ws_ref[g], 0.0).astype(BF16)
            blocks = []
            for n in range(TM // G_BLOCK):
                rs = slice(n * G_BLOCK, (n + 1) * G_BLOCK)
                gate = jnp.dot(wm, vn[rs], preferred_element_type=F32) + bst_ref[:, g:g + 1]
                blocks.append(gu[rs, sl] * gate)
            cols.append(jnp.concatenate(blocks, axis=0))
        gm = jnp.concatenate(cols, axis=1)
        gm_ref[0] = _rms(gm, gout_ref[...]).astype(BF16)


def _inproj(x, g_mix, w_in, gq, gk, bd, gvn, ws, bst, gout):
    b, s, d = x.shape
    nt = s // TM
    xmap = lambda bi, j: (bi, jnp.maximum(j - 1, 0), 0)
    const2 = lambda bi, j: (0, 0)
    return pl.pallas_call(
        _inproj_kernel,
        grid=(b, nt + 1),
        in_specs=[
            pl.BlockSpec((1, TM, d), xmap),
            pl.BlockSpec((1, d), const2),
            pl.BlockSpec(w_in.shape, const2),
            pl.BlockSpec((1, ATT_W), const2),
            pl.BlockSpec((1, ATT_W), const2),
            pl.BlockSpec((ATT_W, ATT_W), const2),
            pl.BlockSpec((G_GROUPS, G_DIM), const2),
            pl.BlockSpec((G_GROUPS, G_BLOCK, G_BLOCK), lambda bi, j: (0, 0, 0)),
            pl.BlockSpec((G_BLOCK, G_GROUPS), const2),
            pl.BlockSpec((1, G_W), const2),
        ],
        out_specs=[
            pl.BlockSpec((1, TM, ATT_W), xmap),
            pl.BlockSpec((1, ATT_W, TM), lambda bi, j: (bi, 0, j)),
            pl.BlockSpec((1, TM, ATT_W), lambda bi, j: (bi, j, 0)),
            pl.BlockSpec((1, TM, G_W), xmap),
        ],
        out_shape=[
            jax.ShapeDtypeStruct((b, s, ATT_W), BF16),
            jax.ShapeDtypeStruct((b, ATT_W, s + LEFT), BF16),
            jax.ShapeDtypeStruct((b, s + LEFT, ATT_W), BF16),
            jax.ShapeDtypeStruct((b, s, G_W), BF16),
        ],
        compiler_params=pltpu.CompilerParams(
            dimension_semantics=("arbitrary", "arbitrary"), vmem_limit_bytes=VMEM_LIMIT),
        name="inproj",
    )(x, g_mix, w_in, gq, gk, bd, gvn, ws, bst, gout)


def _attn_kernel(q_ref, kt_ref, v_ref, bias_ref, gout_ref, o_ref):
    i = pl.program_id(1)
    q0 = pl.multiple_of(i * TQ, TQ)
    win = pl.ds(q0, KWIN)
    colpos = lax.broadcasted_iota(jnp.int32, (1, KWIN), 1)
    pad_bias = jnp.where(colpos >= LEFT - q0, 0.0, NEG_INF).astype(F32)
    lane = lax.broadcasted_iota(jnp.int32, (1, 256), 1)
    outs = []
    for g in range(2):
        gs = slice(g * 256, (g + 1) * 256)
        qg = q_ref[0, :, gs]
        ktg = kt_ref[0, gs, win]
        vg = v_ref[0, win, gs]
        acc = jnp.zeros((TQ, 256), F32)
        for hh in range(4):
            h = 4 * g + hh
            hmask = (lane >= hh * HEAD_DIM) & (lane < (hh + 1) * HEAD_DIM)
            qh = jnp.where(hmask, qg, jnp.zeros_like(qg))
            s = jnp.dot(qh, ktg, preferred_element_type=F32)
            s = s + bias_ref[h] + pad_bias
            m = jnp.max(s, axis=-1, keepdims=True)
            e = jnp.exp(s - m)
            inv = 1.0 / jnp.sum(e, axis=-1, keepdims=True)
            pv = jnp.dot(e.astype(BF16), vg, preferred_element_type=F32)
            acc = jnp.where(hmask, pv * inv, acc)
        outs.append(acc)
    att = jnp.concatenate(outs, axis=1)
    o_ref[0] = _rms(att, gout_ref[...]).astype(BF16)


def _attention(q, kt, v, bias, gout):
    b, s, _ = q.shape
    return pl.pallas_call(
        _attn_kernel,
        grid=(b, s // TQ),
        in_specs=[
            pl.BlockSpec((1, TQ, ATT_W), lambda bi, i: (bi, i, 0)),
            pl.BlockSpec((1, ATT_W, s + LEFT), lambda bi, i: (bi, 0, 0)),
            pl.BlockSpec((1, s + LEFT, ATT_W), lambda bi, i: (bi, 0, 0)),
            pl.BlockSpec(bias.shape, lambda bi, i: (0, 0, 0)),
            pl.BlockSpec((1, ATT_W), lambda bi, i: (0, 0)),
        ],
        out_specs=pl.BlockSpec((1, TQ, ATT_W), lambda bi, i: (bi, i, 0)),
        out_shape=jax.ShapeDtypeStruct((b, s, ATT_W), BF16),
        compiler_params=pltpu.CompilerParams(
            dimension_semantics=("arbitrary", "arbitrary"), vmem_limit_bytes=VMEM_LIMIT),
        name="attention",
    )(q, kt, v, bias, gout)


def _outproj_kernel(a_ref, gm_ref, x_ref, wo_ref, gmoe_ref, wrh_ref, wrl_ref, br_ref,
                    h_ref, xp_ref, idx_ref, gate_ref, rank_ref, cnt_ref, carry_ref):
    i = pl.program_id(0)

    @pl.when(i == 0)
    def _():
        carry_ref[...] = jnp.zeros_like(carry_ref)

    h = (x_ref[...]
         + jnp.dot(a_ref[...], wo_ref[:ATT_W, :], preferred_element_type=F32)
         + jnp.dot(gm_ref[...], wo_ref[ATT_W:, :], preferred_element_type=F32))
    h_ref[...] = h
    xn = _rms(h, gmoe_ref[...])
    xhi = xn.astype(BF16)
    xlo = (xn - xhi.astype(F32)).astype(BF16)
    for sub in range(ROW_SUB):
        xp_ref[pl.ds(sub, TM, stride=ROW_SUB), :] = xn[:, sub * LANES:(sub + 1) * LANES]

    nt = (((1,), (1,)), ((), ()))
    logits = (lax.dot_general(wrh_ref[...], xhi, nt, preferred_element_type=F32)
              + lax.dot_general(wrh_ref[...], xlo, nt, preferred_element_type=F32)
              + lax.dot_general(wrl_ref[...], xhi, nt, preferred_element_type=F32)
              + br_ref[...])
    eidx = lax.broadcasted_iota(jnp.int32, (N_EXPERTS, TM), 0)
    vals, idxs = [], []
    cur = logits
    for _ in range(TOP_K):
        m = jnp.max(cur, axis=0, keepdims=True)
        ik = jnp.min(jnp.where(cur == m, eidx, N_EXPERTS), axis=0, keepdims=True)
        vals.append(m)
        idxs.append(ik)
        cur = jnp.where(eidx == ik, -jnp.inf, cur)
    exps = [jnp.exp(v - vals[0]) for v in vals]
    tot = exps[0] + exps[1] + exps[2] + exps[3]
    idx_ref[...] = jnp.concatenate(idxs, axis=0)
    gate_ref[...] = jnp.concatenate([e / tot for e in exps], axis=0)

    onehots = [eidx == ik for ik in idxs]
    member = sum(oh.astype(F32) for oh in onehots)
    before = (lax.broadcasted_iota(jnp.int32, (TM, TM), 0)
              < lax.broadcasted_iota(jnp.int32, (TM, TM), 1)).astype(BF16)
    base = jnp.dot(member.astype(BF16), before, preferred_element_type=F32) + carry_ref[...]
    ranks = [jnp.sum(jnp.where(oh, base, 0.0), axis=0, keepdims=True) for oh in onehots]
    rank_ref[...] = jnp.concatenate(ranks, axis=0).astype(jnp.int32)
    carry_ref[...] += jnp.sum(member, axis=1, keepdims=True)
    cnt_ref[...] = carry_ref[...].astype(jnp.int32)


def _outproj(att, gm, x2, w_out, g_moe, wr_hi, wr_lo, br):
    t, d = x2.shape
    tile = lambda i: (i, 0)
    const = lambda i: (0, 0)
    lanes = lambda i: (0, i)
    return pl.pallas_call(
        _outproj_kernel,
        grid=(t // TM,),
        in_specs=[
            pl.BlockSpec((TM, ATT_W), tile),
            pl.BlockSpec((TM, G_W), tile),
            pl.BlockSpec((TM, d), tile),
            pl.BlockSpec(w_out.shape, const),
            pl.BlockSpec((1, d), const),
            pl.BlockSpec((N_EXPERTS, d), const),
            pl.BlockSpec((N_EXPERTS, d), const),
            pl.BlockSpec((N_EXPERTS, 1), const),
        ],
        out_specs=[
            pl.BlockSpec((TM, d), tile),
            pl.BlockSpec((TM * ROW_SUB, LANES), tile),
            pl.BlockSpec((TOP_K, TM), lanes),
            pl.BlockSpec((TOP_K, TM), lanes),
            pl.BlockSpec((TOP_K, TM), lanes),
            pl.BlockSpec((N_EXPERTS, 1), const),
        ],
        out_shape=[
            jax.ShapeDtypeStruct((t, d), F32),
            jax.ShapeDtypeStruct((t * ROW_SUB, LANES), F32),
            jax.ShapeDtypeStruct((TOP_K, t), jnp.int32),
            jax.ShapeDtypeStruct((TOP_K, t), F32),
            jax.ShapeDtypeStruct((TOP_K, t), jnp.int32),
            jax.ShapeDtypeStruct((N_EXPERTS, 1), jnp.int32),
        ],
        scratch_shapes=[pltpu.VMEM((N_EXPERTS, 1), F32)],
        compiler_params=pltpu.CompilerParams(
            dimension_semantics=("arbitrary",), vmem_limit_bytes=VMEM_LIMIT),
        name="outproj_router",
    )(att, gm, x2, w_out, g_moe, wr_hi, wr_lo, br)


def _slab(ref, row, n_rows=1):
    return ref.at[pl.ds(pl.multiple_of(row * ROW_SUB, ROW_SUB), n_rows * ROW_SUB)]


def _dispatch_kernel(dest_ref, pend_ref, xp_ref, o_ref, zero_ref, sem, zsem):
    i = pl.program_id(0)
    t_total = pl.num_programs(0) * TT_DISPATCH

    @pl.when(i == 0)
    def _():
        zero_ref[...] = jnp.zeros_like(zero_ref)

        def zero_copy(e):
            return pltpu.make_async_copy(zero_ref, _slab(o_ref, pend_ref[e] - TME, TME), zsem)

        def start(e, c):
            prev = jnp.where(e > 0, pend_ref[jnp.maximum(e - 1, 0)], 0)

            @pl.when(pend_ref[e] > prev)
            def _():
                zero_copy(e).start()
            return c

        def wait(e, c):
            prev = jnp.where(e > 0, pend_ref[jnp.maximum(e - 1, 0)], 0)

            @pl.when(pend_ref[e] > prev)
            def _():
                zero_copy(e).wait()
            return c

        lax.fori_loop(0, N_EXPERTS, start, 0)
        lax.fori_loop(0, N_EXPERTS, wait, 0)

        def tail_copy(blk):
            return pltpu.make_async_copy(zero_ref, _slab(o_ref, blk * TME, TME), zsem)

        first_tail = pend_ref[N_EXPERTS - 1] // TME
        n_blocks = o_ref.shape[0] // (TME * ROW_SUB)
        lax.fori_loop(first_tail, n_blocks, lambda blk, c: (tail_copy(blk).start(), c)[1], 0)
        lax.fori_loop(first_tail, n_blocks, lambda blk, c: (tail_copy(blk).wait(), c)[1], 0)

    base = i * TT_DISPATCH

    def body(t, c):
        src = _slab(xp_ref, t)
        for k in range(TOP_K):
            d = dest_ref[k * t_total + base + t]
            pltpu.make_async_copy(src, _slab(o_ref, d), sem).start()
        return c

    lax.fori_loop(0, TT_DISPATCH, body, 0, unroll=8)
    for k in range(TOP_K):
        pltpu.make_async_copy(xp_ref, _slab(o_ref, 0, TT_DISPATCH), sem).wait()


def _dispatch(dest_flat, pad_ends, xp, n_rows):
    t = xp.shape[0] // ROW_SUB
    return pl.pallas_call(
        _dispatch_kernel,
        grid_spec=pltpu.PrefetchScalarGridSpec(
            num_scalar_prefetch=2,
            grid=(t // TT_DISPATCH,),
            in_specs=[pl.BlockSpec((TT_DISPATCH * ROW_SUB, LANES), lambda i, dest, pend: (i, 0))],
            out_specs=pl.BlockSpec(memory_space=pl.ANY),
            scratch_shapes=[pltpu.VMEM((TME * ROW_SUB, LANES), F32),
                            pltpu.SemaphoreType.DMA, pltpu.SemaphoreType.DMA],
        ),
        out_shape=jax.ShapeDtypeStruct((n_rows * ROW_SUB, LANES), F32),
        compiler_params=pltpu.CompilerParams(
            dimension_semantics=("arbitrary",), vmem_limit_bytes=VMEM_LIMIT),
        name="dispatch",
    )(dest_flat, pad_ends, xp)


def _expert_kernel(be_ref, nvb_ref, x_ref, wi_ref, bi_ref, wo_ref, bo_ref, y_ref, wib_ref, wob_ref):
    i = pl.program_id(0)
    e = be_ref[i]
    prev = be_ref[jnp.maximum(i - 1, 0)]

    @pl.when((i == 0) | (e != prev))
    def _():
        wib_ref[...] = wi_ref[0].astype(BF16)
        wob_ref[...] = wo_ref[0].astype(BF16)

    @pl.when(i < nvb_ref[0])
    def _():
        x = jnp.concatenate(
            [x_ref[pl.ds(sub, TME, stride=ROW_SUB), :].astype(BF16) for sub in range(ROW_SUB)],
            axis=1)
        hdn = jnp.dot(x, wib_ref[...], preferred_element_type=F32) + bi_ref[0]
        gl = jnp.minimum(hdn[:, :D_EXPERT], SWIGLU_LIMIT)
        lin = jnp.clip(hdn[:, D_EXPERT:], -SWIGLU_LIMIT, SWIGLU_LIMIT)
        act = gl * jax.nn.sigmoid(SWIGLU_ALPHA * gl) * (lin + 1.0)
        y = jnp.dot(act.astype(BF16), wob_ref[...], preferred_element_type=F32) + bo_ref[0]
        for sub in range(ROW_SUB):
            y_ref[pl.ds(sub, TME, stride=ROW_SUB), :] = y[:, sub * LANES:(sub + 1) * LANES]

    @pl.when(i >= nvb_ref[0])
    def _():
        y_ref[...] = jnp.zeros_like(y_ref)


def _experts(block_e, nvb, x_pad, w_in, b_in, w_out, b_out):
    n_rows = x_pad.shape[0] // ROW_SUB
    d = D_MODEL
    nb = n_rows // TME
    rows = lambda i, be, nv: (jnp.minimum(i, nv[0] - 1), 0)
    wsel = lambda i, be, nv: (be[i], 0, 0)
    return pl.pallas_call(
        _expert_kernel,
        grid_spec=pltpu.PrefetchScalarGridSpec(
            num_scalar_prefetch=2,
            grid=(nb,),
            in_specs=[
                pl.BlockSpec((TME * ROW_SUB, LANES), rows),
                pl.BlockSpec((1, d, 2 * D_EXPERT), wsel),
                pl.BlockSpec((1, 1, 2 * D_EXPERT), wsel),
                pl.BlockSpec((1, D_EXPERT, d), wsel),
                pl.BlockSpec((1, 1, d), wsel),
            ],
            out_specs=pl.BlockSpec((TME * ROW_SUB, LANES), lambda i, be, nv: (i, 0)),
            scratch_shapes=[pltpu.VMEM((d, 2 * D_EXPERT), BF16), pltpu.VMEM((D_EXPERT, d), BF16)],
        ),
        out_shape=jax.ShapeDtypeStruct((n_rows * ROW_SUB, LANES), F32),
        compiler_params=pltpu.CompilerParams(
            dimension_semantics=("arbitrary",), vmem_limit_bytes=VMEM_LIMIT),
        name="experts",
    )(block_e, nvb, x_pad, w_in, b_in, w_out, b_out)


def _combine_kernel(dest_ref, y_ref, h_ref, gate_ref, o_ref, buf_ref, sem):
    i = pl.program_id(0)
    n = pl.num_programs(0)
    t_total = n * TT_COMBINE

    def issue(step, slot):
        base = step * TT_COMBINE

        def body(t, c):
            for k in range(TOP_K):
                d = dest_ref[k * t_total + base + t]
                pltpu.make_async_copy(_slab(y_ref, d), _slab(buf_ref.at[slot, k], t),
                                      sem.at[slot]).start()
            return c

        lax.fori_loop(0, TT_COMBINE, body, 0, unroll=8)

    @pl.when(i == 0)
    def _():
        issue(0, 0)

    @pl.when(i + 1 < n)
    def _():
        issue(i + 1, (i + 1) % 2)

    slot = i % 2
    for k in range(TOP_K):
        pltpu.make_async_copy(_slab(y_ref, 0, TT_COMBINE), buf_ref.at[slot, k], sem.at[slot]).wait()
    for sub in range(ROW_SUB):
        cols = slice(sub * LANES, (sub + 1) * LANES)
        acc = h_ref[:, cols]
        for k in range(TOP_K):
            acc = acc + gate_ref[:, k:k + 1] * buf_ref[slot, k, pl.ds(sub, TT_COMBINE, stride=ROW_SUB), :]
        o_ref[:, cols] = acc


def _combine(dest_flat, y_pad, h, gates_tk):
    t, d = h.shape
    tile = lambda i, dest: (i, 0)
    return pl.pallas_call(
        _combine_kernel,
        grid_spec=pltpu.PrefetchScalarGridSpec(
            num_scalar_prefetch=1,
            grid=(t // TT_COMBINE,),
            in_specs=[
                pl.BlockSpec(memory_space=pl.ANY),
                pl.BlockSpec((TT_COMBINE, d), tile),
                pl.BlockSpec((TT_COMBINE, TOP_K), tile),
            ],
            out_specs=pl.BlockSpec((TT_COMBINE, d), tile),
            scratch_shapes=[pltpu.VMEM((2, TOP_K, TT_COMBINE * ROW_SUB, LANES), F32),
                            pltpu.SemaphoreType.DMA((2,))],
        ),
        out_shape=jax.ShapeDtypeStruct((t, d), F32),
        compiler_params=pltpu.CompilerParams(
            dimension_semantics=("arbitrary",), vmem_limit_bytes=VMEM_LIMIT),
        name="combine",
    )(dest_flat, y_pad, h, gates_tk)


def _band_bias(rel_bias):
    n_heads = rel_bias.shape[0]
    period = TQ + KWIN - 1
    first = REL_CLIP + LEFT - (KWIN - 1)
    vec = jnp.concatenate(
        [rel_bias[:, first:],
         jnp.broadcast_to(rel_bias[:, -1:], (n_heads, period - (2 * REL_CLIP + 1 - first)))], axis=1)
    shifted = jnp.tile(vec, (1, TQ + 1))[:, :TQ * (period + 1)].reshape(n_heads, TQ, period + 1)
    toeplitz = shifted[:, :, :KWIN][:, :, ::-1]
    r = jnp.arange(TQ)[:, None]
    j = jnp.arange(KWIN)[None, :]
    lo = (r // CHUNK) * CHUNK
    in_band = (j >= lo) & (j < lo + LEFT + CHUNK)
    return jnp.where(in_band[None], toeplitz.astype(F32), NEG_INF)


def kernel(x, norm_mix_g, w_in_proj, q_norm_g, k_norm_g, rel_bias, gmlp_v_norm_g, gmlp_w_s, gmlp_b_s,
           att_out_norm_g, gmlp_out_norm_g, w_out_proj, norm_moe_g, w_router, b_router,
           w_expert_in, b_expert_in, w_expert_out, b_expert_out):
    b, s, d = x.shape
    t = b * s

    head_of = jnp.arange(ATT_W) // HEAD_DIM
    block_diag = (head_of[:, None] == head_of[None, :]).astype(BF16)
    q, kt, v, gm = _inproj(
        x, norm_mix_g.reshape(1, d), w_in_proj.astype(BF16),
        jnp.tile(q_norm_g, N_HEADS).reshape(1, ATT_W), jnp.tile(k_norm_g, N_HEADS).reshape(1, ATT_W),
        block_diag, gmlp_v_norm_g, gmlp_w_s, gmlp_b_s.T, gmlp_out_norm_g.reshape(1, G_W))
    att = _attention(q, kt, v, _band_bias(rel_bias), att_out_norm_g.reshape(1, ATT_W))

    wr_t = w_router.T
    wr_hi = wr_t.astype(BF16)
    wr_lo = (wr_t - wr_hi.astype(F32)).astype(BF16)
    h, xp, idx, gates, rank, cnt = _outproj(
        att.reshape(t, ATT_W), gm.reshape(t, G_W), x.reshape(t, d), w_out_proj.astype(BF16),
        norm_moe_g.reshape(1, d), wr_hi, wr_lo, b_router.reshape(N_EXPERTS, 1))

    counts = cnt[:, 0]
    padded = ((counts + TME - 1) // TME) * TME
    pad_ends = jnp.cumsum(padded).astype(jnp.int32)
    pad_starts = pad_ends - padded
    start_of = jnp.sum(jnp.where(idx[None] == jnp.arange(N_EXPERTS)[:, None, None],
                                 pad_starts[:, None, None], 0), axis=0)
    dest = (start_of + rank).reshape(-1)
    n_blocks = (t * TOP_K) // TME + N_EXPERTS
    nvb = pad_ends[-1] // TME
    block_start = jnp.arange(n_blocks, dtype=jnp.int32) * TME
    block_e = jnp.minimum(jnp.sum(pad_ends[None, :] <= block_start[:, None], axis=1), N_EXPERTS - 1)
    block_e = jnp.where(jnp.arange(n_blocks) < nvb, block_e, block_e[nvb - 1]).astype(jnp.int32)

    x_pad = _dispatch(dest, pad_ends, xp, n_blocks * TME)
    y_pad = _experts(block_e, nvb.reshape(1), x_pad, w_expert_in,
                     b_expert_in.reshape(N_EXPERTS, 1, 2 * D_EXPERT), w_expert_out,
                     b_expert_out.reshape(N_EXPERTS, 1, d))
    out = _combine(dest, y_pad, h, gates.T)
    return out.reshape(b, s, d)
```

```python
import functools

import jax
import jax.numpy as jnp
from jax import lax
from jax.experimental import pallas as pl
from jax.experimental.pallas import tpu as pltpu

D_MODEL = 1024
CHUNK = 64
LEFT = 8 * CHUNK
N_HEADS = 8
HEAD_DIM = 64
ATT_W = 512
REL_CLIP = 128
G_GROUPS = 4
G_DIM = 128
G_BLOCK = 128
G_W = 512
N_EXPERTS = 32
TOP_K = 4
D_EXPERT = 1024
SWIGLU_LIMIT = 7.0
SWIGLU_ALPHA = 1.702
EPS = 1e-6
NEG_INF = -1e30

TM = 512
TQ = 128
KWIN = TQ + LEFT
TME = 256
TT_COMBINE = 128
VMEM_LIMIT = 56 * 1024 * 1024
LANES = 128
ROW_SUB = D_MODEL // LANES

F32 = jnp.float32
BF16 = jnp.bfloat16


def _rms(x, gain):
    ms = jnp.mean(x * x, axis=-1, keepdims=True)
    return x * lax.rsqrt(ms + EPS) * gain


def _gelu(x):
    return 0.5 * x * (1.0 + lax.erf(x * (2.0 ** -0.5)))


def _slab(ref, row, n_rows=1):
    return ref.at[pl.ds(pl.multiple_of(row * ROW_SUB, ROW_SUB), n_rows * ROW_SUB)]


def _dispatch_prologue(pend_ref, xpad_ref, zero_ref, zsem):
    zero_ref[...] = jnp.zeros_like(zero_ref)

    def last_block(e):
        return pltpu.make_async_copy(zero_ref, _slab(xpad_ref, pend_ref[e] - TME, TME), zsem)

    def has_rows(e):
        return pend_ref[e] > jnp.where(e > 0, pend_ref[jnp.maximum(e - 1, 0)], 0)

    def start(e, c):
        pl.when(has_rows(e))(lambda: last_block(e).start())
        return c

    def wait(e, c):
        pl.when(has_rows(e))(lambda: last_block(e).wait())
        return c

    lax.fori_loop(0, N_EXPERTS, start, 0)
    lax.fori_loop(0, N_EXPERTS, wait, 0)

    def tail_block(blk):
        return pltpu.make_async_copy(zero_ref, _slab(xpad_ref, blk * TME, TME), zsem)

    first_tail = pend_ref[N_EXPERTS - 1] // TME
    n_blocks = xpad_ref.shape[0] // (TME * ROW_SUB)
    lax.fori_loop(first_tail, n_blocks, lambda blk, c: (tail_block(blk).start(), c)[1], 0)
    lax.fori_loop(first_tail, n_blocks, lambda blk, c: (tail_block(blk).wait(), c)[1], 0)


def _dispatch_issue(tile, dest_ref, xs_ref, xpad_ref, sem):
    ntok = xs_ref.shape[0] // ROW_SUB
    t_total = dest_ref.shape[0] // TOP_K
    for t in range(ntok):
        for k in range(TOP_K):
            d = dest_ref[k * t_total + tile * ntok + t]
            pltpu.make_async_copy(_slab(xs_ref, t), _slab(xpad_ref, d), sem).start()


def _dispatch_wait(xs_ref, xpad_ref, sem):
    ntok = xs_ref.shape[0] // ROW_SUB
    for _ in range(TOP_K):
        pltpu.make_async_copy(xs_ref, _slab(xpad_ref, 0, ntok), sem).wait()


def _combine_issue(tile, ntok, dest_ref, ypad_ref, buf_ref, sem):
    t_total = dest_ref.shape[0] // TOP_K
    for t in range(ntok):
        for k in range(TOP_K):
            d = dest_ref[k * t_total + tile * ntok + t]
            pltpu.make_async_copy(_slab(ypad_ref, d), _slab(buf_ref.at[k], t), sem).start()


def _combine_wait(ntok, ypad_ref, buf_ref, sem):
    for k in range(TOP_K):
        pltpu.make_async_copy(_slab(ypad_ref, 0, ntok), buf_ref.at[k], sem).wait()


def _combine_sum(ntok, buf_ref, h_ref, gate_ref, o_ref):
    for sub in range(ROW_SUB):
        cols = slice(sub * LANES, (sub + 1) * LANES)
        acc = h_ref[:, cols]
        for k in range(TOP_K):
            acc = acc + gate_ref[:, k:k + 1] * buf_ref[k, pl.ds(sub, ntok, stride=ROW_SUB), :]
        o_ref[:, cols] = acc


def _inproj_kernel(x_ref, g_ref, w_ref, gq_ref, gk_ref, bd_ref, gvn_ref, ws_ref, bst_ref, gout_ref,
                   q_ref, kt_ref, v_ref, gm_ref):
    j = pl.program_id(1)

    @pl.when(j == 0)
    def _():
        kt_ref[...] = jnp.zeros_like(kt_ref)
        v_ref[...] = jnp.zeros_like(v_ref)

    @pl.when(j > 0)
    def _():
        z = _rms(x_ref[0], g_ref[...]).astype(BF16)

        def proj(c0, width):
            return jnp.dot(z, w_ref[:, c0:c0 + width], preferred_element_type=F32)

        def head_norm(t, gain):
            ssum = jnp.dot((t * t).astype(BF16), bd_ref[...], preferred_element_type=F32)
            return t * lax.rsqrt(ssum * (1.0 / HEAD_DIM) + EPS) * gain

        q = head_norm(proj(0, ATT_W), gq_ref[...]) * (HEAD_DIM ** -0.5)
        q_ref[0] = q.astype(BF16)
        k = head_norm(proj(ATT_W, ATT_W), gk_ref[...])
        kt_ref[0] = k.T.astype(BF16)
        v_ref[0] = proj(2 * ATT_W, ATT_W).astype(BF16)

        gu = _gelu(proj(3 * ATT_W, G_W))
        gv = _gelu(proj(3 * ATT_W + G_W, G_W))
        row = lax.broadcasted_iota(jnp.int32, (G_BLOCK, G_BLOCK), 0) // CHUNK
        col = lax.broadcasted_iota(jnp.int32, (G_BLOCK, G_BLOCK), 1) // CHUNK
        tri = row >= col
        cols = []
        for g in range(G_GROUPS):
            sl = slice(g * G_DIM, (g + 1) * G_DIM)
            vn = _rms(gv[:, sl], gvn_ref[g:g + 1, :]).astype(BF16)
            wm = jnp.where(tri, ws_ref[g], 0.0).astype(BF16)
            blocks = []
            for n in range(TM // G_BLOCK):
                rs = slice(n * G_BLOCK, (n + 1) * G_BLOCK)
                gate = jnp.dot(wm, vn[rs], preferred_element_type=F32) + bst_ref[:, g:g + 1]
                blocks.append(gu[rs, sl] * gate)
            cols.append(jnp.concatenate(blocks, axis=0))
        gm = jnp.concatenate(cols, axis=1)
        gm_ref[0] = _rms(gm, gout_ref[...]).astype(BF16)


def _inproj(x, g_mix, w_in, gq, gk, bd, gvn, ws, bst, gout):
    b, s, d = x.shape
    nt = s // TM
    xmap = lambda bi, j: (bi, jnp.maximum(j - 1, 0), 0)
    const2 = lambda bi, j: (0, 0)
    return pl.pallas_call(
        _inproj_kernel,
        grid=(b, nt + 1),
        in_specs=[
            pl.BlockSpec((1, TM, d), xmap),
            pl.BlockSpec((1, d), const2),
            pl.BlockSpec(w_in.shape, const2),
            pl.BlockSpec((1, ATT_W), const2),
            pl.BlockSpec((1, ATT_W), const2),
            pl.BlockSpec((ATT_W, ATT_W), const2),
            pl.BlockSpec((G_GROUPS, G_DIM), const2),
            pl.BlockSpec((G_GROUPS, G_BLOCK, G_BLOCK), lambda bi, j: (0, 0, 0)),
            pl.BlockSpec((G_BLOCK, G_GROUPS), const2),
            pl.BlockSpec((1, G_W), const2),
        ],
        out_specs=[
            pl.BlockSpec((1, TM, ATT_W), xmap),
            pl.BlockSpec((1, ATT_W, TM), lambda bi, j: (bi, 0, j)),
            pl.BlockSpec((1, TM, ATT_W), lambda bi, j: (bi, j, 0)),
            pl.BlockSpec((1, TM, G_W), xmap),
        ],
        out_shape=[
            jax.ShapeDtypeStruct((b, s, ATT_W), BF16),
            jax.ShapeDtypeStruct((b, ATT_W, s + LEFT), BF16),
            jax.ShapeDtypeStruct((b, s + LEFT, ATT_W), BF16),
            jax.ShapeDtypeStruct((b, s, G_W), BF16),
        ],
        compiler_params=pltpu.CompilerParams(
            dimension_semantics=("arbitrary", "arbitrary"), vmem_limit_bytes=VMEM_LIMIT),
        name="inproj",
    )(x, g_mix, w_in, gq, gk, bd, gvn, ws, bst, gout)


def _attn_body(q_ref, kt_ref, v_ref, bias_ref, gout_ref, o_ref):
    i = pl.program_id(1)
    q0 = pl.multiple_of(i * TQ, TQ)
    win = pl.ds(q0, KWIN)
    colpos = lax.broadcasted_iota(jnp.int32, (1, KWIN), 1)
    pad_bias = jnp.where(colpos >= LEFT - q0, 0.0, NEG_INF).astype(F32)
    lane = lax.broadcasted_iota(jnp.int32, (1, 256), 1)
    outs = []
    for g in range(2):
        gs = slice(g * 256, (g + 1) * 256)
        qg = q_ref[0, :, gs]
        ktg = kt_ref[0, gs, win]
        vg = v_ref[0, win, gs]
        acc = jnp.zeros((TQ, 256), F32)
        for hh in range(4):
            h = 4 * g + hh
            hmask = (lane >= hh * HEAD_DIM) & (lane < (hh + 1) * HEAD_DIM)
            qh = jnp.where(hmask, qg, jnp.zeros_like(qg))
            s = jnp.dot(qh, ktg, preferred_element_type=F32)
            s = s + bias_ref[h] + pad_bias
            m = jnp.max(s, axis=-1, keepdims=True)
            e = jnp.exp(s - m)
            inv = 1.0 / jnp.sum(e, axis=-1, keepdims=True)
            pv = jnp.dot(e.astype(BF16), vg, preferred_element_type=F32)
            acc = jnp.where(hmask, pv * inv, acc)
        outs.append(acc)
    att = jnp.concatenate(outs, axis=1)
    o_ref[0] = _rms(att, gout_ref[...]).astype(BF16)


def _attn_kernel(q_ref, kt_ref, v_ref, bias_ref, gout_ref, o_ref):
    _attn_body(q_ref, kt_ref, v_ref, bias_ref, gout_ref, o_ref)


def _attn_dispatch_kernel(dest_ref, pend_ref, q_ref, kt_ref, v_ref, bias_ref, gout_ref, xs_ref,
                          o_ref, xpad_ref, zero_ref, sem, zsem):
    step = pl.program_id(0) * pl.num_programs(1) + pl.program_id(1)

    @pl.when(step == 0)
    def _():
        _dispatch_prologue(pend_ref, xpad_ref, zero_ref, zsem)

    _dispatch_issue(step, dest_ref, xs_ref, xpad_ref, sem)
    _attn_body(q_ref, kt_ref, v_ref, bias_ref, gout_ref, o_ref)
    _dispatch_wait(xs_ref, xpad_ref, sem)


def _attention(q, kt, v, bias, gout, b0, nb, side=None):
    s = q.shape[1]
    n_steps = nb * (s // TQ)
    in_specs = [
        pl.BlockSpec((1, TQ, ATT_W), lambda bi, i, *_: (b0 + bi, i, 0)),
        pl.BlockSpec((1, ATT_W, s + LEFT), lambda bi, i, *_: (b0 + bi, 0, 0)),
        pl.BlockSpec((1, s + LEFT, ATT_W), lambda bi, i, *_: (b0 + bi, 0, 0)),
        pl.BlockSpec(bias.shape, lambda bi, i, *_: (0, 0, 0)),
        pl.BlockSpec((1, ATT_W), lambda bi, i, *_: (0, 0)),
    ]
    att_spec = pl.BlockSpec((1, TQ, ATT_W), lambda bi, i, *_: (bi, i, 0))
    att_shape = jax.ShapeDtypeStruct((nb, s, ATT_W), BF16)
    params = pltpu.CompilerParams(
        dimension_semantics=("arbitrary", "arbitrary"), vmem_limit_bytes=VMEM_LIMIT)
    if side is None:
        return pl.pallas_call(
            _attn_kernel, grid=(nb, s // TQ), in_specs=in_specs, out_specs=att_spec,
            out_shape=att_shape, compiler_params=params, name="attention",
        )(q, kt, v, bias, gout)
    dest, pad_ends, xs, n_pad_rows = side
    ntok = xs.shape[0] // ROW_SUB // n_steps
    xs_spec = pl.BlockSpec((ntok * ROW_SUB, LANES), lambda bi, i, *_: (bi * (s // TQ) + i, 0))
    return pl.pallas_call(
        _attn_dispatch_kernel,
        grid_spec=pltpu.PrefetchScalarGridSpec(
            num_scalar_prefetch=2,
            grid=(nb, s // TQ),
            in_specs=in_specs + [xs_spec],
            out_specs=[att_spec, pl.BlockSpec(memory_space=pl.ANY)],
            scratch_shapes=[pltpu.VMEM((TME * ROW_SUB, LANES), F32),
                            pltpu.SemaphoreType.DMA, pltpu.SemaphoreType.DMA],
        ),
        out_shape=[att_shape, jax.ShapeDtypeStruct((n_pad_rows * ROW_SUB, LANES), F32)],
        compiler_params=params,
        name="attention_dispatch",
    )(dest, pad_ends, q, kt, v, bias, gout, xs)


def _outproj_kernel(a_ref, gm_ref, x_ref, wo_ref, gmoe_ref, wrh_ref, wrl_ref, br_ref,
                    h_ref, xs_ref, idx_ref, gate_ref, rank_ref, cnt_ref, carry_ref):
    i = pl.program_id(0)

    @pl.when(i == 0)
    def _():
        carry_ref[...] = jnp.zeros_like(carry_ref)

    h = (x_ref[...]
         + jnp.dot(a_ref[...], wo_ref[:ATT_W, :], preferred_element_type=F32)
         + jnp.dot(gm_ref[...], wo_ref[ATT_W:, :], preferred_element_type=F32))
    h_ref[...] = h
    xn = _rms(h, gmoe_ref[...])
    xhi = xn.astype(BF16)
    xlo = (xn - xhi.astype(F32)).astype(BF16)
    for sub in range(ROW_SUB):
        xs_ref[pl.ds(sub, TM, stride=ROW_SUB), :] = xn[:, sub * LANES:(sub + 1) * LANES]

    nt = (((1,), (1,)), ((), ()))
    logits = (lax.dot_general(wrh_ref[...], xhi, nt, preferred_element_type=F32)
              + lax.dot_general(wrh_ref[...], xlo, nt, preferred_element_type=F32)
              + lax.dot_general(wrl_ref[...], xhi, nt, preferred_element_type=F32)
              + br_ref[...])
    eidx = lax.broadcasted_iota(jnp.int32, (N_EXPERTS, TM), 0)
    vals, idxs = [], []
    cur = logits
    for _ in range(TOP_K):
        m = jnp.max(cur, axis=0, keepdims=True)
        ik = jnp.min(jnp.where(cur == m, eidx, N_EXPERTS), axis=0, keepdims=True)
        vals.append(m)
        idxs.append(ik)
        cur = jnp.where(eidx == ik, -jnp.inf, cur)
    exps = [jnp.exp(v - vals[0]) for v in vals]
    tot = exps[0] + exps[1] + exps[2] + exps[3]
    idx_ref[...] = jnp.concatenate(idxs, axis=0)
    gate_ref[...] = jnp.concatenate([e / tot for e in exps], axis=0)

    onehots = [eidx == ik for ik in idxs]
    member = sum(oh.astype(F32) for oh in onehots)
    before = (lax.broadcasted_iota(jnp.int32, (TM, TM), 0)
              < lax.broadcasted_iota(jnp.int32, (TM, TM), 1)).astype(BF16)
    base = jnp.dot(member.astype(BF16), before, preferred_element_type=F32) + carry_ref[...]
    ranks = [jnp.sum(jnp.where(oh, base, 0.0), axis=0, keepdims=True) for oh in onehots]
    rank_ref[...] = jnp.concatenate(ranks, axis=0).astype(jnp.int32)
    carry_ref[...] += jnp.sum(member, axis=1, keepdims=True)
    cnt_ref[...] = carry_ref[...].astype(jnp.int32)


def _outproj(att, gm2, x2, tok0, w_out, g_moe, wr_hi, wr_lo, br):
    tg = att.shape[0]
    d = x2.shape[1]
    t0 = tok0 // TM
    tile = lambda i: (i, 0)
    gtile = lambda i: (t0 + i, 0)
    const = lambda i: (0, 0)
    lanes = lambda i: (0, i)
    return pl.pallas_call(
        _outproj_kernel,
        grid=(tg // TM,),
        in_specs=[
            pl.BlockSpec((TM, ATT_W), tile),
            pl.BlockSpec((TM, G_W), gtile),
            pl.BlockSpec((TM, d), gtile),
            pl.BlockSpec(w_out.shape, const),
            pl.BlockSpec((1, d), const),
            pl.BlockSpec((N_EXPERTS, d), const),
            pl.BlockSpec((N_EXPERTS, d), const),
            pl.BlockSpec((N_EXPERTS, 1), const),
        ],
        out_specs=[
            pl.BlockSpec((TM, d), tile),
            pl.BlockSpec((TM * ROW_SUB, LANES), tile),
            pl.BlockSpec((TOP_K, TM), lanes),
            pl.BlockSpec((TOP_K, TM), lanes),
            pl.BlockSpec((TOP_K, TM), lanes),
            pl.BlockSpec((N_EXPERTS, 1), const),
        ],
        out_shape=[
            jax.ShapeDtypeStruct((tg, d), F32),
            jax.ShapeDtypeStruct((tg * ROW_SUB, LANES), F32),
            jax.ShapeDtypeStruct((TOP_K, tg), jnp.int32),
            jax.ShapeDtypeStruct((TOP_K, tg), F32),
            jax.ShapeDtypeStruct((TOP_K, tg), jnp.int32),
            jax.ShapeDtypeStruct((N_EXPERTS, 1), jnp.int32),
        ],
        scratch_shapes=[pltpu.VMEM((N_EXPERTS, 1), F32)],
        compiler_params=pltpu.CompilerParams(
            dimension_semantics=("arbitrary",), vmem_limit_bytes=VMEM_LIMIT),
        name="outproj_router",
    )(att, gm2, x2, w_out, g_moe, wr_hi, wr_lo, br)


def _expert_kernel(*refs, side, n_side):
    if side == "dispatch":
        (be_ref, nvb_ref, dest_ref, pend_ref, x_ref, wi_ref, bi_ref, wo_ref, bo_ref, xs_ref,
         y_ref, xpad_ref, wib_ref, wob_ref, zero_ref, sem, zsem) = refs
    else:
        (be_ref, nvb_ref, dest_ref, x_ref, wi_ref, bi_ref, wo_ref, bo_ref, ypad_ref, h_ref, gate_ref,
         y_ref, o_ref, wib_ref, wob_ref, buf_ref, sem) = refs
    i = pl.program_id(0)
    e = be_ref[i]
    prev = be_ref[jnp.maximum(i - 1, 0)]
    tile = jnp.minimum(i, n_side - 1)

    @pl.when((i == 0) | (e != prev))
    def _():
        wib_ref[...] = wi_ref[0].astype(BF16)
        wob_ref[...] = wo_ref[0].astype(BF16)

    if side == "dispatch":
        @pl.when(i == 0)
        def _():
            _dispatch_prologue(pend_ref, xpad_ref, zero_ref, zsem)
    else:
        ntok = h_ref.shape[0]

        @pl.when(i == 0)
        def _():
            _combine_issue(0, ntok, dest_ref, ypad_ref, buf_ref.at[0], sem.at[0])

        @pl.when(i == pl.num_programs(0) - 1)
        def _():
            last = nvb_ref[0] % 2
            for k in range(TOP_K):
                pltpu.make_async_copy(_slab(ypad_ref, 0, ntok), buf_ref.at[last, k], sem.at[last]).wait()

    def block(slot):
        if side == "dispatch":
            _dispatch_issue(tile, dest_ref, xs_ref, xpad_ref, sem)
        else:
            _combine_wait(ntok, ypad_ref, buf_ref.at[slot], sem.at[slot])
            _combine_issue(jnp.minimum(i + 1, n_side - 1), ntok, dest_ref, ypad_ref,
                           buf_ref.at[1 - slot], sem.at[1 - slot])
            _combine_sum(ntok, buf_ref.at[slot], h_ref, gate_ref, o_ref)
        x = jnp.concatenate(
            [x_ref[pl.ds(sub, TME, stride=ROW_SUB), :].astype(BF16) for sub in range(ROW_SUB)],
            axis=1)
        hdn = jnp.dot(x, wib_ref[...], preferred_element_type=F32) + bi_ref[0]
        gl = jnp.minimum(hdn[:, :D_EXPERT], SWIGLU_LIMIT)
        lin = jnp.clip(hdn[:, D_EXPERT:], -SWIGLU_LIMIT, SWIGLU_LIMIT)
        act = gl * jax.nn.sigmoid(SWIGLU_ALPHA * gl) * (lin + 1.0)
        y = jnp.dot(act.astype(BF16), wob_ref[...], preferred_element_type=F32) + bo_ref[0]
        for sub in range(ROW_SUB):
            y_ref[pl.ds(sub, TME, stride=ROW_SUB), :] = y[:, sub * LANES:(sub + 1) * LANES]
        if side == "dispatch":
            _dispatch_wait(xs_ref, xpad_ref, sem)

    valid = i < nvb_ref[0]
    if side == "dispatch":
        pl.when(valid)(lambda: block(None))
    else:
        pl.when(valid & (i % 2 == 0))(lambda: block(0))
        pl.when(valid & (i % 2 == 1))(lambda: block(1))

    @pl.when(i >= nvb_ref[0])
    def _():
        y_ref[...] = jnp.zeros_like(y_ref)


def _experts(block_e, nvb, x_pad, w_in, b_in, w_out, b_out, side, side_args):
    n_rows = x_pad.shape[0] // ROW_SUB
    d = D_MODEL
    nb = n_rows // TME
    n_side = (n_rows - N_EXPERTS * TME) // TME
    rows = lambda i, be, nv, *_: (jnp.minimum(i, nv[0] - 1), 0)
    wsel = lambda i, be, nv, *_: (be[i], 0, 0)
    side_tile = lambda i, *_: (jnp.minimum(i, n_side - 1), 0)
    in_specs = [
        pl.BlockSpec((TME * ROW_SUB, LANES), rows),
        pl.BlockSpec((1, d, 2 * D_EXPERT), wsel),
        pl.BlockSpec((1, 1, 2 * D_EXPERT), wsel),
        pl.BlockSpec((1, D_EXPERT, d), wsel),
        pl.BlockSpec((1, 1, d), wsel),
    ]
    y_spec = pl.BlockSpec((TME * ROW_SUB, LANES), lambda i, *_: (i, 0))
    y_shape = jax.ShapeDtypeStruct((n_rows * ROW_SUB, LANES), F32)
    weights_bf16 = [pltpu.VMEM((d, 2 * D_EXPERT), BF16), pltpu.VMEM((D_EXPERT, d), BF16)]
    if side == "dispatch":
        dest, pad_ends, xs, n_pad_rows = side_args
        ntok = xs.shape[0] // ROW_SUB // n_side
        prefetch = (block_e, nvb, dest, pad_ends)
        operands = (x_pad, w_in, b_in, w_out, b_out, xs)
        in_specs = in_specs + [pl.BlockSpec((ntok * ROW_SUB, LANES), side_tile)]
        out_specs = [y_spec, pl.BlockSpec(memory_space=pl.ANY)]
        out_shape = [y_shape, jax.ShapeDtypeStruct((n_pad_rows * ROW_SUB, LANES), F32)]
        scratch = weights_bf16 + [pltpu.VMEM((TME * ROW_SUB, LANES), F32),
                                  pltpu.SemaphoreType.DMA, pltpu.SemaphoreType.DMA]
    else:
        dest, y_other, h_other, gates_other, t_out = side_args
        ntok = h_other.shape[0] // n_side
        prefetch = (block_e, nvb, dest)
        operands = (x_pad, w_in, b_in, w_out, b_out, y_other, h_other, gates_other)
        in_specs = in_specs + [pl.BlockSpec(memory_space=pl.ANY),
                               pl.BlockSpec((ntok, d), side_tile),
                               pl.BlockSpec((ntok, TOP_K), side_tile)]
        out_specs = [y_spec, pl.BlockSpec((ntok, d), side_tile)]
        out_shape = [y_shape, jax.ShapeDtypeStruct((t_out, d), F32)]
        scratch = weights_bf16 + [pltpu.VMEM((2, TOP_K, ntok * ROW_SUB, LANES), F32),
                                  pltpu.SemaphoreType.DMA((2,))]
    return pl.pallas_call(
        functools.partial(_expert_kernel, side=side, n_side=n_side),
        grid_spec=pltpu.PrefetchScalarGridSpec(
            num_scalar_prefetch=len(prefetch),
            grid=(nb,),
            in_specs=in_specs,
            out_specs=out_specs,
            scratch_shapes=scratch,
        ),
        out_shape=out_shape,
        compiler_params=pltpu.CompilerParams(
            dimension_semantics=("arbitrary",), vmem_limit_bytes=VMEM_LIMIT),
        name="experts_" + side,
    )(*prefetch, *operands)


def _combine_kernel(dest_ref, y_ref, h_ref, gate_ref, part_ref, o_ref, buf_ref, sem):
    del part_ref
    i = pl.program_id(0)
    n = pl.num_programs(0)

    @pl.when(i == 0)
    def _():
        _combine_issue(0, TT_COMBINE, dest_ref, y_ref, buf_ref.at[0], sem.at[0])

    @pl.when(i + 1 < n)
    def _():
        nxt = (i + 1) % 2
        _combine_issue(i + 1, TT_COMBINE, dest_ref, y_ref, buf_ref.at[nxt], sem.at[nxt])

    slot = i % 2
    _combine_wait(TT_COMBINE, y_ref, buf_ref.at[slot], sem.at[slot])
    _combine_sum(TT_COMBINE, buf_ref.at[slot], h_ref, gate_ref, o_ref)


def _combine(dest_flat, y_pad, h, gates_tk, out_part, tok0):
    tg, d = h.shape
    t0 = tok0 // TT_COMBINE
    tile = lambda i, dest: (i, 0)
    return pl.pallas_call(
        _combine_kernel,
        grid_spec=pltpu.PrefetchScalarGridSpec(
            num_scalar_prefetch=1,
            grid=(tg // TT_COMBINE,),
            in_specs=[
                pl.BlockSpec(memory_space=pl.ANY),
                pl.BlockSpec((TT_COMBINE, d), tile),
                pl.BlockSpec((TT_COMBINE, TOP_K), tile),
                pl.BlockSpec(memory_space=pl.ANY),
            ],
            out_specs=pl.BlockSpec((TT_COMBINE, d), lambda i, dest: (t0 + i, 0)),
            scratch_shapes=[pltpu.VMEM((2, TOP_K, TT_COMBINE * ROW_SUB, LANES), F32),
                            pltpu.SemaphoreType.DMA((2,))],
        ),
        out_shape=jax.ShapeDtypeStruct(out_part.shape, F32),
        input_output_aliases={4: 0},
        compiler_params=pltpu.CompilerParams(
            dimension_semantics=("arbitrary",), vmem_limit_bytes=VMEM_LIMIT),
        name="combine",
    )(dest_flat, y_pad, h, gates_tk, out_part)


def _band_bias(rel_bias):
    n_heads = rel_bias.shape[0]
    period = TQ + KWIN - 1
    first = REL_CLIP + LEFT - (KWIN - 1)
    vec = jnp.concatenate(
        [rel_bias[:, first:],
         jnp.broadcast_to(rel_bias[:, -1:], (n_heads, period - (2 * REL_CLIP + 1 - first)))], axis=1)
    shifted = jnp.tile(vec, (1, TQ + 1))[:, :TQ * (period + 1)].reshape(n_heads, TQ, period + 1)
    toeplitz = shifted[:, :, :KWIN][:, :, ::-1]
    r = jnp.arange(TQ)[:, None]
    j = jnp.arange(KWIN)[None, :]
    lo = (r // CHUNK) * CHUNK
    in_band = (j >= lo) & (j < lo + LEFT + CHUNK)
    return jnp.where(in_band[None], toeplitz.astype(F32), NEG_INF)


def _block_layout(idx, rank, cnt):
    tg = idx.shape[1]
    counts = cnt[:, 0]
    padded = ((counts + TME - 1) // TME) * TME
    pad_ends = jnp.cumsum(padded).astype(jnp.int32)
    pad_starts = pad_ends - padded
    start_of = jnp.sum(jnp.where(idx[None] == jnp.arange(N_EXPERTS)[:, None, None],
                                 pad_starts[:, None, None], 0), axis=0)
    dest = (start_of + rank).reshape(-1)
    n_blocks = (tg * TOP_K) // TME + N_EXPERTS
    nvb = pad_ends[-1] // TME
    block_start = jnp.arange(n_blocks, dtype=jnp.int32) * TME
    block_e = jnp.minimum(jnp.sum(pad_ends[None, :] <= block_start[:, None], axis=1), N_EXPERTS - 1)
    block_e = jnp.where(jnp.arange(n_blocks) < nvb, block_e, block_e[nvb - 1]).astype(jnp.int32)
    return dest, pad_ends, block_e, nvb.reshape(1), n_blocks * TME


def kernel(x, norm_mix_g, w_in_proj, q_norm_g, k_norm_g, rel_bias, gmlp_v_norm_g, gmlp_w_s, gmlp_b_s,
           att_out_norm_g, gmlp_out_norm_g, w_out_proj, norm_moe_g, w_router, b_router,
           w_expert_in, b_expert_in, w_expert_out, b_expert_out):
    b, s, d = x.shape
    t = b * s
    nb1 = b // 2
    tg1 = nb1 * s

    head_of = jnp.arange(ATT_W) // HEAD_DIM
    block_diag = (head_of[:, None] == head_of[None, :]).astype(BF16)
    q, kt, v, gm = _inproj(
        x, norm_mix_g.reshape(1, d), w_in_proj.astype(BF16),
        jnp.tile(q_norm_g, N_HEADS).reshape(1, ATT_W), jnp.tile(k_norm_g, N_HEADS).reshape(1, ATT_W),
        block_diag, gmlp_v_norm_g, gmlp_w_s, gmlp_b_s.T, gmlp_out_norm_g.reshape(1, G_W))
    bias = _band_bias(rel_bias)
    g_att = att_out_norm_g.reshape(1, ATT_W)

    wr_t = w_router.T
    wr_hi = wr_t.astype(BF16)
    wr_lo = (wr_t - wr_hi.astype(F32)).astype(BF16)
    x2 = x.reshape(t, d)
    gm2 = gm.reshape(t, G_W)
    w_out_bf = w_out_proj.astype(BF16)
    router = (w_out_bf, norm_moe_g.reshape(1, d), wr_hi, wr_lo, b_router.reshape(N_EXPERTS, 1))
    b_in = b_expert_in.reshape(N_EXPERTS, 1, 2 * D_EXPERT)
    b_out = b_expert_out.reshape(N_EXPERTS, 1, d)

    att1 = _attention(q, kt, v, bias, g_att, 0, nb1)
    h1, xs1, idx1, gates1, rank1, cnt1 = _outproj(att1.reshape(tg1, ATT_W), gm2, x2, 0, *router)
    dest1, pend1, be1, nvb1, rows1 = _block_layout(idx1, rank1, cnt1)

    att2, xpad1 = _attention(q, kt, v, bias, g_att, nb1, b - nb1, side=(dest1, pend1, xs1, rows1))
    h2, xs2, idx2, gates2, rank2, cnt2 = _outproj(att2.reshape(t - tg1, ATT_W), gm2, x2, tg1, *router)
    dest2, pend2, be2, nvb2, rows2 = _block_layout(idx2, rank2, cnt2)

    ypad1, xpad2 = _experts(be1, nvb1, xpad1, w_expert_in, b_in, w_expert_out, b_out,
                            "dispatch", (dest2, pend2, xs2, rows2))
    ypad2, out_part = _experts(be2, nvb2, xpad2, w_expert_in, b_in, w_expert_out, b_out,
                               "combine", (dest1, ypad1, h1, gates1.T, t))
    out = _combine(dest2, ypad2, h2, gates2.T, out_part, tg1)
    return out.reshape(b, s, d)
```

```python
import jax
import jax.numpy as jnp
from jax import lax
from jax.experimental import pallas as pl
from jax.experimental.pallas import tpu as pltpu

D_MODEL = 1024
CHUNK = 64
LEFT = 8 * CHUNK
N_HEADS = 8
HEAD_DIM = 64
ATT_W = 512
REL_CLIP = 128
G_GROUPS = 4
G_DIM = 128
G_BLOCK = 128
G_W = 512
N_EXPERTS = 32
TOP_K = 4
D_EXPERT = 1024
SWIGLU_LIMIT = 7.0
SWIGLU_ALPHA = 1.702
EPS = 1e-6
NEG_INF = -1e30

TM = 512
TQ = 128
KWIN = TQ + LEFT
TME = 256
N_SLOTS = TM * TOP_K
PCHUNK = 256
SEG_BITS = TM.bit_length()
VMEM_LIMIT = 56 * 1024 * 1024
LANES = 128
ROW_SUB = D_MODEL // LANES

F32 = jnp.float32
BF16 = jnp.bfloat16


def _rms(x, gain):
    ms = jnp.mean(x * x, axis=-1, keepdims=True)
    return x * lax.rsqrt(ms + EPS) * gain


def _gelu(x):
    return 0.5 * x * (1.0 + lax.erf(x * (2.0 ** -0.5)))


def _slab(ref, row, n_rows=1):
    return ref.at[pl.ds(pl.multiple_of(row * ROW_SUB, ROW_SUB), n_rows * ROW_SUB)]


def _segment_copies(tile, cnt_ref, loc_ref, glb_ref, make_copy):
    def per_expert(e, c):
        n = cnt_ref[tile * N_EXPERTS + e]
        loc = loc_ref[tile * N_EXPERTS + e]
        glb = glb_ref[tile * N_EXPERTS + e]
        for bit in range(SEG_BITS):
            size = 1 << bit
            off = (n >> (bit + 1)) << (bit + 1)

            @pl.when((n & size) != 0)
            def _():
                make_copy(loc + off, glb + off, size).start()
        return c

    lax.fori_loop(0, N_EXPERTS, per_expert, 0)


def _inproj_kernel(x_ref, g_ref, w_ref, gq_ref, gk_ref, bd_ref, gvn_ref, ws_ref, bst_ref, gout_ref,
                   q_ref, kt_ref, v_ref, gm_ref):
    j = pl.program_id(1)

    @pl.when(j == 0)
    def _():
        kt_ref[...] = jnp.zeros_like(kt_ref)
        v_ref[...] = jnp.zeros_like(v_ref)

    @pl.when(j > 0)
    def _():
        z = _rms(x_ref[0], g_ref[...]).astype(BF16)

        def proj(c0, width):
            return jnp.dot(z, w_ref[:, c0:c0 + width], preferred_element_type=F32)

        def head_norm(t, gain):
            ssum = jnp.dot((t * t).astype(BF16), bd_ref[...], preferred_element_type=F32)
            return t * lax.rsqrt(ssum * (1.0 / HEAD_DIM) + EPS) * gain

        q = head_norm(proj(0, ATT_W), gq_ref[...]) * (HEAD_DIM ** -0.5)
        q_ref[0] = q.astype(BF16)
        k = head_norm(proj(ATT_W, ATT_W), gk_ref[...])
        kt_ref[0] = k.T.astype(BF16)
        v_ref[0] = proj(2 * ATT_W, ATT_W).astype(BF16)

        gu = _gelu(proj(3 * ATT_W, G_W))
        gv = _gelu(proj(3 * ATT_W + G_W, G_W))
        row = lax.broadcasted_iota(jnp.int32, (G_BLOCK, G_BLOCK), 0) // CHUNK
        col = lax.broadcasted_iota(jnp.int32, (G_BLOCK, G_BLOCK), 1) // CHUNK
        tri = row >= col
        cols = []
        for g in range(G_GROUPS):
            sl = slice(g * G_DIM, (g + 1) * G_DIM)
            vn = _rms(gv[:, sl], gvn_ref[g:g + 1, :]).astype(BF16)
            wm = jnp.where(tri, ws_ref[g], 0.0).astype(BF16)
            blocks = []
            for n in range(TM // G_BLOCK):
                rs = slice(n * G_BLOCK, (n + 1) * G_BLOCK)
                gate = jnp.dot(wm, vn[rs], preferred_element_type=F32) + bst_ref[:, g:g + 1]
                blocks.append(gu[rs, sl] * gate)
            cols.append(jnp.concatenate(blocks, axis=0))
        gm = jnp.concatenate(cols, axis=1)
        gm_ref[0] = _rms(gm, gout_ref[...]).astype(BF16)


def _inproj(x, g_mix, w_in, gq, gk, bd, gvn, ws, bst, gout):
    b, s, d = x.shape
    nt = s // TM
    xmap = lambda bi, j: (bi, jnp.maximum(j - 1, 0), 0)
    const2 = lambda bi, j: (0, 0)
    return pl.pallas_call(
        _inproj_kernel,
        grid=(b, nt + 1),
        in_specs=[
            pl.BlockSpec((1, TM, d), xmap),
            pl.BlockSpec((1, d), const2),
            pl.BlockSpec(w_in.shape, const2),
            pl.BlockSpec((1, ATT_W), const2),
            pl.BlockSpec((1, ATT_W), const2),
            pl.BlockSpec((ATT_W, ATT_W), const2),
            pl.BlockSpec((G_GROUPS, G_DIM), const2),
            pl.BlockSpec((G_GROUPS, G_BLOCK, G_BLOCK), lambda bi, j: (0, 0, 0)),
            pl.BlockSpec((G_BLOCK, G_GROUPS), const2),
            pl.BlockSpec((1, G_W), const2),
        ],
        out_specs=[
            pl.BlockSpec((1, TM, ATT_W), xmap),
            pl.BlockSpec((1, ATT_W, TM), lambda bi, j: (bi, 0, j)),
            pl.BlockSpec((1, TM, ATT_W), lambda bi, j: (bi, j, 0)),
            pl.BlockSpec((1, TM, G_W), xmap),
        ],
        out_shape=[
            jax.ShapeDtypeStruct((b, s, ATT_W), BF16),
            jax.ShapeDtypeStruct((b, ATT_W, s + LEFT), BF16),
            jax.ShapeDtypeStruct((b, s + LEFT, ATT_W), BF16),
            jax.ShapeDtypeStruct((b, s, G_W), BF16),
        ],
        compiler_params=pltpu.CompilerParams(
            dimension_semantics=("arbitrary", "arbitrary"), vmem_limit_bytes=VMEM_LIMIT),
        name="inproj",
    )(x, g_mix, w_in, gq, gk, bd, gvn, ws, bst, gout)


def _attn_kernel(q_ref, kt_ref, v_ref, bias_ref, gout_ref, o_ref):
    i = pl.program_id(1)
    q0 = pl.multiple_of(i * TQ, TQ)
    win = pl.ds(q0, KWIN)
    colpos = lax.broadcasted_iota(jnp.int32, (1, KWIN), 1)
    pad_bias = jnp.where(colpos >= LEFT - q0, 0.0, NEG_INF).astype(F32)
    lane = lax.broadcasted_iota(jnp.int32, (1, 256), 1)
    outs = []
    for g in range(2):
        gs = slice(g * 256, (g + 1) * 256)
        qg = q_ref[0, :, gs]
        ktg = kt_ref[0, gs, win]
        vg = v_ref[0, win, gs]
        acc = jnp.zeros((TQ, 256), F32)
        for hh in range(4):
            h = 4 * g + hh
            hmask = (lane >= hh * HEAD_DIM) & (lane < (hh + 1) * HEAD_DIM)
            qh = jnp.where(hmask, qg, jnp.zeros_like(qg))
            s = jnp.dot(qh, ktg, preferred_element_type=F32)
            s = s + bias_ref[h] + pad_bias
            m = jnp.max(s, axis=-1, keepdims=True)
            e = jnp.exp(s - m)
            inv = 1.0 / jnp.sum(e, axis=-1, keepdims=True)
            pv = jnp.dot(e.astype(BF16), vg, preferred_element_type=F32)
            acc = jnp.where(hmask, pv * inv, acc)
        outs.append(acc)
    att = jnp.concatenate(outs, axis=1)
    o_ref[0] = _rms(att, gout_ref[...]).astype(BF16)


def _attention(q, kt, v, bias, gout):
    b, s, _ = q.shape
    return pl.pallas_call(
        _attn_kernel,
        grid=(b, s // TQ),
        in_specs=[
            pl.BlockSpec((1, TQ, ATT_W), lambda bi, i: (bi, i, 0)),
            pl.BlockSpec((1, ATT_W, s + LEFT), lambda bi, i: (bi, 0, 0)),
            pl.BlockSpec((1, s + LEFT, ATT_W), lambda bi, i: (bi, 0, 0)),
            pl.BlockSpec(bias.shape, lambda bi, i: (0, 0, 0)),
            pl.BlockSpec((1, ATT_W), lambda bi, i: (0, 0)),
        ],
        out_specs=pl.BlockSpec((1, TQ, ATT_W), lambda bi, i: (bi, i, 0)),
        out_shape=jax.ShapeDtypeStruct((b, s, ATT_W), BF16),
        compiler_params=pltpu.CompilerParams(
            dimension_semantics=("arbitrary", "arbitrary"), vmem_limit_bytes=VMEM_LIMIT),
        name="attention",
    )(q, kt, v, bias, gout)


def _outproj_kernel(a_ref, gm_ref, x_ref, wo_ref, gmoe_ref, wrh_ref, wrl_ref, br_ref,
                    h_ref, xn_ref, slot_ref, gate_ref, cnt_ref):
    h = (x_ref[...]
         + jnp.dot(a_ref[...], wo_ref[:ATT_W, :], preferred_element_type=F32)
         + jnp.dot(gm_ref[...], wo_ref[ATT_W:, :], preferred_element_type=F32))
    h_ref[...] = h
    xn = _rms(h, gmoe_ref[...])
    xhi = xn.astype(BF16)
    xlo = (xn - xhi.astype(F32)).astype(BF16)
    xn_ref[...] = xhi

    nt = (((1,), (1,)), ((), ()))
    logits = (lax.dot_general(wrh_ref[...], xhi, nt, preferred_element_type=F32)
              + lax.dot_general(wrh_ref[...], xlo, nt, preferred_element_type=F32)
              + lax.dot_general(wrl_ref[...], xhi, nt, preferred_element_type=F32)
              + br_ref[...])
    eidx = lax.broadcasted_iota(jnp.int32, (N_EXPERTS, TM), 0)
    vals, idxs = [], []
    cur = logits
    for _ in range(TOP_K):
        m = jnp.max(cur, axis=0, keepdims=True)
        ik = jnp.min(jnp.where(cur == m, eidx, N_EXPERTS), axis=0, keepdims=True)
        vals.append(m)
        idxs.append(ik)
        cur = jnp.where(eidx == ik, -jnp.inf, cur)
    exps = [jnp.exp(v - vals[0]) for v in vals]
    tot = exps[0] + exps[1] + exps[2] + exps[3]
    gate_ref[...] = jnp.concatenate([e / tot for e in exps], axis=0)

    onehots = [eidx == ik for ik in idxs]
    member = sum(oh.astype(F32) for oh in onehots).astype(BF16)
    earlier_tok = (lax.broadcasted_iota(jnp.int32, (TM, TM), 0)
                   < lax.broadcasted_iota(jnp.int32, (TM, TM), 1)).astype(BF16)
    earlier_exp = (lax.broadcasted_iota(jnp.int32, (N_EXPERTS, N_EXPERTS), 1)
                   < lax.broadcasted_iota(jnp.int32, (N_EXPERTS, N_EXPERTS), 0)).astype(BF16)
    rank = jnp.dot(member, earlier_tok, preferred_element_type=F32)
    lower = jnp.sum(jnp.dot(earlier_exp, member, preferred_element_type=F32), axis=1, keepdims=True)
    base = rank + lower
    slots = [jnp.sum(jnp.where(oh, base, 0.0), axis=0, keepdims=True) for oh in onehots]
    slot_ref[...] = jnp.concatenate(slots, axis=0).astype(jnp.int32)
    cnt_ref[0] = jnp.sum(member.astype(F32), axis=1, keepdims=True).astype(jnp.int32)


def _outproj(att, gm, x2, w_out, g_moe, wr_hi, wr_lo, br):
    t, d = x2.shape
    tile = lambda i: (i, 0)
    const = lambda i: (0, 0)
    lanes = lambda i: (0, i)
    return pl.pallas_call(
        _outproj_kernel,
        grid=(t // TM,),
        in_specs=[
            pl.BlockSpec((TM, ATT_W), tile),
            pl.BlockSpec((TM, G_W), tile),
            pl.BlockSpec((TM, d), tile),
            pl.BlockSpec(w_out.shape, const),
            pl.BlockSpec((1, d), const),
            pl.BlockSpec((N_EXPERTS, d), const),
            pl.BlockSpec((N_EXPERTS, d), const),
            pl.BlockSpec((N_EXPERTS, 1), const),
        ],
        out_specs=[
            pl.BlockSpec((TM, d), tile),
            pl.BlockSpec((TM, d), tile),
            pl.BlockSpec((TOP_K, TM), lanes),
            pl.BlockSpec((TOP_K, TM), lanes),
            pl.BlockSpec((1, N_EXPERTS, 1), lambda i: (i, 0, 0)),
        ],
        out_shape=[
            jax.ShapeDtypeStruct((t, d), F32),
            jax.ShapeDtypeStruct((t, d), BF16),
            jax.ShapeDtypeStruct((TOP_K, t), jnp.int32),
            jax.ShapeDtypeStruct((TOP_K, t), F32),
            jax.ShapeDtypeStruct((t // TM, N_EXPERTS, 1), jnp.int32),
        ],
        compiler_params=pltpu.CompilerParams(
            dimension_semantics=("arbitrary",), vmem_limit_bytes=VMEM_LIMIT),
        name="outproj_router",
    )(att, gm, x2, w_out, g_moe, wr_hi, wr_lo, br)


def _dispatch_kernel(cnt_ref, loc_ref, glb_ref, pend_ref, xn_ref, slot_ref, xpad_ref,
                     sorted_ref, zero_ref, sem, zsem):
    i = pl.program_id(0)
    n = pl.num_programs(0)
    buf = i % 2

    def drain(b):
        pltpu.make_async_copy(sorted_ref.at[b], _slab(xpad_ref, 0, N_SLOTS), sem.at[b]).wait()

    @pl.when(i == 0)
    def _():
        zero_ref[...] = jnp.zeros_like(zero_ref)

        def last_block(e):
            return pltpu.make_async_copy(zero_ref, _slab(xpad_ref, pend_ref[e] - TME, TME), zsem)

        def has_rows(e):
            return pend_ref[e] > jnp.where(e > 0, pend_ref[jnp.maximum(e - 1, 0)], 0)

        def start(e, c):
            pl.when(has_rows(e))(lambda: last_block(e).start())
            return c

        def wait(e, c):
            pl.when(has_rows(e))(lambda: last_block(e).wait())
            return c

        lax.fori_loop(0, N_EXPERTS, start, 0)
        lax.fori_loop(0, N_EXPERTS, wait, 0)

        def tail_block(blk):
            return pltpu.make_async_copy(zero_ref, _slab(xpad_ref, blk * TME, TME), zsem)

        first_tail = pend_ref[N_EXPERTS - 1] // TME
        n_blocks = xpad_ref.shape[0] // (TME * ROW_SUB)
        lax.fori_loop(first_tail, n_blocks, lambda blk, c: (tail_block(blk).start(), c)[1], 0)
        lax.fori_loop(first_tail, n_blocks, lambda blk, c: (tail_block(blk).wait(), c)[1], 0)

    pl.when(i >= 2)(lambda: drain(buf))

    xn = xn_ref[...]
    for c in range(N_SLOTS // PCHUNK):
        pos = lax.broadcasted_iota(jnp.int32, (PCHUNK, TM), 0) + c * PCHUNK
        hit = pos == slot_ref[0:1, :]
        for k in range(1, TOP_K):
            hit = hit | (pos == slot_ref[k:k + 1, :])
        rows = jnp.dot(hit.astype(F32).astype(BF16), xn, preferred_element_type=F32)
        for sub in range(ROW_SUB):
            sorted_ref[buf, pl.ds(c * PCHUNK * ROW_SUB + sub, PCHUNK, stride=ROW_SUB), :] = (
                rows[:, sub * LANES:(sub + 1) * LANES])

    _segment_copies(
        i, cnt_ref, loc_ref, glb_ref,
        lambda loc, glb, size: pltpu.make_async_copy(
            _slab(sorted_ref.at[buf], loc, size), _slab(xpad_ref, glb, size), sem.at[buf]))

    @pl.when(i == n - 1)
    def _():
        drain(buf)
        pl.when(n >= 2)(lambda: drain(1 - buf))


def _dispatch(tcnt, tloc, tglb, pad_ends, xn, slots, n_rows):
    t, d = xn.shape
    return pl.pallas_call(
        _dispatch_kernel,
        grid_spec=pltpu.PrefetchScalarGridSpec(
            num_scalar_prefetch=4,
            grid=(t // TM,),
            in_specs=[pl.BlockSpec((TM, d), lambda i, *_: (i, 0)),
                      pl.BlockSpec((TOP_K, TM), lambda i, *_: (0, i))],
            out_specs=pl.BlockSpec(memory_space=pl.ANY),
            scratch_shapes=[pltpu.VMEM((2, N_SLOTS * ROW_SUB, LANES), F32),
                            pltpu.VMEM((TME * ROW_SUB, LANES), F32),
                            pltpu.SemaphoreType.DMA((2,)), pltpu.SemaphoreType.DMA],
        ),
        out_shape=jax.ShapeDtypeStruct((n_rows * ROW_SUB, LANES), F32),
        compiler_params=pltpu.CompilerParams(
            dimension_semantics=("arbitrary",), vmem_limit_bytes=VMEM_LIMIT),
        name="dispatch",
    )(tcnt, tloc, tglb, pad_ends, xn, slots)


def _expert_kernel(be_ref, nvb_ref, x_ref, wi_ref, bi_ref, wo_ref, bo_ref, y_ref, wib_ref, wob_ref):
    i = pl.program_id(0)
    e = be_ref[i]
    prev = be_ref[jnp.maximum(i - 1, 0)]

    @pl.when((i == 0) | (e != prev))
    def _():
        wib_ref[...] = wi_ref[0].astype(BF16)
        wob_ref[...] = wo_ref[0].astype(BF16)

    @pl.when(i < nvb_ref[0])
    def _():
        x = jnp.concatenate(
            [x_ref[pl.ds(sub, TME, stride=ROW_SUB), :].astype(BF16) for sub in range(ROW_SUB)],
            axis=1)
        hdn = jnp.dot(x, wib_ref[...], preferred_element_type=F32) + bi_ref[0]
        gl = jnp.minimum(hdn[:, :D_EXPERT], SWIGLU_LIMIT)
        lin = jnp.clip(hdn[:, D_EXPERT:], -SWIGLU_LIMIT, SWIGLU_LIMIT)
        act = gl * jax.nn.sigmoid(SWIGLU_ALPHA * gl) * (lin + 1.0)
        y = jnp.dot(act.astype(BF16), wob_ref[...], preferred_element_type=F32) + bo_ref[0]
        for sub in range(ROW_SUB):
            y_ref[pl.ds(sub, TME, stride=ROW_SUB), :] = y[:, sub * LANES:(sub + 1) * LANES]

    @pl.when(i >= nvb_ref[0])
    def _():
        y_ref[...] = jnp.zeros_like(y_ref)


def _experts(block_e, nvb, x_pad, w_in, b_in, w_out, b_out):
    n_rows = x_pad.shape[0] // ROW_SUB
    d = D_MODEL
    nb = n_rows // TME
    rows = lambda i, be, nv: (jnp.minimum(i, nv[0] - 1), 0)
    wsel = lambda i, be, nv: (be[i], 0, 0)
    return pl.pallas_call(
        _expert_kernel,
        grid_spec=pltpu.PrefetchScalarGridSpec(
            num_scalar_prefetch=2,
            grid=(nb,),
            in_specs=[
                pl.BlockSpec((TME * ROW_SUB, LANES), rows),
                pl.BlockSpec((1, d, 2 * D_EXPERT), wsel),
                pl.BlockSpec((1, 1, 2 * D_EXPERT), wsel),
                pl.BlockSpec((1, D_EXPERT, d), wsel),
                pl.BlockSpec((1, 1, d), wsel),
            ],
            out_specs=pl.BlockSpec((TME * ROW_SUB, LANES), lambda i, be, nv: (i, 0)),
            scratch_shapes=[pltpu.VMEM((d, 2 * D_EXPERT), BF16), pltpu.VMEM((D_EXPERT, d), BF16)],
        ),
        out_shape=jax.ShapeDtypeStruct((n_rows * ROW_SUB, LANES), F32),
        compiler_params=pltpu.CompilerParams(
            dimension_semantics=("arbitrary",), vmem_limit_bytes=VMEM_LIMIT),
        name="experts",
    )(block_e, nvb, x_pad, w_in, b_in, w_out, b_out)


def _combine_kernel(cnt_ref, loc_ref, glb_ref, ypad_ref, h_ref, slot_ref, gate_ref, o_ref,
                    sorted_ref, sem):
    i = pl.program_id(0)
    n = pl.num_programs(0)
    buf = i % 2

    def fetch(tile, b):
        _segment_copies(
            tile, cnt_ref, loc_ref, glb_ref,
            lambda loc, glb, size: pltpu.make_async_copy(
                _slab(ypad_ref, glb, size), _slab(sorted_ref.at[b], loc, size), sem.at[b]))

    pl.when(i == 0)(lambda: fetch(0, 0))
    pl.when(i + 1 < n)(lambda: fetch(i + 1, 1 - buf))
    pltpu.make_async_copy(_slab(ypad_ref, 0, N_SLOTS), sorted_ref.at[buf], sem.at[buf]).wait()

    acc = h_ref[...]
    for c in range(N_SLOTS // PCHUNK):
        pos = lax.broadcasted_iota(jnp.int32, (TM, PCHUNK), 1) + c * PCHUNK
        weight = jnp.zeros((TM, PCHUNK), F32)
        for k in range(TOP_K):
            weight = weight + jnp.where(pos == slot_ref[:, k:k + 1], gate_ref[:, k:k + 1], 0.0)
        y = jnp.concatenate(
            [sorted_ref[buf, pl.ds(c * PCHUNK * ROW_SUB + sub, PCHUNK, stride=ROW_SUB), :].astype(BF16)
             for sub in range(ROW_SUB)], axis=1)
        acc = acc + jnp.dot(weight.astype(BF16), y, preferred_element_type=F32)
    o_ref[...] = acc


def _combine(tcnt, tloc, tglb, y_pad, h, slots_tk, gates_tk):
    t, d = h.shape
    tile = lambda i, *_: (i, 0)
    return pl.pallas_call(
        _combine_kernel,
        grid_spec=pltpu.PrefetchScalarGridSpec(
            num_scalar_prefetch=3,
            grid=(t // TM,),
            in_specs=[
                pl.BlockSpec(memory_space=pl.ANY),
                pl.BlockSpec((TM, d), tile),
                pl.BlockSpec((TM, TOP_K), tile),
                pl.BlockSpec((TM, TOP_K), tile),
            ],
            out_specs=pl.BlockSpec((TM, d), tile),
            scratch_shapes=[pltpu.VMEM((2, N_SLOTS * ROW_SUB, LANES), F32),
                            pltpu.SemaphoreType.DMA((2,))],
        ),
        out_shape=jax.ShapeDtypeStruct((t, d), F32),
        compiler_params=pltpu.CompilerParams(
            dimension_semantics=("arbitrary",), vmem_limit_bytes=VMEM_LIMIT),
        name="combine",
    )(tcnt, tloc, tglb, y_pad, h, slots_tk, gates_tk)


def _band_bias(rel_bias):
    n_heads = rel_bias.shape[0]
    period = TQ + KWIN - 1
    first = REL_CLIP + LEFT - (KWIN - 1)
    vec = jnp.concatenate(
        [rel_bias[:, first:],
         jnp.broadcast_to(rel_bias[:, -1:], (n_heads, period - (2 * REL_CLIP + 1 - first)))], axis=1)
    shifted = jnp.tile(vec, (1, TQ + 1))[:, :TQ * (period + 1)].reshape(n_heads, TQ, period + 1)
    toeplitz = shifted[:, :, :KWIN][:, :, ::-1]
    r = jnp.arange(TQ)[:, None]
    j = jnp.arange(KWIN)[None, :]
    lo = (r // CHUNK) * CHUNK
    in_band = (j >= lo) & (j < lo + LEFT + CHUNK)
    return jnp.where(in_band[None], toeplitz.astype(F32), NEG_INF)


def kernel(x, norm_mix_g, w_in_proj, q_norm_g, k_norm_g, rel_bias, gmlp_v_norm_g, gmlp_w_s, gmlp_b_s,
           att_out_norm_g, gmlp_out_norm_g, w_out_proj, norm_moe_g, w_router, b_router,
           w_expert_in, b_expert_in, w_expert_out, b_expert_out):
    b, s, d = x.shape
    t = b * s

    head_of = jnp.arange(ATT_W) // HEAD_DIM
    block_diag = (head_of[:, None] == head_of[None, :]).astype(BF16)
    q, kt, v, gm = _inproj(
        x, norm_mix_g.reshape(1, d), w_in_proj.astype(BF16),
        jnp.tile(q_norm_g, N_HEADS).reshape(1, ATT_W), jnp.tile(k_norm_g, N_HEADS).reshape(1, ATT_W),
        block_diag, gmlp_v_norm_g, gmlp_w_s, gmlp_b_s.T, gmlp_out_norm_g.reshape(1, G_W))
    att = _attention(q, kt, v, _band_bias(rel_bias), att_out_norm_g.reshape(1, ATT_W))

    wr_t = w_router.T
    wr_hi = wr_t.astype(BF16)
    wr_lo = (wr_t - wr_hi.astype(F32)).astype(BF16)
    h, xn, slots, gates, tile_cnt = _outproj(
        att.reshape(t, ATT_W), gm.reshape(t, G_W), x.reshape(t, d), w_out_proj.astype(BF16),
        norm_moe_g.reshape(1, d), wr_hi, wr_lo, b_router.reshape(N_EXPERTS, 1))

    tcnt = tile_cnt[:, :, 0]
    counts = jnp.sum(tcnt, axis=0)
    padded = ((counts + TME - 1) // TME) * TME
    pad_ends = jnp.cumsum(padded).astype(jnp.int32)
    pad_starts = pad_ends - padded
    tloc = jnp.cumsum(tcnt, axis=1) - tcnt
    tglb = pad_starts[None, :] + jnp.cumsum(tcnt, axis=0) - tcnt
    n_blocks = (t * TOP_K) // TME + N_EXPERTS
    nvb = pad_ends[-1] // TME
    block_start = jnp.arange(n_blocks, dtype=jnp.int32) * TME
    block_e = jnp.minimum(jnp.sum(pad_ends[None, :] <= block_start[:, None], axis=1), N_EXPERTS - 1)
    block_e = jnp.where(jnp.arange(n_blocks) < nvb, block_e, block_e[nvb - 1]).astype(jnp.int32)
    runs = (tcnt.reshape(-1), tloc.reshape(-1).astype(jnp.int32), tglb.reshape(-1).astype(jnp.int32))

    x_pad = _dispatch(*runs, pad_ends, xn, slots, n_blocks * TME)
    y_pad = _experts(block_e, nvb.reshape(1), x_pad, w_expert_in,
                     b_expert_in.reshape(N_EXPERTS, 1, 2 * D_EXPERT), w_expert_out,
                     b_expert_out.reshape(N_EXPERTS, 1, d))
    out = _combine(*runs, y_pad, h, slots.T, gates.T)
    return out.reshape(b, s, d)
```

```python
import jax
import jax.numpy as jnp
from jax import lax
from jax.experimental import pallas as pl
from jax.experimental.pallas import tpu as pltpu

D_MODEL = 1024
CHUNK = 64
LEFT = 8 * CHUNK
N_HEADS = 8
HEAD_DIM = 64
ATT_W = 512
REL_CLIP = 128
G_GROUPS = 4
G_DIM = 128
G_BLOCK = 128
G_W = 512
N_EXPERTS = 32
TOP_K = 4
D_EXPERT = 1024
SWIGLU_LIMIT = 7.0
SWIGLU_ALPHA = 1.702
EPS = 1e-6
NEG_INF = -1e30

TM = 512
TQ = 256
KWIN = TQ + LEFT
TME = 256
N_SLOTS = TM * TOP_K
PCHUNK = 256
SEG_BITS = TM.bit_length()
VMEM_LIMIT = 56 * 1024 * 1024
LANES = 128
ROW_SUB = D_MODEL // LANES

F32 = jnp.float32
BF16 = jnp.bfloat16


def _rms(x, gain):
    ms = jnp.mean(x * x, axis=-1, keepdims=True)
    return x * lax.rsqrt(ms + EPS) * gain


def _gelu(x):
    return 0.5 * x * (1.0 + lax.erf(x * (2.0 ** -0.5)))


def _slab(ref, row, n_rows=1):
    return ref.at[pl.ds(pl.multiple_of(row * ROW_SUB, ROW_SUB), n_rows * ROW_SUB)]


def _segment_copies(tile, cnt_ref, loc_ref, glb_ref, make_copy):
    def per_expert(e, c):
        n = cnt_ref[tile * N_EXPERTS + e]
        loc = loc_ref[tile * N_EXPERTS + e]
        glb = glb_ref[tile * N_EXPERTS + e]
        for bit in range(SEG_BITS):
            size = 1 << bit
            off = (n >> (bit + 1)) << (bit + 1)

            @pl.when((n & size) != 0)
            def _():
                make_copy(loc + off, glb + off, size).start()
        return c

    lax.fori_loop(0, N_EXPERTS, per_expert, 0)


def _inproj_kernel(x_ref, g_ref, w_ref, gq_ref, gk_ref, bd_ref, gvn_ref, ws_ref, bst_ref, gout_ref,
                   q_ref, kt_ref, v_ref, gm_ref):
    j = pl.program_id(1)

    @pl.when(j == 0)
    def _():
        kt_ref[...] = jnp.zeros_like(kt_ref)
        v_ref[...] = jnp.zeros_like(v_ref)

    @pl.when(j > 0)
    def _():
        z = _rms(x_ref[0], g_ref[...]).astype(BF16)

        def proj(c0, width):
            return jnp.dot(z, w_ref[:, c0:c0 + width], preferred_element_type=F32)

        def head_norm(t, gain):
            ssum = jnp.dot((t * t).astype(BF16), bd_ref[...], preferred_element_type=F32)
            return t * lax.rsqrt(ssum * (1.0 / HEAD_DIM) + EPS) * gain

        q = head_norm(proj(0, ATT_W), gq_ref[...]) * (HEAD_DIM ** -0.5)
        q_ref[0] = q.astype(BF16)
        k = head_norm(proj(ATT_W, ATT_W), gk_ref[...])
        kt_ref[0] = k.T.astype(BF16)
        v_ref[0] = proj(2 * ATT_W, ATT_W).astype(BF16)

        gu = _gelu(proj(3 * ATT_W, G_W))
        gv = _gelu(proj(3 * ATT_W + G_W, G_W))
        row = lax.broadcasted_iota(jnp.int32, (G_BLOCK, G_BLOCK), 0) // CHUNK
        col = lax.broadcasted_iota(jnp.int32, (G_BLOCK, G_BLOCK), 1) // CHUNK
        tri = row >= col
        cols = []
        for g in range(G_GROUPS):
            sl = slice(g * G_DIM, (g + 1) * G_DIM)
            vn = _rms(gv[:, sl], gvn_ref[g:g + 1, :]).astype(BF16)
            wm = jnp.where(tri, ws_ref[g], 0.0).astype(BF16)
            blocks = []
            for n in range(TM // G_BLOCK):
                rs = slice(n * G_BLOCK, (n + 1) * G_BLOCK)
                gate = jnp.dot(wm, vn[rs], preferred_element_type=F32) + bst_ref[:, g:g + 1]
                blocks.append(gu[rs, sl] * gate)
            cols.append(jnp.concatenate(blocks, axis=0))
        gm = jnp.concatenate(cols, axis=1)
        gm_ref[0] = _rms(gm, gout_ref[...]).astype(BF16)


def _inproj(x, g_mix, w_in, gq, gk, bd, gvn, ws, bst, gout):
    b, s, d = x.shape
    nt = s // TM
    xmap = lambda bi, j: (bi, jnp.maximum(j - 1, 0), 0)
    const2 = lambda bi, j: (0, 0)
    return pl.pallas_call(
        _inproj_kernel,
        grid=(b, nt + 1),
        in_specs=[
            pl.BlockSpec((1, TM, d), xmap),
            pl.BlockSpec((1, d), const2),
            pl.BlockSpec(w_in.shape, const2),
            pl.BlockSpec((1, ATT_W), const2),
            pl.BlockSpec((1, ATT_W), const2),
            pl.BlockSpec((ATT_W, ATT_W), const2),
            pl.BlockSpec((G_GROUPS, G_DIM), const2),
            pl.BlockSpec((G_GROUPS, G_BLOCK, G_BLOCK), lambda bi, j: (0, 0, 0)),
            pl.BlockSpec((G_BLOCK, G_GROUPS), const2),
            pl.BlockSpec((1, G_W), const2),
        ],
        out_specs=[
            pl.BlockSpec((1, TM, ATT_W), xmap),
            pl.BlockSpec((1, ATT_W, TM), lambda bi, j: (bi, 0, j)),
            pl.BlockSpec((1, TM, ATT_W), lambda bi, j: (bi, j, 0)),
            pl.BlockSpec((1, TM, G_W), xmap),
        ],
        out_shape=[
            jax.ShapeDtypeStruct((b, s, ATT_W), BF16),
            jax.ShapeDtypeStruct((b, ATT_W, s + LEFT), BF16),
            jax.ShapeDtypeStruct((b, s + LEFT, ATT_W), BF16),
            jax.ShapeDtypeStruct((b, s, G_W), BF16),
        ],
        compiler_params=pltpu.CompilerParams(
            dimension_semantics=("arbitrary", "arbitrary"), vmem_limit_bytes=VMEM_LIMIT),
        name="inproj",
    )(x, g_mix, w_in, gq, gk, bd, gvn, ws, bst, gout)


def _attn_kernel(q_ref, kt_ref, v_ref, bias_ref, gout_ref, o_ref):
    i = pl.program_id(1)
    q0 = pl.multiple_of(i * TQ, TQ)
    win = pl.ds(q0, KWIN)
    colpos = lax.broadcasted_iota(jnp.int32, (1, KWIN), 1)
    pad_bias = jnp.where(colpos >= LEFT - q0, 0.0, NEG_INF).astype(F32)
    lane = lax.broadcasted_iota(jnp.int32, (1, 256), 1)
    outs = []
    for g in range(2):
        gs = slice(g * 256, (g + 1) * 256)
        qg = q_ref[0, :, gs]
        ktg = kt_ref[0, gs, win]
        vg = v_ref[0, win, gs]
        acc = jnp.zeros((TQ, 256), F32)
        for hh in range(4):
            h = 4 * g + hh
            hmask = (lane >= hh * HEAD_DIM) & (lane < (hh + 1) * HEAD_DIM)
            qh = jnp.where(hmask, qg, jnp.zeros_like(qg))
            s = jnp.dot(qh, ktg, preferred_element_type=F32)
            s = s + bias_ref[h] + pad_bias
            m = jnp.max(s, axis=-1, keepdims=True)
            e = jnp.exp(s - m)
            inv = 1.0 / jnp.sum(e, axis=-1, keepdims=True)
            pv = jnp.dot(e.astype(BF16), vg, preferred_element_type=F32)
            acc = jnp.where(hmask, pv * inv, acc)
        outs.append(acc)
    att = jnp.concatenate(outs, axis=1)
    o_ref[0] = _rms(att, gout_ref[...]).astype(BF16)


def _attention(q, kt, v, bias, gout):
    b, s, _ = q.shape
    return pl.pallas_call(
        _attn_kernel,
        grid=(b, s // TQ),
        in_specs=[
            pl.BlockSpec((1, TQ, ATT_W), lambda bi, i: (bi, i, 0)),
            pl.BlockSpec((1, ATT_W, s + LEFT), lambda bi, i: (bi, 0, 0)),
            pl.BlockSpec((1, s + LEFT, ATT_W), lambda bi, i: (bi, 0, 0)),
            pl.BlockSpec(bias.shape, lambda bi, i: (0, 0, 0)),
            pl.BlockSpec((1, ATT_W), lambda bi, i: (0, 0)),
        ],
        out_specs=pl.BlockSpec((1, TQ, ATT_W), lambda bi, i: (bi, i, 0)),
        out_shape=jax.ShapeDtypeStruct((b, s, ATT_W), BF16),
        compiler_params=pltpu.CompilerParams(
            dimension_semantics=("arbitrary", "arbitrary"), vmem_limit_bytes=VMEM_LIMIT),
        name="attention",
    )(q, kt, v, bias, gout)


def _outproj_kernel(a_ref, gm_ref, x_ref, wo_ref, gmoe_ref, wrh_ref, wrl_ref, br_ref,
                    h_ref, xn_ref, slot_ref, gate_ref, cnt_ref):
    h = (x_ref[...]
         + jnp.dot(a_ref[...], wo_ref[:ATT_W, :], preferred_element_type=F32)
         + jnp.dot(gm_ref[...], wo_ref[ATT_W:, :], preferred_element_type=F32))
    h_ref[...] = h
    xn = _rms(h, gmoe_ref[...])
    xhi = xn.astype(BF16)
    xlo = (xn - xhi.astype(F32)).astype(BF16)
    xn_ref[...] = xhi

    nt = (((1,), (1,)), ((), ()))
    logits = (lax.dot_general(wrh_ref[...], xhi, nt, preferred_element_type=F32)
              + lax.dot_general(wrh_ref[...], xlo, nt, preferred_element_type=F32)
              + lax.dot_general(wrl_ref[...], xhi, nt, preferred_element_type=F32)
              + br_ref[...])
    eidx = lax.broadcasted_iota(jnp.int32, (N_EXPERTS, TM), 0)
    vals, idxs = [], []
    cur = logits
    for _ in range(TOP_K):
        m = jnp.max(cur, axis=0, keepdims=True)
        ik = jnp.min(jnp.where(cur == m, eidx, N_EXPERTS), axis=0, keepdims=True)
        vals.append(m)
        idxs.append(ik)
        cur = jnp.where(eidx == ik, -jnp.inf, cur)
    exps = [jnp.exp(v - vals[0]) for v in vals]
    tot = exps[0] + exps[1] + exps[2] + exps[3]
    gate_ref[...] = jnp.concatenate([e / tot for e in exps], axis=0)

    onehots = [eidx == ik for ik in idxs]
    member = sum(oh.astype(F32) for oh in onehots).astype(BF16)
    earlier_tok = (lax.broadcasted_iota(jnp.int32, (TM, TM), 0)
                   < lax.broadcasted_iota(jnp.int32, (TM, TM), 1)).astype(BF16)
    earlier_exp = (lax.broadcasted_iota(jnp.int32, (N_EXPERTS, N_EXPERTS), 1)
                   < lax.broadcasted_iota(jnp.int32, (N_EXPERTS, N_EXPERTS), 0)).astype(BF16)
    rank = jnp.dot(member, earlier_tok, preferred_element_type=F32)
    lower = jnp.sum(jnp.dot(earlier_exp, member, preferred_element_type=F32), axis=1, keepdims=True)
    base = rank + lower
    slots = [jnp.sum(jnp.where(oh, base, 0.0), axis=0, keepdims=True) for oh in onehots]
    slot_ref[...] = jnp.concatenate(slots, axis=0).astype(jnp.int32)
    cnt_ref[0] = jnp.sum(member.astype(F32), axis=1, keepdims=True).astype(jnp.int32)


def _outproj(att, gm, x2, w_out, g_moe, wr_hi, wr_lo, br):
    t, d = x2.shape
    tile = lambda i: (i, 0)
    const = lambda i: (0, 0)
    lanes = lambda i: (0, i)
    return pl.pallas_call(
        _outproj_kernel,
        grid=(t // TM,),
        in_specs=[
            pl.BlockSpec((TM, ATT_W), tile),
            pl.BlockSpec((TM, G_W), tile),
            pl.BlockSpec((TM, d), tile),
            pl.BlockSpec(w_out.shape, const),
            pl.BlockSpec((1, d), const),
            pl.BlockSpec((N_EXPERTS, d), const),
            pl.BlockSpec((N_EXPERTS, d), const),
            pl.BlockSpec((N_EXPERTS, 1), const),
        ],
        out_specs=[
            pl.BlockSpec((TM, d), tile),
            pl.BlockSpec((TM, d), tile),
            pl.BlockSpec((TOP_K, TM), lanes),
            pl.BlockSpec((TOP_K, TM), lanes),
            pl.BlockSpec((1, N_EXPERTS, 1), lambda i: (i, 0, 0)),
        ],
        out_shape=[
            jax.ShapeDtypeStruct((t, d), F32),
            jax.ShapeDtypeStruct((t, d), BF16),
            jax.ShapeDtypeStruct((TOP_K, t), jnp.int32),
            jax.ShapeDtypeStruct((TOP_K, t), F32),
            jax.ShapeDtypeStruct((t // TM, N_EXPERTS, 1), jnp.int32),
        ],
        compiler_params=pltpu.CompilerParams(
            dimension_semantics=("arbitrary",), vmem_limit_bytes=VMEM_LIMIT),
        name="outproj_router",
    )(att, gm, x2, w_out, g_moe, wr_hi, wr_lo, br)


def _dispatch_kernel(cnt_ref, loc_ref, glb_ref, pend_ref, xn_ref, slot_ref, xpad_ref,
                     sorted_ref, zero_ref, sem, zsem):
    i = pl.program_id(0)
    n = pl.num_programs(0)
    buf = i % 2

    def drain(b):
        pltpu.make_async_copy(sorted_ref.at[b], _slab(xpad_ref, 0, N_SLOTS), sem.at[b]).wait()

    @pl.when(i == 0)
    def _():
        zero_ref[...] = jnp.zeros_like(zero_ref)

        def last_block(e):
            return pltpu.make_async_copy(zero_ref, _slab(xpad_ref, pend_ref[e] - TME, TME), zsem)

        def has_rows(e):
            return pend_ref[e] > jnp.where(e > 0, pend_ref[jnp.maximum(e - 1, 0)], 0)

        def start(e, c):
            pl.when(has_rows(e))(lambda: last_block(e).start())
            return c

        def wait(e, c):
            pl.when(has_rows(e))(lambda: last_block(e).wait())
            return c

        lax.fori_loop(0, N_EXPERTS, start, 0)
        lax.fori_loop(0, N_EXPERTS, wait, 0)

        def tail_block(blk):
            return pltpu.make_async_copy(zero_ref, _slab(xpad_ref, blk * TME, TME), zsem)

        first_tail = pend_ref[N_EXPERTS - 1] // TME
        n_blocks = xpad_ref.shape[0] // (TME * ROW_SUB)
        lax.fori_loop(first_tail, n_blocks, lambda blk, c: (tail_block(blk).start(), c)[1], 0)
        lax.fori_loop(first_tail, n_blocks, lambda blk, c: (tail_block(blk).wait(), c)[1], 0)

    pl.when(i >= 2)(lambda: drain(buf))

    xn = xn_ref[...]
    for c in range(N_SLOTS // PCHUNK):
        pos = lax.broadcasted_iota(jnp.int32, (PCHUNK, TM), 0) + c * PCHUNK
        hit = pos == slot_ref[0:1, :]
        for k in range(1, TOP_K):
            hit = hit | (pos == slot_ref[k:k + 1, :])
        rows = jnp.dot(hit.astype(F32).astype(BF16), xn, preferred_element_type=F32)
        for sub in range(ROW_SUB):
            sorted_ref[buf, pl.ds(c * PCHUNK * ROW_SUB + sub, PCHUNK, stride=ROW_SUB), :] = (
                rows[:, sub * LANES:(sub + 1) * LANES])

    _segment_copies(
        i, cnt_ref, loc_ref, glb_ref,
        lambda loc, glb, size: pltpu.make_async_copy(
            _slab(sorted_ref.at[buf], loc, size), _slab(xpad_ref, glb, size), sem.at[buf]))

    @pl.when(i == n - 1)
    def _():
        drain(buf)
        pl.when(n >= 2)(lambda: drain(1 - buf))


def _dispatch(tcnt, tloc, tglb, pad_ends, xn, slots, n_rows):
    t, d = xn.shape
    return pl.pallas_call(
        _dispatch_kernel,
        grid_spec=pltpu.PrefetchScalarGridSpec(
            num_scalar_prefetch=4,
            grid=(t // TM,),
            in_specs=[pl.BlockSpec((TM, d), lambda i, *_: (i, 0)),
                      pl.BlockSpec((TOP_K, TM), lambda i, *_: (0, i))],
            out_specs=pl.BlockSpec(memory_space=pl.ANY),
            scratch_shapes=[pltpu.VMEM((2, N_SLOTS * ROW_SUB, LANES), F32),
                            pltpu.VMEM((TME * ROW_SUB, LANES), F32),
                            pltpu.SemaphoreType.DMA((2,)), pltpu.SemaphoreType.DMA],
        ),
        out_shape=jax.ShapeDtypeStruct((n_rows * ROW_SUB, LANES), F32),
        compiler_params=pltpu.CompilerParams(
            dimension_semantics=("arbitrary",), vmem_limit_bytes=VMEM_LIMIT),
        name="dispatch",
    )(tcnt, tloc, tglb, pad_ends, xn, slots)


def _expert_kernel(be_ref, nvb_ref, run_ref, rexp_ref, nrun_ref, x_ref, wi_hbm, bi_ref, wo_hbm, bo_ref,
                   y_ref, wi32_ref, wo32_ref, wib_ref, wob_ref, wsem):
    i = pl.program_id(0)
    run = run_ref[i]
    buf = run % 2

    def fetch(r, b):
        e = rexp_ref[r]
        return (pltpu.make_async_copy(wi_hbm.at[e], wi32_ref.at[b], wsem.at[b]),
                pltpu.make_async_copy(wo_hbm.at[e], wo32_ref.at[b], wsem.at[b]))

    @pl.when(i == 0)
    def _():
        for copy in fetch(0, 0):
            copy.start()

    @pl.when((i == 0) | (run != run_ref[jnp.maximum(i - 1, 0)]))
    def _():
        for copy in fetch(run, buf):
            copy.wait()

        @pl.when(run + 1 < nrun_ref[0])
        def _():
            for copy in fetch(run + 1, 1 - buf):
                copy.start()

        wib_ref[...] = wi32_ref[buf].astype(BF16)
        wob_ref[...] = wo32_ref[buf].astype(BF16)

    @pl.when(i < nvb_ref[0])
    def _():
        x = jnp.concatenate(
            [x_ref[pl.ds(sub, TME, stride=ROW_SUB), :].astype(BF16) for sub in range(ROW_SUB)],
            axis=1)
        hdn = jnp.dot(x, wib_ref[...], preferred_element_type=F32) + bi_ref[0]
        gl = jnp.minimum(hdn[:, :D_EXPERT], SWIGLU_LIMIT)
        lin = jnp.clip(hdn[:, D_EXPERT:], -SWIGLU_LIMIT, SWIGLU_LIMIT)
        act = gl * jax.nn.sigmoid(SWIGLU_ALPHA * gl) * (lin + 1.0)
        y = jnp.dot(act.astype(BF16), wob_ref[...], preferred_element_type=F32) + bo_ref[0]
        for sub in range(ROW_SUB):
            y_ref[pl.ds(sub, TME, stride=ROW_SUB), :] = y[:, sub * LANES:(sub + 1) * LANES]

    @pl.when(i >= nvb_ref[0])
    def _():
        y_ref[...] = jnp.zeros_like(y_ref)


def _experts(block_e, nvb, block_run, run_expert, n_runs, x_pad, w_in, b_in, w_out, b_out):
    n_rows = x_pad.shape[0] // ROW_SUB
    d = D_MODEL
    nb = n_rows // TME
    rows = lambda i, be, nv, *_: (jnp.minimum(i, nv[0] - 1), 0)
    wsel = lambda i, be, *_: (be[i], 0, 0)
    return pl.pallas_call(
        _expert_kernel,
        grid_spec=pltpu.PrefetchScalarGridSpec(
            num_scalar_prefetch=5,
            grid=(nb,),
            in_specs=[
                pl.BlockSpec((TME * ROW_SUB, LANES), rows),
                pl.BlockSpec(memory_space=pl.ANY),
                pl.BlockSpec((1, 1, 2 * D_EXPERT), wsel),
                pl.BlockSpec(memory_space=pl.ANY),
                pl.BlockSpec((1, 1, d), wsel),
            ],
            out_specs=pl.BlockSpec((TME * ROW_SUB, LANES), lambda i, *_: (i, 0)),
            scratch_shapes=[pltpu.VMEM((2, d, 2 * D_EXPERT), F32), pltpu.VMEM((2, D_EXPERT, d), F32),
                            pltpu.VMEM((d, 2 * D_EXPERT), BF16), pltpu.VMEM((D_EXPERT, d), BF16),
                            pltpu.SemaphoreType.DMA((2,))],
        ),
        out_shape=jax.ShapeDtypeStruct((n_rows * ROW_SUB, LANES), F32),
        compiler_params=pltpu.CompilerParams(
            dimension_semantics=("arbitrary",), vmem_limit_bytes=VMEM_LIMIT),
        name="experts",
    )(block_e, nvb, block_run, run_expert, n_runs, x_pad, w_in, b_in, w_out, b_out)


def _combine_kernel(cnt_ref, loc_ref, glb_ref, ypad_ref, h_ref, slot_ref, gate_ref, o_ref,
                    sorted_ref, sem):
    i = pl.program_id(0)
    n = pl.num_programs(0)
    buf = i % 2

    def fetch(tile, b):
        _segment_copies(
            tile, cnt_ref, loc_ref, glb_ref,
            lambda loc, glb, size: pltpu.make_async_copy(
                _slab(ypad_ref, glb, size), _slab(sorted_ref.at[b], loc, size), sem.at[b]))

    pl.when(i == 0)(lambda: fetch(0, 0))
    pl.when(i + 1 < n)(lambda: fetch(i + 1, 1 - buf))
    pltpu.make_async_copy(_slab(ypad_ref, 0, N_SLOTS), sorted_ref.at[buf], sem.at[buf]).wait()

    acc = h_ref[...]
    for c in range(N_SLOTS // PCHUNK):
        pos = lax.broadcasted_iota(jnp.int32, (TM, PCHUNK), 1) + c * PCHUNK
        weight = jnp.zeros((TM, PCHUNK), F32)
        for k in range(TOP_K):
            weight = weight + jnp.where(pos == slot_ref[:, k:k + 1], gate_ref[:, k:k + 1], 0.0)
        y = jnp.concatenate(
            [sorted_ref[buf, pl.ds(c * PCHUNK * ROW_SUB + sub, PCHUNK, stride=ROW_SUB), :].astype(BF16)
             for sub in range(ROW_SUB)], axis=1)
        acc = acc + jnp.dot(weight.astype(BF16), y, preferred_element_type=F32)
    o_ref[...] = acc


def _combine(tcnt, tloc, tglb, y_pad, h, slots_tk, gates_tk):
    t, d = h.shape
    tile = lambda i, *_: (i, 0)
    return pl.pallas_call(
        _combine_kernel,
        grid_spec=pltpu.PrefetchScalarGridSpec(
            num_scalar_prefetch=3,
            grid=(t // TM,),
            in_specs=[
                pl.BlockSpec(memory_space=pl.ANY),
                pl.BlockSpec((TM, d), tile),
                pl.BlockSpec((TM, TOP_K), tile),
                pl.BlockSpec((TM, TOP_K), tile),
            ],
            out_specs=pl.BlockSpec((TM, d), tile),
            scratch_shapes=[pltpu.VMEM((2, N_SLOTS * ROW_SUB, LANES), F32),
                            pltpu.SemaphoreType.DMA((2,))],
        ),
        out_shape=jax.ShapeDtypeStruct((t, d), F32),
        compiler_params=pltpu.CompilerParams(
            dimension_semantics=("arbitrary",), vmem_limit_bytes=VMEM_LIMIT),
        name="combine",
    )(tcnt, tloc, tglb, y_pad, h, slots_tk, gates_tk)


def _band_bias(rel_bias):
    n_heads = rel_bias.shape[0]
    period = TQ + KWIN - 1
    n_low = max(0, TQ - 1 - REL_CLIP)
    first = max(0, REL_CLIP - (TQ - 1))
    n_high = period - n_low - (2 * REL_CLIP + 1 - first)
    vec = jnp.concatenate(
        [jnp.broadcast_to(rel_bias[:, :1], (n_heads, n_low)), rel_bias[:, first:],
         jnp.broadcast_to(rel_bias[:, -1:], (n_heads, n_high))], axis=1)
    shifted = jnp.tile(vec, (1, TQ + 1))[:, :TQ * (period + 1)].reshape(n_heads, TQ, period + 1)
    toeplitz = shifted[:, :, :KWIN][:, :, ::-1]
    r = jnp.arange(TQ)[:, None]
    j = jnp.arange(KWIN)[None, :]
    lo = (r // CHUNK) * CHUNK
    in_band = (j >= lo) & (j < lo + LEFT + CHUNK)
    return jnp.where(in_band[None], toeplitz.astype(F32), NEG_INF)


def kernel(x, norm_mix_g, w_in_proj, q_norm_g, k_norm_g, rel_bias, gmlp_v_norm_g, gmlp_w_s, gmlp_b_s,
           att_out_norm_g, gmlp_out_norm_g, w_out_proj, norm_moe_g, w_router, b_router,
           w_expert_in, b_expert_in, w_expert_out, b_expert_out):
    b, s, d = x.shape
    t = b * s

    head_of = jnp.arange(ATT_W) // HEAD_DIM
    block_diag = (head_of[:, None] == head_of[None, :]).astype(BF16)
    q, kt, v, gm = _inproj(
        x, norm_mix_g.reshape(1, d), w_in_proj.astype(BF16),
        jnp.tile(q_norm_g, N_HEADS).reshape(1, ATT_W), jnp.tile(k_norm_g, N_HEADS).reshape(1, ATT_W),
        block_diag, gmlp_v_norm_g, gmlp_w_s, gmlp_b_s.T, gmlp_out_norm_g.reshape(1, G_W))
    att = _attention(q, kt, v, _band_bias(rel_bias), att_out_norm_g.reshape(1, ATT_W))

    wr_t = w_router.T
    wr_hi = wr_t.astype(BF16)
    wr_lo = (wr_t - wr_hi.astype(F32)).astype(BF16)
    h, xn, slots, gates, tile_cnt = _outproj(
        att.reshape(t, ATT_W), gm.reshape(t, G_W), x.reshape(t, d), w_out_proj.astype(BF16),
        norm_moe_g.reshape(1, d), wr_hi, wr_lo, b_router.reshape(N_EXPERTS, 1))

    tcnt = tile_cnt[:, :, 0]
    counts = jnp.sum(tcnt, axis=0)
    padded = ((counts + TME - 1) // TME) * TME
    pad_ends = jnp.cumsum(padded).astype(jnp.int32)
    pad_starts = pad_ends - padded
    tloc = jnp.cumsum(tcnt, axis=1) - tcnt
    tglb = pad_starts[None, :] + jnp.cumsum(tcnt, axis=0) - tcnt
    n_blocks = (t * TOP_K) // TME + N_EXPERTS
    nvb = pad_ends[-1] // TME
    block_start = jnp.arange(n_blocks, dtype=jnp.int32) * TME
    block_e = jnp.minimum(jnp.sum(pad_ends[None, :] <= block_start[:, None], axis=1), N_EXPERTS - 1)
    block_e = jnp.where(jnp.arange(n_blocks) < nvb, block_e, block_e[nvb - 1]).astype(jnp.int32)
    runs = (tcnt.reshape(-1), tloc.reshape(-1).astype(jnp.int32), tglb.reshape(-1).astype(jnp.int32))
    present = padded > 0
    run_of_expert = jnp.cumsum(present) - 1
    block_run = run_of_expert[block_e].astype(jnp.int32)
    experts_iota = jnp.arange(N_EXPERTS)
    run_expert = jnp.sum(jnp.where(present[None, :] & (run_of_expert[None, :] == experts_iota[:, None]),
                                   experts_iota[None, :], 0), axis=1).astype(jnp.int32)
    n_runs = jnp.sum(present).astype(jnp.int32).reshape(1)

    x_pad = _dispatch(*runs, pad_ends, xn, slots, n_blocks * TME)
    y_pad = _experts(block_e, nvb.reshape(1), block_run, run_expert, n_runs, x_pad, w_expert_in,
                     b_expert_in.reshape(N_EXPERTS, 1, 2 * D_EXPERT), w_expert_out,
                     b_expert_out.reshape(N_EXPERTS, 1, d))
    out = _combine(*runs, y_pad, h, slots.T, gates.T)
    return out.reshape(b, s, d)
```

```python
import jax
import jax.numpy as jnp
from jax import lax
from jax.experimental import pallas as pl
from jax.experimental.pallas import tpu as pltpu

D_MODEL = 1024
CHUNK = 64
LEFT = 8 * CHUNK
N_HEADS = 8
HEAD_DIM = 64
ATT_W = 512
REL_CLIP = 128
G_GROUPS = 4
G_DIM = 128
G_BLOCK = 128
G_W = 512
N_EXPERTS = 32
TOP_K = 4
D_EXPERT = 1024
SWIGLU_LIMIT = 7.0
SWIGLU_ALPHA = 1.702
EPS = 1e-6
NEG_INF = -1e30
LOG2_E = 1.4426950408889634

TM = 512
TQ = 256
KWIN = TQ + LEFT
TME = 256
N_SLOTS = TM * TOP_K
PCHUNK = 256
SEG_BITS = TM.bit_length()
VMEM_LIMIT = 56 * 1024 * 1024
LANES = 128
ROW_SUB = D_MODEL // LANES

F32 = jnp.float32
BF16 = jnp.bfloat16


def _rms(x, gain):
    ms = jnp.mean(x * x, axis=-1, keepdims=True)
    return x * lax.rsqrt(ms + EPS) * gain


def _gelu(x):
    return 0.5 * x * (1.0 + lax.erf(x * (2.0 ** -0.5)))


def _slab(ref, row, n_rows=1):
    return ref.at[pl.ds(pl.multiple_of(row * ROW_SUB, ROW_SUB), n_rows * ROW_SUB)]


def _segment_copies(tile, cnt_ref, loc_ref, glb_ref, make_copy):
    def per_expert(e, c):
        n = cnt_ref[tile * N_EXPERTS + e]
        loc = loc_ref[tile * N_EXPERTS + e]
        glb = glb_ref[tile * N_EXPERTS + e]
        for bit in range(SEG_BITS):
            size = 1 << bit
            off = (n >> (bit + 1)) << (bit + 1)

            @pl.when((n & size) != 0)
            def _():
                make_copy(loc + off, glb + off, size).start(priority=bit % 2)
        return c

    lax.fori_loop(0, N_EXPERTS, per_expert, 0)


def _inproj_kernel(x_ref, g_ref, w_ref, gq_ref, gk_ref, bd_ref, gvn_ref, ws_ref, bst_ref, gout_ref,
                   q_ref, kt_ref, v_ref, gm_ref):
    j = pl.program_id(1)

    @pl.when(j == 0)
    def _():
        kt_ref[...] = jnp.zeros_like(kt_ref)
        v_ref[...] = jnp.zeros_like(v_ref)

    @pl.when(j > 0)
    def _():
        z = _rms(x_ref[0], g_ref[...]).astype(BF16)

        def proj(c0, width):
            return jnp.dot(z, w_ref[:, c0:c0 + width], preferred_element_type=F32)

        def head_norm(t, gain):
            ssum = jnp.dot((t * t).astype(BF16), bd_ref[...], preferred_element_type=F32)
            return t * lax.rsqrt(ssum * (1.0 / HEAD_DIM) + EPS) * gain

        q = head_norm(proj(0, ATT_W), gq_ref[...]) * (HEAD_DIM ** -0.5 * LOG2_E)
        q_ref[0] = q.astype(BF16)
        k = head_norm(proj(ATT_W, ATT_W), gk_ref[...])
        kt_ref[0] = k.T.astype(BF16)
        v_ref[0] = proj(2 * ATT_W, ATT_W).astype(BF16)

        gu = _gelu(proj(3 * ATT_W, G_W))
        gv = _gelu(proj(3 * ATT_W + G_W, G_W))
        row = lax.broadcasted_iota(jnp.int32, (G_BLOCK, G_BLOCK), 0) // CHUNK
        col = lax.broadcasted_iota(jnp.int32, (G_BLOCK, G_BLOCK), 1) // CHUNK
        tri = row >= col
        cols = []
        for g in range(G_GROUPS):
            sl = slice(g * G_DIM, (g + 1) * G_DIM)
            vn = _rms(gv[:, sl], gvn_ref[g:g + 1, :]).astype(BF16)
            wm = jnp.where(tri, ws_ref[g], 0.0).astype(BF16)
            blocks = []
            for n in range(TM // G_BLOCK):
                rs = slice(n * G_BLOCK, (n + 1) * G_BLOCK)
                gate = jnp.dot(wm, vn[rs], preferred_element_type=F32) + bst_ref[:, g:g + 1]
                blocks.append(gu[rs, sl] * gate)
            cols.append(jnp.concatenate(blocks, axis=0))
        gm = jnp.concatenate(cols, axis=1)
        gm_ref[0] = _rms(gm, gout_ref[...]).astype(BF16)


def _inproj(x, g_mix, w_in, gq, gk, bd, gvn, ws, bst, gout):
    b, s, d = x.shape
    nt = s // TM
    xmap = lambda bi, j: (bi, jnp.maximum(j - 1, 0), 0)
    const2 = lambda bi, j: (0, 0)
    return pl.pallas_call(
        _inproj_kernel,
        grid=(b, nt + 1),
        in_specs=[
            pl.BlockSpec((1, TM, d), xmap),
            pl.BlockSpec((1, d), const2),
            pl.BlockSpec(w_in.shape, const2),
            pl.BlockSpec((1, ATT_W), const2),
            pl.BlockSpec((1, ATT_W), const2),
            pl.BlockSpec((ATT_W, ATT_W), const2),
            pl.BlockSpec((G_GROUPS, G_DIM), const2),
            pl.BlockSpec((G_GROUPS, G_BLOCK, G_BLOCK), lambda bi, j: (0, 0, 0)),
            pl.BlockSpec((G_BLOCK, G_GROUPS), const2),
            pl.BlockSpec((1, G_W), const2),
        ],
        out_specs=[
            pl.BlockSpec((1, TM, ATT_W), xmap),
            pl.BlockSpec((1, ATT_W, TM), lambda bi, j: (bi, 0, j)),
            pl.BlockSpec((1, TM, ATT_W), lambda bi, j: (bi, j, 0)),
            pl.BlockSpec((1, TM, G_W), xmap),
        ],
        out_shape=[
            jax.ShapeDtypeStruct((b, s, ATT_W), BF16),
            jax.ShapeDtypeStruct((b, ATT_W, s + LEFT), BF16),
            jax.ShapeDtypeStruct((b, s + LEFT, ATT_W), BF16),
            jax.ShapeDtypeStruct((b, s, G_W), BF16),
        ],
        compiler_params=pltpu.CompilerParams(
            dimension_semantics=("arbitrary", "arbitrary"), vmem_limit_bytes=VMEM_LIMIT),
        name="inproj",
    )(x, g_mix, w_in, gq, gk, bd, gvn, ws, bst, gout)


def _attn_kernel(q_ref, kt_ref, v_ref, bias_ref, gout_ref, o_ref):
    i = pl.program_id(1)
    q0 = pl.multiple_of(i * TQ, TQ)
    win = pl.ds(q0, KWIN)
    lane = lax.broadcasted_iota(jnp.int32, (1, 256), 1)

    def tile(has_left_padding):
        if has_left_padding:
            colpos = lax.broadcasted_iota(jnp.int32, (1, KWIN), 1)
            pad_bias = jnp.where(colpos >= LEFT - q0, 0.0, NEG_INF).astype(F32)
        outs = []
        for g in range(2):
            gs = slice(g * 256, (g + 1) * 256)
            qg = q_ref[0, :, gs]
            ktg = kt_ref[0, gs, win]
            vg = v_ref[0, win, gs]
            acc = jnp.zeros((TQ, 256), F32)
            for hh in range(4):
                h = 4 * g + hh
                hmask = (lane >= hh * HEAD_DIM) & (lane < (hh + 1) * HEAD_DIM)
                qh = jnp.where(hmask, qg, jnp.zeros_like(qg))
                s = jnp.dot(qh, ktg, preferred_element_type=F32) + bias_ref[h]
                if has_left_padding:
                    s = s + pad_bias
                m = jnp.max(s, axis=-1, keepdims=True)
                e = jnp.exp2(s - m)
                inv = 1.0 / jnp.sum(e, axis=-1, keepdims=True)
                pv = jnp.dot(e.astype(BF16), vg, preferred_element_type=F32)
                acc = jnp.where(hmask, pv * inv, acc)
            outs.append(acc)
        att = jnp.concatenate(outs, axis=1)
        o_ref[0] = _rms(att, gout_ref[...]).astype(BF16)

    pl.when(q0 < LEFT)(lambda: tile(True))
    pl.when(q0 >= LEFT)(lambda: tile(False))


def _attention(q, kt, v, bias, gout):
    b, s, _ = q.shape
    return pl.pallas_call(
        _attn_kernel,
        grid=(b, s // TQ),
        in_specs=[
            pl.BlockSpec((1, TQ, ATT_W), lambda bi, i: (bi, i, 0)),
            pl.BlockSpec((1, ATT_W, s + LEFT), lambda bi, i: (bi, 0, 0)),
            pl.BlockSpec((1, s + LEFT, ATT_W), lambda bi, i: (bi, 0, 0)),
            pl.BlockSpec(bias.shape, lambda bi, i: (0, 0, 0)),
            pl.BlockSpec((1, ATT_W), lambda bi, i: (0, 0)),
        ],
        out_specs=pl.BlockSpec((1, TQ, ATT_W), lambda bi, i: (bi, i, 0)),
        out_shape=jax.ShapeDtypeStruct((b, s, ATT_W), BF16),
        compiler_params=pltpu.CompilerParams(
            dimension_semantics=("arbitrary", "arbitrary"), vmem_limit_bytes=VMEM_LIMIT),
        name="attention",
    )(q, kt, v, bias, gout)


def _outproj_kernel(a_ref, gm_ref, x_ref, wo_ref, gmoe_ref, wrh_ref, wrl_ref, br_ref,
                    h_ref, xn_ref, slot_ref, gate_ref, cnt_ref):
    h = (x_ref[...]
         + jnp.dot(a_ref[...], wo_ref[:ATT_W, :], preferred_element_type=F32)
         + jnp.dot(gm_ref[...], wo_ref[ATT_W:, :], preferred_element_type=F32))
    h_ref[...] = h
    xn = _rms(h, gmoe_ref[...])
    xhi = xn.astype(BF16)
    xlo = (xn - xhi.astype(F32)).astype(BF16)
    xn_ref[...] = xhi

    nt = (((1,), (1,)), ((), ()))
    logits = (lax.dot_general(wrh_ref[...], xhi, nt, preferred_element_type=F32)
              + lax.dot_general(wrh_ref[...], xlo, nt, preferred_element_type=F32)
              + lax.dot_general(wrl_ref[...], xhi, nt, preferred_element_type=F32)
              + br_ref[...])
    eidx = lax.broadcasted_iota(jnp.int32, (N_EXPERTS, TM), 0)
    vals, idxs = [], []
    cur = logits
    for _ in range(TOP_K):
        m = jnp.max(cur, axis=0, keepdims=True)
        ik = jnp.min(jnp.where(cur == m, eidx, N_EXPERTS), axis=0, keepdims=True)
        vals.append(m)
        idxs.append(ik)
        cur = jnp.where(eidx == ik, -jnp.inf, cur)
    exps = [jnp.exp(v - vals[0]) for v in vals]
    tot = exps[0] + exps[1] + exps[2] + exps[3]
    gate_ref[...] = jnp.concatenate([e / tot for e in exps], axis=0)

    onehots = [eidx == ik for ik in idxs]
    member = sum(oh.astype(F32) for oh in onehots).astype(BF16)
    earlier_tok = (lax.broadcasted_iota(jnp.int32, (TM, TM), 0)
                   < lax.broadcasted_iota(jnp.int32, (TM, TM), 1)).astype(BF16)
    earlier_exp = (lax.broadcasted_iota(jnp.int32, (N_EXPERTS, N_EXPERTS), 1)
                   < lax.broadcasted_iota(jnp.int32, (N_EXPERTS, N_EXPERTS), 0)).astype(BF16)
    rank = jnp.dot(member, earlier_tok, preferred_element_type=F32)
    lower = jnp.sum(jnp.dot(earlier_exp, member, preferred_element_type=F32), axis=1, keepdims=True)
    base = rank + lower
    slots = [jnp.sum(jnp.where(oh, base, 0.0), axis=0, keepdims=True) for oh in onehots]
    slot_ref[...] = jnp.concatenate(slots, axis=0).astype(jnp.int32)
    cnt_ref[0] = jnp.sum(member.astype(F32), axis=1, keepdims=True).astype(jnp.int32)


def _outproj(att, gm, x2, w_out, g_moe, wr_hi, wr_lo, br):
    t, d = x2.shape
    tile = lambda i: (i, 0)
    const = lambda i: (0, 0)
    lanes = lambda i: (0, i)
    return pl.pallas_call(
        _outproj_kernel,
        grid=(t // TM,),
        in_specs=[
            pl.BlockSpec((TM, ATT_W), tile),
            pl.BlockSpec((TM, G_W), tile),
            pl.BlockSpec((TM, d), tile),
            pl.BlockSpec(w_out.shape, const),
            pl.BlockSpec((1, d), const),
            pl.BlockSpec((N_EXPERTS, d), const),
            pl.BlockSpec((N_EXPERTS, d), const),
            pl.BlockSpec((N_EXPERTS, 1), const),
        ],
        out_specs=[
            pl.BlockSpec((TM, d), tile),
            pl.BlockSpec((TM, d), tile),
            pl.BlockSpec((TOP_K, TM), lanes),
            pl.BlockSpec((TOP_K, TM), lanes),
            pl.BlockSpec((1, N_EXPERTS, 1), lambda i: (i, 0, 0)),
        ],
        out_shape=[
            jax.ShapeDtypeStruct((t, d), F32),
            jax.ShapeDtypeStruct((t, d), BF16),
            jax.ShapeDtypeStruct((TOP_K, t), jnp.int32),
            jax.ShapeDtypeStruct((TOP_K, t), F32),
            jax.ShapeDtypeStruct((t // TM, N_EXPERTS, 1), jnp.int32),
        ],
        compiler_params=pltpu.CompilerParams(
            dimension_semantics=("arbitrary",), vmem_limit_bytes=VMEM_LIMIT),
        name="outproj_router",
    )(att, gm, x2, w_out, g_moe, wr_hi, wr_lo, br)


def _dispatch_kernel(cnt_ref, loc_ref, glb_ref, pend_ref, xn_ref, slot_ref, xpad_ref,
                     sorted_ref, zero_ref, sem, zsem):
    i = pl.program_id(0)
    n = pl.num_programs(0)
    buf = i % 2

    def drain(b):
        pltpu.make_async_copy(sorted_ref.at[b], _slab(xpad_ref, 0, N_SLOTS), sem.at[b]).wait()

    @pl.when(i == 0)
    def _():
        zero_ref[...] = jnp.zeros_like(zero_ref)

        def last_block(e):
            return pltpu.make_async_copy(zero_ref, _slab(xpad_ref, pend_ref[e] - TME, TME), zsem)

        def has_rows(e):
            return pend_ref[e] > jnp.where(e > 0, pend_ref[jnp.maximum(e - 1, 0)], 0)

        def start(e, c):
            pl.when(has_rows(e))(lambda: last_block(e).start())
            return c

        def wait(e, c):
            pl.when(has_rows(e))(lambda: last_block(e).wait())
            return c

        lax.fori_loop(0, N_EXPERTS, start, 0)
        lax.fori_loop(0, N_EXPERTS, wait, 0)

        def tail_block(blk):
            return pltpu.make_async_copy(zero_ref, _slab(xpad_ref, blk * TME, TME), zsem)

        first_tail = pend_ref[N_EXPERTS - 1] // TME
        n_blocks = xpad_ref.shape[0] // (TME * ROW_SUB)
        lax.fori_loop(first_tail, n_blocks, lambda blk, c: (tail_block(blk).start(), c)[1], 0)
        lax.fori_loop(first_tail, n_blocks, lambda blk, c: (tail_block(blk).wait(), c)[1], 0)

    pl.when(i >= 2)(lambda: drain(buf))

    xn = xn_ref[...]
    for c in range(N_SLOTS // PCHUNK):
        pos = lax.broadcasted_iota(jnp.int32, (PCHUNK, TM), 0) + c * PCHUNK
        hit = pos == slot_ref[0:1, :]
        for k in range(1, TOP_K):
            hit = hit | (pos == slot_ref[k:k + 1, :])
        rows = jnp.dot(hit.astype(F32).astype(BF16), xn, preferred_element_type=F32)
        for sub in range(ROW_SUB):
            sorted_ref[buf, pl.ds(c * PCHUNK * ROW_SUB + sub, PCHUNK, stride=ROW_SUB), :] = (
                rows[:, sub * LANES:(sub + 1) * LANES])

    _segment_copies(
        i, cnt_ref, loc_ref, glb_ref,
        lambda loc, glb, size: pltpu.make_async_copy(
            _slab(sorted_ref.at[buf], loc, size), _slab(xpad_ref, glb, size), sem.at[buf]))

    @pl.when(i == n - 1)
    def _():
        drain(buf)
        pl.when(n >= 2)(lambda: drain(1 - buf))


def _dispatch(tcnt, tloc, tglb, pad_ends, xn, slots, n_rows):
    t, d = xn.shape
    return pl.pallas_call(
        _dispatch_kernel,
        grid_spec=pltpu.PrefetchScalarGridSpec(
            num_scalar_prefetch=4,
            grid=(t // TM,),
            in_specs=[pl.BlockSpec((TM, d), lambda i, *_: (i, 0)),
                      pl.BlockSpec((TOP_K, TM), lambda i, *_: (0, i))],
            out_specs=pl.BlockSpec(memory_space=pl.ANY),
            scratch_shapes=[pltpu.VMEM((2, N_SLOTS * ROW_SUB, LANES), F32),
                            pltpu.VMEM((TME * ROW_SUB, LANES), F32),
                            pltpu.SemaphoreType.DMA((2,)), pltpu.SemaphoreType.DMA],
        ),
        out_shape=jax.ShapeDtypeStruct((n_rows * ROW_SUB, LANES), F32),
        compiler_params=pltpu.CompilerParams(
            dimension_semantics=("arbitrary",), vmem_limit_bytes=VMEM_LIMIT),
        name="dispatch",
    )(tcnt, tloc, tglb, pad_ends, xn, slots)


def _expert_kernel(be_ref, nvb_ref, run_ref, rexp_ref, nrun_ref, x_ref, wi_hbm, bi_ref, wo_hbm, bo_ref,
                   y_ref, wi32_ref, wo32_ref, wib_ref, wob_ref, wsem):
    i = pl.program_id(0)
    run = run_ref[i]
    buf = run % 2

    def fetch(r, b):
        e = rexp_ref[r]
        return (pltpu.make_async_copy(wi_hbm.at[e], wi32_ref.at[b], wsem.at[b]),
                pltpu.make_async_copy(wo_hbm.at[e], wo32_ref.at[b], wsem.at[b]))

    @pl.when(i == 0)
    def _():
        for copy in fetch(0, 0):
            copy.start()

    @pl.when((i == 0) | (run != run_ref[jnp.maximum(i - 1, 0)]))
    def _():
        for copy in fetch(run, buf):
            copy.wait()

        @pl.when(run + 1 < nrun_ref[0])
        def _():
            for copy in fetch(run + 1, 1 - buf):
                copy.start()

        wib_ref[...] = wi32_ref[buf].astype(BF16)
        wob_ref[...] = wo32_ref[buf].astype(BF16)

    @pl.when(i < nvb_ref[0])
    def _():
        x = jnp.concatenate(
            [x_ref[pl.ds(sub, TME, stride=ROW_SUB), :].astype(BF16) for sub in range(ROW_SUB)],
            axis=1)
        hdn = jnp.dot(x, wib_ref[...], preferred_element_type=F32) + bi_ref[0]
        gl = jnp.minimum(hdn[:, :D_EXPERT], SWIGLU_LIMIT)
        lin = jnp.clip(hdn[:, D_EXPERT:], -SWIGLU_LIMIT, SWIGLU_LIMIT)
        act = gl * jax.nn.sigmoid(SWIGLU_ALPHA * gl) * (lin + 1.0)
        y = jnp.dot(act.astype(BF16), wob_ref[...], preferred_element_type=F32) + bo_ref[0]
        for sub in range(ROW_SUB):
            y_ref[pl.ds(sub, TME, stride=ROW_SUB), :] = y[:, sub * LANES:(sub + 1) * LANES]

    @pl.when(i >= nvb_ref[0])
    def _():
        y_ref[...] = jnp.zeros_like(y_ref)


def _experts(block_e, nvb, block_run, run_expert, n_runs, x_pad, w_in, b_in, w_out, b_out):
    n_rows = x_pad.shape[0] // ROW_SUB
    d = D_MODEL
    nb = n_rows // TME
    rows = lambda i, be, nv, *_: (jnp.minimum(i, nv[0] - 1), 0)
    wsel = lambda i, be, *_: (be[i], 0, 0)
    return pl.pallas_call(
        _expert_kernel,
        grid_spec=pltpu.PrefetchScalarGridSpec(
            num_scalar_prefetch=5,
            grid=(nb,),
            in_specs=[
                pl.BlockSpec((TME * ROW_SUB, LANES), rows),
                pl.BlockSpec(memory_space=pl.ANY),
                pl.BlockSpec((1, 1, 2 * D_EXPERT), wsel),
                pl.BlockSpec(memory_space=pl.ANY),
                pl.BlockSpec((1, 1, d), wsel),
            ],
            out_specs=pl.BlockSpec((TME * ROW_SUB, LANES), lambda i, *_: (i, 0)),
            scratch_shapes=[pltpu.VMEM((2, d, 2 * D_EXPERT), F32), pltpu.VMEM((2, D_EXPERT, d), F32),
                            pltpu.VMEM((d, 2 * D_EXPERT), BF16), pltpu.VMEM((D_EXPERT, d), BF16),
                            pltpu.SemaphoreType.DMA((2,))],
        ),
        out_shape=jax.ShapeDtypeStruct((n_rows * ROW_SUB, LANES), F32),
        compiler_params=pltpu.CompilerParams(
            dimension_semantics=("arbitrary",), vmem_limit_bytes=VMEM_LIMIT),
        name="experts",
    )(block_e, nvb, block_run, run_expert, n_runs, x_pad, w_in, b_in, w_out, b_out)


def _combine_kernel(cnt_ref, loc_ref, glb_ref, ypad_ref, h_ref, slot_ref, gate_ref, o_ref,
                    sorted_ref, sem):
    i = pl.program_id(0)
    n = pl.num_programs(0)
    buf = i % 2

    def fetch(tile, b):
        _segment_copies(
            tile, cnt_ref, loc_ref, glb_ref,
            lambda loc, glb, size: pltpu.make_async_copy(
                _slab(ypad_ref, glb, size), _slab(sorted_ref.at[b], loc, size), sem.at[b]))

    pl.when(i == 0)(lambda: fetch(0, 0))
    pl.when(i + 1 < n)(lambda: fetch(i + 1, 1 - buf))
    pltpu.make_async_copy(_slab(ypad_ref, 0, N_SLOTS), sorted_ref.at[buf], sem.at[buf]).wait()

    acc = h_ref[...]
    for c in range(N_SLOTS // PCHUNK):
        pos = lax.broadcasted_iota(jnp.int32, (TM, PCHUNK), 1) + c * PCHUNK
        weight = jnp.zeros((TM, PCHUNK), F32)
        for k in range(TOP_K):
            weight = weight + jnp.where(pos == slot_ref[:, k:k + 1], gate_ref[:, k:k + 1], 0.0)
        y = jnp.concatenate(
            [sorted_ref[buf, pl.ds(c * PCHUNK * ROW_SUB + sub, PCHUNK, stride=ROW_SUB), :].astype(BF16)
             for sub in range(ROW_SUB)], axis=1)
        acc = acc + jnp.dot(weight.astype(BF16), y, preferred_element_type=F32)
    o_ref[...] = acc


def _combine(tcnt, tloc, tglb, y_pad, h, slots_tk, gates_tk):
    t, d = h.shape
    tile = lambda i, *_: (i, 0)
    return pl.pallas_call(
        _combine_kernel,
        grid_spec=pltpu.PrefetchScalarGridSpec(
            num_scalar_prefetch=3,
            grid=(t // TM,),
            in_specs=[
                pl.BlockSpec(memory_space=pl.ANY),
                pl.BlockSpec((TM, d), tile),
                pl.BlockSpec((TM, TOP_K), tile),
                pl.BlockSpec((TM, TOP_K), tile),
            ],
            out_specs=pl.BlockSpec((TM, d), tile),
            scratch_shapes=[pltpu.VMEM((2, N_SLOTS * ROW_SUB, LANES), F32),
                            pltpu.SemaphoreType.DMA((2,))],
        ),
        out_shape=jax.ShapeDtypeStruct((t, d), F32),
        compiler_params=pltpu.CompilerParams(
            dimension_semantics=("arbitrary",), vmem_limit_bytes=VMEM_LIMIT),
        name="combine",
    )(tcnt, tloc, tglb, y_pad, h, slots_tk, gates_tk)


def _band_bias(rel_bias):
    n_heads = rel_bias.shape[0]
    period = TQ + KWIN - 1
    far = LEFT - REL_CLIP
    falling = rel_bias[:, ::-1][:, 1:]
    n_fall = min(KWIN - 1 - far, 2 * REL_CLIP)
    vec = jnp.concatenate(
        [jnp.broadcast_to(rel_bias[:, -1:], (n_heads, far + 1)), falling[:, :n_fall],
         jnp.broadcast_to(rel_bias[:, :1], (n_heads, KWIN - 1 - far - n_fall)),
         jnp.broadcast_to(rel_bias[:, -1:], (n_heads, TQ - 1))], axis=1)
    toeplitz = jnp.tile(vec, (1, TQ))[:, :TQ * (period - 1)].reshape(n_heads, TQ, period - 1)[:, :, :KWIN]
    r = jnp.arange(TQ)[:, None]
    j = jnp.arange(KWIN)[None, :]
    lo = (r // CHUNK) * CHUNK
    in_band = (j >= lo) & (j < lo + LEFT + CHUNK)
    return jnp.where(in_band[None], toeplitz.astype(F32) * LOG2_E, NEG_INF)


def kernel(x, norm_mix_g, w_in_proj, q_norm_g, k_norm_g, rel_bias, gmlp_v_norm_g, gmlp_w_s, gmlp_b_s,
           att_out_norm_g, gmlp_out_norm_g, w_out_proj, norm_moe_g, w_router, b_router,
           w_expert_in, b_expert_in, w_expert_out, b_expert_out):
    b, s, d = x.shape
    t = b * s

    head_of = jnp.arange(ATT_W) // HEAD_DIM
    block_diag = (head_of[:, None] == head_of[None, :]).astype(BF16)
    q, kt, v, gm = _inproj(
        x, norm_mix_g.reshape(1, d), w_in_proj.astype(BF16),
        jnp.tile(q_norm_g, N_HEADS).reshape(1, ATT_W), jnp.tile(k_norm_g, N_HEADS).reshape(1, ATT_W),
        block_diag, gmlp_v_norm_g, gmlp_w_s, gmlp_b_s.T, gmlp_out_norm_g.reshape(1, G_W))
    att = _attention(q, kt, v, _band_bias(rel_bias), att_out_norm_g.reshape(1, ATT_W))

    wr_t = w_router.T
    wr_hi = wr_t.astype(BF16)
    wr_lo = (wr_t - wr_hi.astype(F32)).astype(BF16)
    h, xn, slots, gates, tile_cnt = _outproj(
        att.reshape(t, ATT_W), gm.reshape(t, G_W), x.reshape(t, d), w_out_proj.astype(BF16),
        norm_moe_g.reshape(1, d), wr_hi, wr_lo, b_router.reshape(N_EXPERTS, 1))

    tcnt = tile_cnt[:, :, 0]
    counts = jnp.sum(tcnt, axis=0)
    padded = ((counts + TME - 1) // TME) * TME
    pad_ends = jnp.cumsum(padded).astype(jnp.int32)
    pad_starts = pad_ends - padded
    tloc = jnp.cumsum(tcnt, axis=1) - tcnt
    tglb = pad_starts[None, :] + jnp.cumsum(tcnt, axis=0) - tcnt
    n_blocks = (t * TOP_K) // TME + N_EXPERTS
    nvb = pad_ends[-1] // TME
    block_start = jnp.arange(n_blocks, dtype=jnp.int32) * TME
    block_e = jnp.minimum(jnp.sum(pad_ends[None, :] <= block_start[:, None], axis=1), N_EXPERTS - 1)
    block_e = jnp.where(jnp.arange(n_blocks) < nvb, block_e, block_e[nvb - 1]).astype(jnp.int32)
    runs = (tcnt.reshape(-1), tloc.reshape(-1).astype(jnp.int32), tglb.reshape(-1).astype(jnp.int32))
    present = padded > 0
    run_of_expert = jnp.cumsum(present) - 1
    block_run = run_of_expert[block_e].astype(jnp.int32)
    experts_iota = jnp.arange(N_EXPERTS)
    run_expert = jnp.sum(jnp.where(present[None, :] & (run_of_expert[None, :] == experts_iota[:, None]),
                                   experts_iota[None, :], 0), axis=1).astype(jnp.int32)
    n_runs = jnp.sum(present).astype(jnp.int32).reshape(1)

    x_pad = _dispatch(*runs, pad_ends, xn, slots, n_blocks * TME)
    y_pad = _experts(block_e, nvb.reshape(1), block_run, run_expert, n_runs, x_pad, w_expert_in,
                     b_expert_in.reshape(N_EXPERTS, 1, 2 * D_EXPERT), w_expert_out,
                     b_expert_out.reshape(N_EXPERTS, 1, d))
    out = _combine(*runs, y_pad, h, slots.T, gates.T)
    return out.reshape(b, s, d)
```

```python
import jax
import jax.numpy as jnp
from jax import lax
from jax.experimental import pallas as pl
from jax.experimental.pallas import tpu as pltpu

D_MODEL = 1024
CHUNK = 64
LEFT = 8 * CHUNK
N_HEADS = 8
HEAD_DIM = 64
ATT_W = 512
REL_CLIP = 128
G_GROUPS = 4
G_DIM = 128
G_BLOCK = 128
G_W = 512
N_EXPERTS = 32
TOP_K = 4
D_EXPERT = 1024
SWIGLU_LIMIT = 7.0
SWIGLU_ALPHA = 1.702
EPS = 1e-6
NEG_INF = -1e30
LOG2_E = 1.4426950408889634

TM = 512
TQ = 256
KWIN = TQ + LEFT
BIAS_LANES = 1024
TME = 256
N_SLOTS = TM * TOP_K
PCHUNK = 256
SEG_BITS = TM.bit_length()
VMEM_LIMIT = 56 * 1024 * 1024
LANES = 128
ROW_SUB = D_MODEL // LANES

F32 = jnp.float32
BF16 = jnp.bfloat16


def _rms(x, gain):
    ms = jnp.mean(x * x, axis=-1, keepdims=True)
    return x * lax.rsqrt(ms + EPS) * gain


def _gelu(x):
    return 0.5 * x * (1.0 + lax.erf(x * (2.0 ** -0.5)))


def _slab(ref, row, n_rows=1):
    return ref.at[pl.ds(pl.multiple_of(row * ROW_SUB, ROW_SUB), n_rows * ROW_SUB)]


def _slab_at(ref, offset, n_rows):
    return ref.at[pl.ds(pl.multiple_of(offset, ROW_SUB), n_rows * ROW_SUB)]


def _segment_copies(tile, cnt_ref, loc_ref, glb_ref, make_copy):
    def per_expert(e, c):
        n = cnt_ref[tile * N_EXPERTS + e]
        loc = loc_ref[tile * N_EXPERTS + e]
        glb = glb_ref[tile * N_EXPERTS + e]
        for bit in range(SEG_BITS):
            size = 1 << bit
            off = n & ~(2 * size * ROW_SUB - 1)

            @pl.when((n & (size * ROW_SUB)) != 0)
            def _():
                make_copy(loc + off, glb + off, size).start()
        return c

    lax.fori_loop(0, N_EXPERTS, per_expert, 0, unroll=2)


def _inproj_kernel(x_ref, g_ref, w_ref, gq_ref, gk_ref, bd_ref, gvn_ref, ws_ref, bst_ref, gout_ref,
                   q_ref, kt_ref, v_ref, gm_ref):
    j = pl.program_id(1)

    @pl.when(j == 0)
    def _():
        kt_ref[...] = jnp.zeros_like(kt_ref)
        v_ref[...] = jnp.zeros_like(v_ref)

    @pl.when(j > 0)
    def _():
        z = _rms(x_ref[0], g_ref[...]).astype(BF16)

        def proj(c0, width):
            return jnp.dot(z, w_ref[:, c0:c0 + width], preferred_element_type=F32)

        def head_norm(t, gain):
            ssum = jnp.dot((t * t).astype(BF16), bd_ref[...], preferred_element_type=F32)
            return t * lax.rsqrt(ssum * (1.0 / HEAD_DIM) + EPS) * gain

        q = head_norm(proj(0, ATT_W), gq_ref[...]) * (HEAD_DIM ** -0.5 * LOG2_E)
        q_ref[0] = q.astype(BF16)
        k = head_norm(proj(ATT_W, ATT_W), gk_ref[...])
        kt_ref[0] = k.T.astype(BF16)
        v_ref[0] = proj(2 * ATT_W, ATT_W).astype(BF16)

        gu = _gelu(proj(3 * ATT_W, G_W))
        gv = _gelu(proj(3 * ATT_W + G_W, G_W))
        row = lax.broadcasted_iota(jnp.int32, (G_BLOCK, G_BLOCK), 0) // CHUNK
        col = lax.broadcasted_iota(jnp.int32, (G_BLOCK, G_BLOCK), 1) // CHUNK
        tri = row >= col
        cols = []
        for g in range(G_GROUPS):
            sl = slice(g * G_DIM, (g + 1) * G_DIM)
            vn = _rms(gv[:, sl], gvn_ref[g:g + 1, :]).astype(BF16)
            wm = jnp.where(tri, ws_ref[g], 0.0).astype(BF16)
            blocks = []
            for n in range(TM // G_BLOCK):
                rs = slice(n * G_BLOCK, (n + 1) * G_BLOCK)
                gate = jnp.dot(wm, vn[rs], preferred_element_type=F32) + bst_ref[:, g:g + 1]
                blocks.append(gu[rs, sl] * gate)
            cols.append(jnp.concatenate(blocks, axis=0))
        gm = jnp.concatenate(cols, axis=1)
        gm_ref[0] = _rms(gm, gout_ref[...]).astype(BF16)


def _inproj(x, g_mix, w_in, gq, gk, bd, gvn, ws, bst, gout):
    b, s, d = x.shape
    nt = s // TM
    xmap = lambda bi, j: (bi, jnp.maximum(j - 1, 0), 0)
    const2 = lambda bi, j: (0, 0)
    return pl.pallas_call(
        _inproj_kernel,
        grid=(b, nt + 1),
        in_specs=[
            pl.BlockSpec((1, TM, d), xmap),
            pl.BlockSpec((1, d), const2),
            pl.BlockSpec(w_in.shape, const2),
            pl.BlockSpec((1, ATT_W), const2),
            pl.BlockSpec((1, ATT_W), const2),
            pl.BlockSpec((ATT_W, ATT_W), const2),
            pl.BlockSpec((G_GROUPS, G_DIM), const2),
            pl.BlockSpec((G_GROUPS, G_BLOCK, G_BLOCK), lambda bi, j: (0, 0, 0)),
            pl.BlockSpec((G_BLOCK, G_GROUPS), const2),
            pl.BlockSpec((1, G_W), const2),
        ],
        out_specs=[
            pl.BlockSpec((1, TM, ATT_W), xmap),
            pl.BlockSpec((1, ATT_W, TM), lambda bi, j: (bi, 0, j)),
            pl.BlockSpec((1, TM, ATT_W), lambda bi, j: (bi, j, 0)),
            pl.BlockSpec((1, TM, G_W), xmap),
        ],
        out_shape=[
            jax.ShapeDtypeStruct((b, s, ATT_W), BF16),
            jax.ShapeDtypeStruct((b, ATT_W, s + LEFT), BF16),
            jax.ShapeDtypeStruct((b, s + LEFT, ATT_W), BF16),
            jax.ShapeDtypeStruct((b, s, G_W), BF16),
        ],
        compiler_params=pltpu.CompilerParams(
            dimension_semantics=("arbitrary", "arbitrary"), vmem_limit_bytes=VMEM_LIMIT),
        name="inproj",
    )(x, g_mix, w_in, gq, gk, bd, gvn, ws, bst, gout)


def _attn_kernel(q_ref, kt_ref, v_ref, relvec_ref, gout_ref, o_ref, bias_ref):
    i = pl.program_id(1)
    q0 = pl.multiple_of(i * TQ, TQ)
    win = pl.ds(q0, KWIN)
    lane = lax.broadcasted_iota(jnp.int32, (1, 256), 1)

    @pl.when((pl.program_id(0) == 0) & (i == 0))
    def _():
        r = lax.broadcasted_iota(jnp.int32, (TQ, KWIN), 0)
        j = lax.broadcasted_iota(jnp.int32, (TQ, KWIN), 1)
        lo = (r // CHUNK) * CHUNK
        in_band = (j >= lo) & (j < lo + LEFT + CHUNK)
        for h in range(N_HEADS):
            shifted = pltpu.roll(jnp.broadcast_to(relvec_ref[h:h + 1, :], (TQ, BIAS_LANES)), 0, 1,
                                 stride=1, stride_axis=0)
            bias_ref[h] = jnp.where(in_band, shifted[:, :KWIN], NEG_INF)

    def tile(has_left_padding):
        if has_left_padding:
            colpos = lax.broadcasted_iota(jnp.int32, (1, KWIN), 1)
            pad_bias = jnp.where(colpos >= LEFT - q0, 0.0, NEG_INF).astype(F32)
        outs = []
        for g in range(2):
            gs = slice(g * 256, (g + 1) * 256)
            qg = q_ref[0, :, gs]
            ktg = kt_ref[0, gs, win]
            vg = v_ref[0, win, gs]
            acc = jnp.zeros((TQ, 256), F32)
            for hh in range(4):
                h = 4 * g + hh
                hmask = (lane >= hh * HEAD_DIM) & (lane < (hh + 1) * HEAD_DIM)
                qh = jnp.where(hmask, qg, jnp.zeros_like(qg))
                s = jnp.dot(qh, ktg, preferred_element_type=F32) + bias_ref[h]
                if has_left_padding:
                    s = s + pad_bias
                m = jnp.max(s, axis=-1, keepdims=True)
                e = jnp.exp2(s - m)
                inv = 1.0 / jnp.sum(e, axis=-1, keepdims=True)
                pv = jnp.dot(e.astype(BF16), vg, preferred_element_type=F32)
                acc = jnp.where(hmask, pv * inv, acc)
            outs.append(acc)
        att = jnp.concatenate(outs, axis=1)
        o_ref[0] = _rms(att, gout_ref[...]).astype(BF16)

    pl.when(q0 < LEFT)(lambda: tile(True))
    pl.when(q0 >= LEFT)(lambda: tile(False))


def _attention(q, kt, v, relvec, gout):
    b, s, _ = q.shape
    return pl.pallas_call(
        _attn_kernel,
        grid=(b, s // TQ),
        in_specs=[
            pl.BlockSpec((1, TQ, ATT_W), lambda bi, i: (bi, i, 0)),
            pl.BlockSpec((1, ATT_W, s + LEFT), lambda bi, i: (bi, 0, 0)),
            pl.BlockSpec((1, s + LEFT, ATT_W), lambda bi, i: (bi, 0, 0)),
            pl.BlockSpec(relvec.shape, lambda bi, i: (0, 0)),
            pl.BlockSpec((1, ATT_W), lambda bi, i: (0, 0)),
        ],
        out_specs=pl.BlockSpec((1, TQ, ATT_W), lambda bi, i: (bi, i, 0)),
        out_shape=jax.ShapeDtypeStruct((b, s, ATT_W), BF16),
        scratch_shapes=[pltpu.VMEM((N_HEADS, TQ, KWIN), F32)],
        compiler_params=pltpu.CompilerParams(
            dimension_semantics=("arbitrary", "arbitrary"), vmem_limit_bytes=VMEM_LIMIT),
        name="attention",
    )(q, kt, v, relvec, gout)


def _outproj_kernel(a_ref, gm_ref, x_ref, wo_ref, gmoe_ref, wrh_ref, wrl_ref, br_ref,
                    h_ref, xn_ref, slot_ref, gate_ref, cnt_ref):
    h = (x_ref[...]
         + jnp.dot(a_ref[...], wo_ref[:ATT_W, :], preferred_element_type=F32)
         + jnp.dot(gm_ref[...], wo_ref[ATT_W:, :], preferred_element_type=F32))
    h_ref[...] = h
    xn = _rms(h, gmoe_ref[...])
    xhi = xn.astype(BF16)
    xlo = (xn - xhi.astype(F32)).astype(BF16)
    xn_ref[...] = xhi

    nt = (((1,), (1,)), ((), ()))
    logits = (lax.dot_general(wrh_ref[...], xhi, nt, preferred_element_type=F32)
              + lax.dot_general(wrh_ref[...], xlo, nt, preferred_element_type=F32)
              + lax.dot_general(wrl_ref[...], xhi, nt, preferred_element_type=F32)
              + br_ref[...])
    eidx = lax.broadcasted_iota(jnp.int32, (N_EXPERTS, TM), 0)
    vals, idxs = [], []
    cur = logits
    for _ in range(TOP_K):
        m = jnp.max(cur, axis=0, keepdims=True)
        ik = jnp.min(jnp.where(cur == m, eidx, N_EXPERTS), axis=0, keepdims=True)
        vals.append(m)
        idxs.append(ik)
        cur = jnp.where(eidx == ik, -jnp.inf, cur)
    exps = [jnp.exp(v - vals[0]) for v in vals]
    tot = exps[0] + exps[1] + exps[2] + exps[3]
    gate_ref[...] = jnp.concatenate([e / tot for e in exps], axis=0)

    onehots = [eidx == ik for ik in idxs]
    member = sum(oh.astype(F32) for oh in onehots).astype(BF16)
    earlier_tok = (lax.broadcasted_iota(jnp.int32, (TM, TM), 0)
                   < lax.broadcasted_iota(jnp.int32, (TM, TM), 1)).astype(BF16)
    earlier_exp = (lax.broadcasted_iota(jnp.int32, (N_EXPERTS, N_EXPERTS), 1)
                   < lax.broadcasted_iota(jnp.int32, (N_EXPERTS, N_EXPERTS), 0)).astype(BF16)
    rank = jnp.dot(member, earlier_tok, preferred_element_type=F32)
    lower = jnp.sum(jnp.dot(earlier_exp, member, preferred_element_type=F32), axis=1, keepdims=True)
    base = rank + lower
    slots = [jnp.sum(jnp.where(oh, base, 0.0), axis=0, keepdims=True) for oh in onehots]
    slot_ref[...] = jnp.concatenate(slots, axis=0).astype(jnp.int32)
    cnt_ref[0] = jnp.sum(member.astype(F32), axis=1, keepdims=True).astype(jnp.int32)


def _outproj(att, gm, x2, w_out, g_moe, wr_hi, wr_lo, br):
    t, d = x2.shape
    tile = lambda i: (i, 0)
    const = lambda i: (0, 0)
    lanes = lambda i: (0, i)
    return pl.pallas_call(
        _outproj_kernel,
        grid=(t // TM,),
        in_specs=[
            pl.BlockSpec((TM, ATT_W), tile),
            pl.BlockSpec((TM, G_W), tile),
            pl.BlockSpec((TM, d), tile),
            pl.BlockSpec(w_out.shape, const),
            pl.BlockSpec((1, d), const),
            pl.BlockSpec((N_EXPERTS, d), const),
            pl.BlockSpec((N_EXPERTS, d), const),
            pl.BlockSpec((N_EXPERTS, 1), const),
        ],
        out_specs=[
            pl.BlockSpec((TM, d), tile),
            pl.BlockSpec((TM, d), tile),
            pl.BlockSpec((TOP_K, TM), lanes),
            pl.BlockSpec((TOP_K, TM), lanes),
            pl.BlockSpec((1, N_EXPERTS, 1), lambda i: (i, 0, 0)),
        ],
        out_shape=[
            jax.ShapeDtypeStruct((t, d), F32),
            jax.ShapeDtypeStruct((t, d), BF16),
            jax.ShapeDtypeStruct((TOP_K, t), jnp.int32),
            jax.ShapeDtypeStruct((TOP_K, t), F32),
            jax.ShapeDtypeStruct((t // TM, N_EXPERTS, 1), jnp.int32),
        ],
        compiler_params=pltpu.CompilerParams(
            dimension_semantics=("arbitrary",), vmem_limit_bytes=VMEM_LIMIT),
        name="outproj_router",
    )(att, gm, x2, w_out, g_moe, wr_hi, wr_lo, br)


def _dispatch_kernel(cnt_ref, loc_ref, glb_ref, pend_ref, xn_ref, slot_ref, xpad_ref,
                     sorted_ref, zero_ref, sem, zsem):
    i = pl.program_id(0)
    n = pl.num_programs(0)
    buf = i % 2

    def drain(b):
        pltpu.make_async_copy(sorted_ref.at[b], _slab(xpad_ref, 0, N_SLOTS), sem.at[b]).wait()

    @pl.when(i == 0)
    def _():
        zero_ref[...] = jnp.zeros_like(zero_ref)

        def last_block(e):
            return pltpu.make_async_copy(zero_ref, _slab(xpad_ref, pend_ref[e] - TME, TME), zsem)

        def has_rows(e):
            return pend_ref[e] > jnp.where(e > 0, pend_ref[jnp.maximum(e - 1, 0)], 0)

        def start(e, c):
            pl.when(has_rows(e))(lambda: last_block(e).start())
            return c

        def wait(e, c):
            pl.when(has_rows(e))(lambda: last_block(e).wait())
            return c

        lax.fori_loop(0, N_EXPERTS, start, 0)
        lax.fori_loop(0, N_EXPERTS, wait, 0)

        def tail_block(blk):
            return pltpu.make_async_copy(zero_ref, _slab(xpad_ref, blk * TME, TME), zsem)

        first_tail = pend_ref[N_EXPERTS - 1] // TME
        n_blocks = xpad_ref.shape[0] // (TME * ROW_SUB)
        lax.fori_loop(first_tail, n_blocks, lambda blk, c: (tail_block(blk).start(), c)[1], 0)
        lax.fori_loop(first_tail, n_blocks, lambda blk, c: (tail_block(blk).wait(), c)[1], 0)

    pl.when(i >= 2)(lambda: drain(buf))

    xn = xn_ref[...]
    for c in range(N_SLOTS // PCHUNK):
        pos = lax.broadcasted_iota(jnp.int32, (PCHUNK, TM), 0) + c * PCHUNK
        hit = pos == slot_ref[0:1, :]
        for k in range(1, TOP_K):
            hit = hit | (pos == slot_ref[k:k + 1, :])
        rows = jnp.dot(hit.astype(F32).astype(BF16), xn, preferred_element_type=F32)
        for sub in range(ROW_SUB):
            sorted_ref[buf, pl.ds(c * PCHUNK * ROW_SUB + sub, PCHUNK, stride=ROW_SUB), :] = (
                rows[:, sub * LANES:(sub + 1) * LANES])

    _segment_copies(
        i, cnt_ref, loc_ref, glb_ref,
        lambda loc, glb, size: pltpu.make_async_copy(
            _slab_at(sorted_ref.at[buf], loc, size), _slab_at(xpad_ref, glb, size), sem.at[buf]))

    @pl.when(i == n - 1)
    def _():
        drain(buf)
        pl.when(n >= 2)(lambda: drain(1 - buf))


def _dispatch(tcnt, tloc, tglb, pad_ends, xn, slots, n_rows):
    t, d = xn.shape
    return pl.pallas_call(
        _dispatch_kernel,
        grid_spec=pltpu.PrefetchScalarGridSpec(
            num_scalar_prefetch=4,
            grid=(t // TM,),
            in_specs=[pl.BlockSpec((TM, d), lambda i, *_: (i, 0)),
                      pl.BlockSpec((TOP_K, TM), lambda i, *_: (0, i))],
            out_specs=pl.BlockSpec(memory_space=pl.ANY),
            scratch_shapes=[pltpu.VMEM((2, N_SLOTS * ROW_SUB, LANES), F32),
                            pltpu.VMEM((TME * ROW_SUB, LANES), F32),
                            pltpu.SemaphoreType.DMA((2,)), pltpu.SemaphoreType.DMA],
        ),
        out_shape=jax.ShapeDtypeStruct((n_rows * ROW_SUB, LANES), F32),
        compiler_params=pltpu.CompilerParams(
            dimension_semantics=("arbitrary",), vmem_limit_bytes=VMEM_LIMIT),
        name="dispatch",
    )(tcnt, tloc, tglb, pad_ends, xn, slots)


def _expert_kernel(be_ref, nvb_ref, run_ref, rexp_ref, nrun_ref, x_ref, wi_hbm, bi_ref, wo_hbm, bo_ref,
                   y_ref, wi32_ref, wo32_ref, wib_ref, wob_ref, wsem):
    i = pl.program_id(0)
    run = run_ref[i]
    buf = run % 2

    def fetch(r, b):
        e = rexp_ref[r]
        return (pltpu.make_async_copy(wi_hbm.at[e], wi32_ref.at[b], wsem.at[b]),
                pltpu.make_async_copy(wo_hbm.at[e], wo32_ref.at[b], wsem.at[b]))

    @pl.when(i == 0)
    def _():
        for copy in fetch(0, 0):
            copy.start()

    @pl.when((i == 0) | (run != run_ref[jnp.maximum(i - 1, 0)]))
    def _():
        for copy in fetch(run, buf):
            copy.wait()

        @pl.when(run + 1 < nrun_ref[0])
        def _():
            for copy in fetch(run + 1, 1 - buf):
                copy.start()

        wib_ref[...] = wi32_ref[buf].astype(BF16)
        wob_ref[...] = wo32_ref[buf].astype(BF16)

    @pl.when(i < nvb_ref[0])
    def _():
        x = jnp.concatenate(
            [x_ref[pl.ds(sub, TME, stride=ROW_SUB), :].astype(BF16) for sub in range(ROW_SUB)],
            axis=1)
        hdn = jnp.dot(x, wib_ref[...], preferred_element_type=F32) + bi_ref[0]
        gl = jnp.minimum(hdn[:, :D_EXPERT], SWIGLU_LIMIT)
        lin = jnp.clip(hdn[:, D_EXPERT:], -SWIGLU_LIMIT, SWIGLU_LIMIT)
        act = gl * jax.nn.sigmoid(SWIGLU_ALPHA * gl) * (lin + 1.0)
        y = jnp.dot(act.astype(BF16), wob_ref[...], preferred_element_type=F32) + bo_ref[0]
        for sub in range(ROW_SUB):
            y_ref[pl.ds(sub, TME, stride=ROW_SUB), :] = y[:, sub * LANES:(sub + 1) * LANES]

    @pl.when(i >= nvb_ref[0])
    def _():
        y_ref[...] = jnp.zeros_like(y_ref)


def _experts(block_e, nvb, block_run, run_expert, n_runs, x_pad, w_in, b_in, w_out, b_out):
    n_rows = x_pad.shape[0] // ROW_SUB
    d = D_MODEL
    nb = n_rows // TME
    rows = lambda i, be, nv, *_: (jnp.minimum(i, nv[0] - 1), 0)
    wsel = lambda i, be, *_: (be[i], 0, 0)
    return pl.pallas_call(
        _expert_kernel,
        grid_spec=pltpu.PrefetchScalarGridSpec(
            num_scalar_prefetch=5,
            grid=(nb,),
            in_specs=[
                pl.BlockSpec((TME * ROW_SUB, LANES), rows),
                pl.BlockSpec(memory_space=pl.ANY),
                pl.BlockSpec((1, 1, 2 * D_EXPERT), wsel),
                pl.BlockSpec(memory_space=pl.ANY),
                pl.BlockSpec((1, 1, d), wsel),
            ],
            out_specs=pl.BlockSpec((TME * ROW_SUB, LANES), lambda i, *_: (i, 0)),
            scratch_shapes=[pltpu.VMEM((2, d, 2 * D_EXPERT), F32), pltpu.VMEM((2, D_EXPERT, d), F32),
                            pltpu.VMEM((d, 2 * D_EXPERT), BF16), pltpu.VMEM((D_EXPERT, d), BF16),
                            pltpu.SemaphoreType.DMA((2,))],
        ),
        out_shape=jax.ShapeDtypeStruct((n_rows * ROW_SUB, LANES), F32),
        compiler_params=pltpu.CompilerParams(
            dimension_semantics=("arbitrary",), vmem_limit_bytes=VMEM_LIMIT),
        name="experts",
    )(block_e, nvb, block_run, run_expert, n_runs, x_pad, w_in, b_in, w_out, b_out)


def _combine_kernel(cnt_ref, loc_ref, glb_ref, ypad_ref, h_ref, slot_ref, gate_ref, o_ref,
                    sorted_ref, sem):
    i = pl.program_id(0)
    n = pl.num_programs(0)
    buf = i % 2

    def fetch(tile, b):
        _segment_copies(
            tile, cnt_ref, loc_ref, glb_ref,
            lambda loc, glb, size: pltpu.make_async_copy(
                _slab_at(ypad_ref, glb, size), _slab_at(sorted_ref.at[b], loc, size), sem.at[b]))

    pl.when(i == 0)(lambda: fetch(0, 0))
    pl.when(i + 1 < n)(lambda: fetch(i + 1, 1 - buf))
    pltpu.make_async_copy(_slab(ypad_ref, 0, N_SLOTS), sorted_ref.at[buf], sem.at[buf]).wait()

    acc = h_ref[...]
    for c in range(N_SLOTS // PCHUNK):
        pos = lax.broadcasted_iota(jnp.int32, (TM, PCHUNK), 1) + c * PCHUNK
        weight = jnp.zeros((TM, PCHUNK), F32)
        for k in range(TOP_K):
            weight = weight + jnp.where(pos == slot_ref[:, k:k + 1], gate_ref[:, k:k + 1], 0.0)
        y = jnp.concatenate(
            [sorted_ref[buf, pl.ds(c * PCHUNK * ROW_SUB + sub, PCHUNK, stride=ROW_SUB), :].astype(BF16)
             for sub in range(ROW_SUB)], axis=1)
        acc = acc + jnp.dot(weight.astype(BF16), y, preferred_element_type=F32)
    o_ref[...] = acc


def _combine(tcnt, tloc, tglb, y_pad, h, slots_tk, gates_tk):
    t, d = h.shape
    tile = lambda i, *_: (i, 0)
    return pl.pallas_call(
        _combine_kernel,
        grid_spec=pltpu.PrefetchScalarGridSpec(
            num_scalar_prefetch=3,
            grid=(t // TM,),
            in_specs=[
                pl.BlockSpec(memory_space=pl.ANY),
                pl.BlockSpec((TM, d), tile),
                pl.BlockSpec((TM, TOP_K), tile),
                pl.BlockSpec((TM, TOP_K), tile),
            ],
            out_specs=pl.BlockSpec((TM, d), tile),
            scratch_shapes=[pltpu.VMEM((2, N_SLOTS * ROW_SUB, LANES), F32),
                            pltpu.SemaphoreType.DMA((2,))],
        ),
        out_shape=jax.ShapeDtypeStruct((t, d), F32),
        compiler_params=pltpu.CompilerParams(
            dimension_semantics=("arbitrary",), vmem_limit_bytes=VMEM_LIMIT),
        name="combine",
    )(tcnt, tloc, tglb, y_pad, h, slots_tk, gates_tk)


def _rel_vector(rel_bias):
    n_heads = rel_bias.shape[0]
    far = LEFT - REL_CLIP
    falling = rel_bias[:, ::-1][:, 1:]
    n_fall = min(KWIN - 1 - far, 2 * REL_CLIP)
    vec = jnp.concatenate(
        [jnp.broadcast_to(rel_bias[:, -1:], (n_heads, far + 1)), falling[:, :n_fall],
         jnp.broadcast_to(rel_bias[:, :1], (n_heads, KWIN - 1 - far - n_fall)),
         jnp.broadcast_to(rel_bias[:, -1:], (n_heads, BIAS_LANES - KWIN))], axis=1)
    return vec.astype(F32) * LOG2_E


def kernel(x, norm_mix_g, w_in_proj, q_norm_g, k_norm_g, rel_bias, gmlp_v_norm_g, gmlp_w_s, gmlp_b_s,
           att_out_norm_g, gmlp_out_norm_g, w_out_proj, norm_moe_g, w_router, b_router,
           w_expert_in, b_expert_in, w_expert_out, b_expert_out):
    b, s, d = x.shape
    t = b * s

    head_of = jnp.arange(ATT_W) // HEAD_DIM
    block_diag = (head_of[:, None] == head_of[None, :]).astype(BF16)
    q, kt, v, gm = _inproj(
        x, norm_mix_g.reshape(1, d), w_in_proj.astype(BF16),
        jnp.tile(q_norm_g, N_HEADS).reshape(1, ATT_W), jnp.tile(k_norm_g, N_HEADS).reshape(1, ATT_W),
        block_diag, gmlp_v_norm_g, gmlp_w_s, gmlp_b_s.T, gmlp_out_norm_g.reshape(1, G_W))
    att = _attention(q, kt, v, _rel_vector(rel_bias), att_out_norm_g.reshape(1, ATT_W))

    wr_t = w_router.T
    wr_hi = wr_t.astype(BF16)
    wr_lo = (wr_t - wr_hi.astype(F32)).astype(BF16)
    h, xn, slots, gates, tile_cnt = _outproj(
        att.reshape(t, ATT_W), gm.reshape(t, G_W), x.reshape(t, d), w_out_proj.astype(BF16),
        norm_moe_g.reshape(1, d), wr_hi, wr_lo, b_router.reshape(N_EXPERTS, 1))

    tcnt = tile_cnt[:, :, 0]
    counts = jnp.sum(tcnt, axis=0)
    padded = ((counts + TME - 1) // TME) * TME
    pad_ends = jnp.cumsum(padded).astype(jnp.int32)
    pad_starts = pad_ends - padded
    tloc = jnp.cumsum(tcnt, axis=1) - tcnt
    tglb = pad_starts[None, :] + jnp.cumsum(tcnt, axis=0) - tcnt
    n_blocks = (t * TOP_K) // TME + N_EXPERTS
    nvb = pad_ends[-1] // TME
    block_start = jnp.arange(n_blocks, dtype=jnp.int32) * TME
    block_e = jnp.minimum(jnp.sum(pad_ends[None, :] <= block_start[:, None], axis=1), N_EXPERTS - 1)
    block_e = jnp.where(jnp.arange(n_blocks) < nvb, block_e, block_e[nvb - 1]).astype(jnp.int32)
    runs = tuple((a * ROW_SUB).reshape(-1).astype(jnp.int32) for a in (tcnt, tloc, tglb))
    present = padded > 0
    run_of_expert = jnp.cumsum(present) - 1
    block_run = run_of_expert[block_e].astype(jnp.int32)
    experts_iota = jnp.arange(N_EXPERTS)
    run_expert = jnp.sum(jnp.where(present[None, :] & (run_of_expert[None, :] == experts_iota[:, None]),
                                   experts_iota[None, :], 0), axis=1).astype(jnp.int32)
    n_runs = jnp.sum(present).astype(jnp.int32).reshape(1)

    x_pad = _dispatch(*runs, pad_ends, xn, slots, n_blocks * TME)
    y_pad = _experts(block_e, nvb.reshape(1), block_run, run_expert, n_runs, x_pad, w_expert_in,
                     b_expert_in.reshape(N_EXPERTS, 1, 2 * D_EXPERT), w_expert_out,
                     b_expert_out.reshape(N_EXPERTS, 1, d))
    out = _combine(*runs, y_pad, h, slots.T, gates.T)
    return out.reshape(b, s, d)
```

```python
import jax
import jax.numpy as jnp
from jax import lax
from jax.experimental import pallas as pl
from jax.experimental.pallas import tpu as pltpu

D_MODEL = 1024
CHUNK = 64
LEFT = 8 * CHUNK
N_HEADS = 8
HEAD_DIM = 64
ATT_W = 512
REL_CLIP = 128
G_GROUPS = 4
G_DIM = 128
G_BLOCK = 128
G_W = 512
N_EXPERTS = 32
TOP_K = 4
D_EXPERT = 1024
SWIGLU_LIMIT = 7.0
SWIGLU_ALPHA = 1.702
EPS = 1e-6
NEG_INF = -1e30
LOG2_E = 1.4426950408889634

TM = 512
TQ = 256
KWIN = TQ + LEFT
BIAS_LANES = 1024
TME = 256
N_SLOTS = TM * TOP_K
PCHUNK = 256
SEG_BITS = TM.bit_length()
VMEM_LIMIT = 56 * 1024 * 1024
LANES = 128
ROW_SUB = D_MODEL // LANES

F32 = jnp.float32
BF16 = jnp.bfloat16


def _rms(x, gain):
    ms = jnp.mean(x * x, axis=-1, keepdims=True)
    return x * lax.rsqrt(ms + EPS) * gain


def _gelu(x):
    return 0.5 * x * (1.0 + lax.erf(x * (2.0 ** -0.5)))


def _slab(ref, row, n_rows=1):
    return ref.at[pl.ds(pl.multiple_of(row * ROW_SUB, ROW_SUB), n_rows * ROW_SUB)]


def _slab_at(ref, offset, n_rows):
    return ref.at[pl.ds(pl.multiple_of(offset, ROW_SUB), n_rows * ROW_SUB)]


def _segment_copies(tile, cnt_ref, loc_ref, glb_ref, make_copy, enabled=None):
    def per_expert(e, c):
        n = cnt_ref[tile * N_EXPERTS + e]
        if enabled is not None:
            n = jnp.where(enabled, n, 0)
        loc = loc_ref[tile * N_EXPERTS + e]
        glb = glb_ref[tile * N_EXPERTS + e]
        for bit in range(SEG_BITS):
            size = 1 << bit
            off = n & ~(2 * size * ROW_SUB - 1)

            @pl.when((n & (size * ROW_SUB)) != 0)
            def _():
                make_copy(loc + off, glb + off, size).start()
        return c

    if enabled is None:
        lax.fori_loop(0, N_EXPERTS, per_expert, 0, unroll=2)
    else:
        for e in range(N_EXPERTS):
            per_expert(e, 0)


def _inproj_kernel(x_ref, g_ref, w_ref, gq_ref, gk_ref, bd_ref, gvn_ref, ws_ref, bst_ref, gout_ref,
                   q_ref, kt_ref, v_ref, gm_ref):
    j = pl.program_id(1)

    @pl.when(j == 0)
    def _():
        kt_ref[...] = jnp.zeros_like(kt_ref)
        v_ref[...] = jnp.zeros_like(v_ref)

    @pl.when(j > 0)
    def _():
        z = _rms(x_ref[0], g_ref[...]).astype(BF16)

        def proj(c0, width):
            return jnp.dot(z, w_ref[:, c0:c0 + width], preferred_element_type=F32)

        def head_norm(t, gain):
            ssum = jnp.dot((t * t).astype(BF16), bd_ref[...], preferred_element_type=F32)
            return t * lax.rsqrt(ssum * (1.0 / HEAD_DIM) + EPS) * gain

        q = head_norm(proj(0, ATT_W), gq_ref[...]) * (HEAD_DIM ** -0.5 * LOG2_E)
        q_ref[0] = q.astype(BF16)
        k = head_norm(proj(ATT_W, ATT_W), gk_ref[...])
        kt_ref[0] = k.T.astype(BF16)
        v_ref[0] = proj(2 * ATT_W, ATT_W).astype(BF16)

        gu = _gelu(proj(3 * ATT_W, G_W))
        gv = _gelu(proj(3 * ATT_W + G_W, G_W))
        row = lax.broadcasted_iota(jnp.int32, (G_BLOCK, G_BLOCK), 0) // CHUNK
        col = lax.broadcasted_iota(jnp.int32, (G_BLOCK, G_BLOCK), 1) // CHUNK
        tri = row >= col
        cols = []
        for g in range(G_GROUPS):
            sl = slice(g * G_DIM, (g + 1) * G_DIM)
            vn = _rms(gv[:, sl], gvn_ref[g:g + 1, :]).astype(BF16)
            wm = jnp.where(tri, ws_ref[g], 0.0).astype(BF16)
            blocks = []
            for n in range(TM // G_BLOCK):
                rs = slice(n * G_BLOCK, (n + 1) * G_BLOCK)
                gate = jnp.dot(wm, vn[rs], preferred_element_type=F32) + bst_ref[:, g:g + 1]
                blocks.append(gu[rs, sl] * gate)
            cols.append(jnp.concatenate(blocks, axis=0))
        gm = jnp.concatenate(cols, axis=1)
        gm_ref[0] = _rms(gm, gout_ref[...]).astype(BF16)


def _inproj(x, g_mix, w_in, gq, gk, bd, gvn, ws, bst, gout):
    b, s, d = x.shape
    nt = s // TM
    xmap = lambda bi, j: (bi, jnp.maximum(j - 1, 0), 0)
    const2 = lambda bi, j: (0, 0)
    return pl.pallas_call(
        _inproj_kernel,
        grid=(b, nt + 1),
        in_specs=[
            pl.BlockSpec((1, TM, d), xmap),
            pl.BlockSpec((1, d), const2),
            pl.BlockSpec(w_in.shape, const2),
            pl.BlockSpec((1, ATT_W), const2),
            pl.BlockSpec((1, ATT_W), const2),
            pl.BlockSpec((ATT_W, ATT_W), const2),
            pl.BlockSpec((G_GROUPS, G_DIM), const2),
            pl.BlockSpec((G_GROUPS, G_BLOCK, G_BLOCK), lambda bi, j: (0, 0, 0)),
            pl.BlockSpec((G_BLOCK, G_GROUPS), const2),
            pl.BlockSpec((1, G_W), const2),
        ],
        out_specs=[
            pl.BlockSpec((1, TM, ATT_W), xmap),
            pl.BlockSpec((1, ATT_W, TM), lambda bi, j: (bi, 0, j)),
            pl.BlockSpec((1, TM, ATT_W), lambda bi, j: (bi, j, 0)),
            pl.BlockSpec((1, TM, G_W), xmap),
        ],
        out_shape=[
            jax.ShapeDtypeStruct((b, s, ATT_W), BF16),
            jax.ShapeDtypeStruct((b, ATT_W, s + LEFT), BF16),
            jax.ShapeDtypeStruct((b, s + LEFT, ATT_W), BF16),
            jax.ShapeDtypeStruct((b, s, G_W), BF16),
        ],
        compiler_params=pltpu.CompilerParams(
            dimension_semantics=("arbitrary", "arbitrary"), vmem_limit_bytes=VMEM_LIMIT),
        name="inproj",
    )(x, g_mix, w_in, gq, gk, bd, gvn, ws, bst, gout)


def _attn_kernel(q_ref, kt_ref, v_ref, relvec_ref, gout_ref, o_ref, bias_ref):
    i = pl.program_id(1)
    q0 = pl.multiple_of(i * TQ, TQ)
    win = pl.ds(q0, KWIN)
    lane = lax.broadcasted_iota(jnp.int32, (1, 256), 1)

    @pl.when((pl.program_id(0) == 0) & (i == 0))
    def _():
        r = lax.broadcasted_iota(jnp.int32, (TQ, KWIN), 0)
        j = lax.broadcasted_iota(jnp.int32, (TQ, KWIN), 1)
        lo = (r // CHUNK) * CHUNK
        in_band = (j >= lo) & (j < lo + LEFT + CHUNK)
        for h in range(N_HEADS):
            shifted = pltpu.roll(jnp.broadcast_to(relvec_ref[h:h + 1, :], (TQ, BIAS_LANES)), 0, 1,
                                 stride=1, stride_axis=0)
            bias_ref[h] = jnp.where(in_band, shifted[:, :KWIN], NEG_INF)

    def tile(has_left_padding):
        if has_left_padding:
            colpos = lax.broadcasted_iota(jnp.int32, (1, KWIN), 1)
            pad_bias = jnp.where(colpos >= LEFT - q0, 0.0, NEG_INF).astype(F32)
        outs = []
        for g in range(2):
            gs = slice(g * 256, (g + 1) * 256)
            qg = q_ref[0, :, gs]
            ktg = kt_ref[0, gs, win]
            vg = v_ref[0, win, gs]
            acc = jnp.zeros((TQ, 256), F32)
            for hh in range(4):
                h = 4 * g + hh
                hmask = (lane >= hh * HEAD_DIM) & (lane < (hh + 1) * HEAD_DIM)
                qh = jnp.where(hmask, qg, jnp.zeros_like(qg))
                s = jnp.dot(qh, ktg, preferred_element_type=F32) + bias_ref[h]
                if has_left_padding:
                    s = s + pad_bias
                m = jnp.max(s, axis=-1, keepdims=True)
                e = jnp.exp2(s - m)
                inv = 1.0 / jnp.sum(e, axis=-1, keepdims=True)
                pv = jnp.dot(e.astype(BF16), vg, preferred_element_type=F32)
                acc = jnp.where(hmask, pv * inv, acc)
            outs.append(acc)
        att = jnp.concatenate(outs, axis=1)
        o_ref[0] = _rms(att, gout_ref[...]).astype(BF16)

    pl.when(q0 < LEFT)(lambda: tile(True))
    pl.when(q0 >= LEFT)(lambda: tile(False))


def _attention(q, kt, v, relvec, gout):
    b, s, _ = q.shape
    return pl.pallas_call(
        _attn_kernel,
        grid=(b, s // TQ),
        in_specs=[
            pl.BlockSpec((1, TQ, ATT_W), lambda bi, i: (bi, i, 0)),
            pl.BlockSpec((1, ATT_W, s + LEFT), lambda bi, i: (bi, 0, 0)),
            pl.BlockSpec((1, s + LEFT, ATT_W), lambda bi, i: (bi, 0, 0)),
            pl.BlockSpec(relvec.shape, lambda bi, i: (0, 0)),
            pl.BlockSpec((1, ATT_W), lambda bi, i: (0, 0)),
        ],
        out_specs=pl.BlockSpec((1, TQ, ATT_W), lambda bi, i: (bi, i, 0)),
        out_shape=jax.ShapeDtypeStruct((b, s, ATT_W), BF16),
        scratch_shapes=[pltpu.VMEM((N_HEADS, TQ, KWIN), F32)],
        compiler_params=pltpu.CompilerParams(
            dimension_semantics=("arbitrary", "arbitrary"), vmem_limit_bytes=VMEM_LIMIT),
        name="attention",
    )(q, kt, v, relvec, gout)


def _outproj_kernel(a_ref, gm_ref, x_ref, wo_ref, gmoe_ref, wrh_ref, wrl_ref, br_ref,
                    h_ref, xn_ref, slot_ref, gate_ref, cnt_ref):
    h = (x_ref[...]
         + jnp.dot(a_ref[...], wo_ref[:ATT_W, :], preferred_element_type=F32)
         + jnp.dot(gm_ref[...], wo_ref[ATT_W:, :], preferred_element_type=F32))
    h_ref[...] = h
    xn = _rms(h, gmoe_ref[...])
    xhi = xn.astype(BF16)
    xlo = (xn - xhi.astype(F32)).astype(BF16)
    xn_ref[...] = xhi

    nt = (((1,), (1,)), ((), ()))
    logits = (lax.dot_general(wrh_ref[...], xhi, nt, preferred_element_type=F32)
              + lax.dot_general(wrh_ref[...], xlo, nt, preferred_element_type=F32)
              + lax.dot_general(wrl_ref[...], xhi, nt, preferred_element_type=F32)
              + br_ref[...])
    eidx = lax.broadcasted_iota(jnp.int32, (N_EXPERTS, TM), 0)
    vals, idxs = [], []
    cur = logits
    for _ in range(TOP_K):
        m = jnp.max(cur, axis=0, keepdims=True)
        ik = jnp.min(jnp.where(cur == m, eidx, N_EXPERTS), axis=0, keepdims=True)
        vals.append(m)
        idxs.append(ik)
        cur = jnp.where(eidx == ik, -jnp.inf, cur)
    exps = [jnp.exp(v - vals[0]) for v in vals]
    tot = exps[0] + exps[1] + exps[2] + exps[3]
    gate_ref[...] = jnp.concatenate([e / tot for e in exps], axis=0)

    onehots = [eidx == ik for ik in idxs]
    member = sum(oh.astype(F32) for oh in onehots).astype(BF16)
    earlier_tok = (lax.broadcasted_iota(jnp.int32, (TM, TM), 0)
                   < lax.broadcasted_iota(jnp.int32, (TM, TM), 1)).astype(BF16)
    earlier_exp = (lax.broadcasted_iota(jnp.int32, (N_EXPERTS, N_EXPERTS), 1)
                   < lax.broadcasted_iota(jnp.int32, (N_EXPERTS, N_EXPERTS), 0)).astype(BF16)
    rank = jnp.dot(member, earlier_tok, preferred_element_type=F32)
    lower = jnp.sum(jnp.dot(earlier_exp, member, preferred_element_type=F32), axis=1, keepdims=True)
    base = rank + lower
    slots = [jnp.sum(jnp.where(oh, base, 0.0), axis=0, keepdims=True) for oh in onehots]
    slot_ref[...] = jnp.concatenate(slots, axis=0).astype(jnp.int32)
    cnt_ref[0] = jnp.sum(member.astype(F32), axis=1, keepdims=True).astype(jnp.int32)


def _outproj(att, gm, x2, w_out, g_moe, wr_hi, wr_lo, br):
    t, d = x2.shape
    tile = lambda i: (i, 0)
    const = lambda i: (0, 0)
    lanes = lambda i: (0, i)
    return pl.pallas_call(
        _outproj_kernel,
        grid=(t // TM,),
        in_specs=[
            pl.BlockSpec((TM, ATT_W), tile),
            pl.BlockSpec((TM, G_W), tile),
            pl.BlockSpec((TM, d), tile),
            pl.BlockSpec(w_out.shape, const),
            pl.BlockSpec((1, d), const),
            pl.BlockSpec((N_EXPERTS, d), const),
            pl.BlockSpec((N_EXPERTS, d), const),
            pl.BlockSpec((N_EXPERTS, 1), const),
        ],
        out_specs=[
            pl.BlockSpec((TM, d), tile),
            pl.BlockSpec((TM, d), tile),
            pl.BlockSpec((TOP_K, TM), lanes),
            pl.BlockSpec((TOP_K, TM), lanes),
            pl.BlockSpec((1, N_EXPERTS, 1), lambda i: (i, 0, 0)),
        ],
        out_shape=[
            jax.ShapeDtypeStruct((t, d), F32),
            jax.ShapeDtypeStruct((t, d), BF16),
            jax.ShapeDtypeStruct((TOP_K, t), jnp.int32),
            jax.ShapeDtypeStruct((TOP_K, t), F32),
            jax.ShapeDtypeStruct((t // TM, N_EXPERTS, 1), jnp.int32),
        ],
        compiler_params=pltpu.CompilerParams(
            dimension_semantics=("arbitrary",), vmem_limit_bytes=VMEM_LIMIT),
        name="outproj_router",
    )(att, gm, x2, w_out, g_moe, wr_hi, wr_lo, br)


def _dispatch_kernel(cnt_ref, loc_ref, glb_ref, pend_ref, xn_ref, slot_ref, xpad_ref,
                     sorted0_ref, sorted1_ref, zero_ref, sem, zsem):
    i = pl.program_id(0)
    n = pl.num_programs(0)
    sorted_refs = (sorted0_ref, sorted1_ref)

    def drain(b):
        pltpu.make_async_copy(sorted_refs[b], _slab(xpad_ref, 0, N_SLOTS), sem.at[b]).wait()

    @pl.when(i == 0)
    def _():
        zero_ref[...] = jnp.zeros_like(zero_ref)

        def last_block(e):
            return pltpu.make_async_copy(zero_ref, _slab(xpad_ref, pend_ref[e] - TME, TME), zsem)

        def has_rows(e):
            return pend_ref[e] > jnp.where(e > 0, pend_ref[jnp.maximum(e - 1, 0)], 0)

        def start(e, c):
            pl.when(has_rows(e))(lambda: last_block(e).start())
            return c

        def wait(e, c):
            pl.when(has_rows(e))(lambda: last_block(e).wait())
            return c

        lax.fori_loop(0, N_EXPERTS, start, 0)
        lax.fori_loop(0, N_EXPERTS, wait, 0)

        def tail_block(blk):
            return pltpu.make_async_copy(zero_ref, _slab(xpad_ref, blk * TME, TME), zsem)

        first_tail = pend_ref[N_EXPERTS - 1] // TME
        n_blocks = xpad_ref.shape[0] // (TME * ROW_SUB)
        lax.fori_loop(first_tail, n_blocks, lambda blk, c: (tail_block(blk).start(), c)[1], 0)
        lax.fori_loop(first_tail, n_blocks, lambda blk, c: (tail_block(blk).wait(), c)[1], 0)

    def send(tile, b, enabled=None):
        _segment_copies(
            tile, cnt_ref, loc_ref, glb_ref,
            lambda loc, glb, size: pltpu.make_async_copy(
                _slab_at(sorted_refs[b], loc, size), _slab_at(xpad_ref, glb, size), sem.at[b]),
            enabled)

    def step(buf):
        pl.when(i >= 2)(lambda: drain(buf))
        send(jnp.maximum(i - 1, 0), 1 - buf, enabled=i >= 1)

        xn = xn_ref[...]
        for c in range(N_SLOTS // PCHUNK):
            pos = lax.broadcasted_iota(jnp.int32, (PCHUNK, TM), 0) + c * PCHUNK
            hit = pos == slot_ref[0:1, :]
            for k in range(1, TOP_K):
                hit = hit | (pos == slot_ref[k:k + 1, :])
            rows = jnp.dot(hit.astype(F32).astype(BF16), xn, preferred_element_type=F32)
            for sub in range(ROW_SUB):
                sorted_refs[buf][pl.ds(c * PCHUNK * ROW_SUB + sub, PCHUNK, stride=ROW_SUB), :] = (
                    rows[:, sub * LANES:(sub + 1) * LANES])

        @pl.when(i == n - 1)
        def _():
            send(i, buf)
            pl.when(i >= 1)(lambda: drain(1 - buf))
            drain(buf)

    pl.when(i % 2 == 0)(lambda: step(0))
    pl.when(i % 2 == 1)(lambda: step(1))


def _dispatch(tcnt, tloc, tglb, pad_ends, xn, slots, n_rows):
    t, d = xn.shape
    return pl.pallas_call(
        _dispatch_kernel,
        grid_spec=pltpu.PrefetchScalarGridSpec(
            num_scalar_prefetch=4,
            grid=(t // TM,),
            in_specs=[pl.BlockSpec((TM, d), lambda i, *_: (i, 0)),
                      pl.BlockSpec((TOP_K, TM), lambda i, *_: (0, i))],
            out_specs=pl.BlockSpec(memory_space=pl.ANY),
            scratch_shapes=[pltpu.VMEM((N_SLOTS * ROW_SUB, LANES), F32),
                            pltpu.VMEM((N_SLOTS * ROW_SUB, LANES), F32),
                            pltpu.VMEM((TME * ROW_SUB, LANES), F32),
                            pltpu.SemaphoreType.DMA((2,)), pltpu.SemaphoreType.DMA],
        ),
        out_shape=jax.ShapeDtypeStruct((n_rows * ROW_SUB, LANES), F32),
        compiler_params=pltpu.CompilerParams(
            dimension_semantics=("arbitrary",), vmem_limit_bytes=VMEM_LIMIT),
        name="dispatch",
    )(tcnt, tloc, tglb, pad_ends, xn, slots)


def _expert_kernel(be_ref, nvb_ref, run_ref, rexp_ref, nrun_ref, x_ref, wi_hbm, bi_ref, wo_hbm, bo_ref,
                   y_ref, wi32_ref, wo32_ref, wib_ref, wob_ref, wsem):
    i = pl.program_id(0)
    run = run_ref[i]
    buf = run % 2

    def fetch(r, b):
        e = rexp_ref[r]
        return (pltpu.make_async_copy(wi_hbm.at[e], wi32_ref.at[b], wsem.at[b]),
                pltpu.make_async_copy(wo_hbm.at[e], wo32_ref.at[b], wsem.at[b]))

    @pl.when(i == 0)
    def _():
        for copy in fetch(0, 0):
            copy.start()

    @pl.when((i == 0) | (run != run_ref[jnp.maximum(i - 1, 0)]))
    def _():
        for copy in fetch(run, buf):
            copy.wait()

        @pl.when(run + 1 < nrun_ref[0])
        def _():
            for copy in fetch(run + 1, 1 - buf):
                copy.start()

        wib_ref[...] = wi32_ref[buf].astype(BF16)
        wob_ref[...] = wo32_ref[buf].astype(BF16)

    @pl.when(i < nvb_ref[0])
    def _():
        x = jnp.concatenate(
            [x_ref[pl.ds(sub, TME, stride=ROW_SUB), :].astype(BF16) for sub in range(ROW_SUB)],
            axis=1)
        hdn = jnp.dot(x, wib_ref[...], preferred_element_type=F32) + bi_ref[0]
        gl = jnp.minimum(hdn[:, :D_EXPERT], SWIGLU_LIMIT)
        lin = jnp.clip(hdn[:, D_EXPERT:], -SWIGLU_LIMIT, SWIGLU_LIMIT)
        act = gl * jax.nn.sigmoid(SWIGLU_ALPHA * gl) * (lin + 1.0)
        y = jnp.dot(act.astype(BF16), wob_ref[...], preferred_element_type=F32) + bo_ref[0]
        for sub in range(ROW_SUB):
            y_ref[pl.ds(sub, TME, stride=ROW_SUB), :] = y[:, sub * LANES:(sub + 1) * LANES]

    @pl.when(i >= nvb_ref[0])
    def _():
        y_ref[...] = jnp.zeros_like(y_ref)


def _experts(block_e, nvb, block_run, run_expert, n_runs, x_pad, w_in, b_in, w_out, b_out):
    n_rows = x_pad.shape[0] // ROW_SUB
    d = D_MODEL
    nb = n_rows // TME
    rows = lambda i, be, nv, *_: (jnp.minimum(i, nv[0] - 1), 0)
    wsel = lambda i, be, *_: (be[i], 0, 0)
    return pl.pallas_call(
        _expert_kernel,
        grid_spec=pltpu.PrefetchScalarGridSpec(
            num_scalar_prefetch=5,
            grid=(nb,),
            in_specs=[
                pl.BlockSpec((TME * ROW_SUB, LANES), rows),
                pl.BlockSpec(memory_space=pl.ANY),
                pl.BlockSpec((1, 1, 2 * D_EXPERT), wsel),
                pl.BlockSpec(memory_space=pl.ANY),
                pl.BlockSpec((1, 1, d), wsel),
            ],
            out_specs=pl.BlockSpec((TME * ROW_SUB, LANES), lambda i, *_: (i, 0)),
            scratch_shapes=[pltpu.VMEM((2, d, 2 * D_EXPERT), F32), pltpu.VMEM((2, D_EXPERT, d), F32),
                            pltpu.VMEM((d, 2 * D_EXPERT), BF16), pltpu.VMEM((D_EXPERT, d), BF16),
                            pltpu.SemaphoreType.DMA((2,))],
        ),
        out_shape=jax.ShapeDtypeStruct((n_rows * ROW_SUB, LANES), F32),
        compiler_params=pltpu.CompilerParams(
            dimension_semantics=("arbitrary",), vmem_limit_bytes=VMEM_LIMIT),
        name="experts",
    )(block_e, nvb, block_run, run_expert, n_runs, x_pad, w_in, b_in, w_out, b_out)


def _combine_kernel(cnt_ref, loc_ref, glb_ref, ypad_ref, h_ref, slot_ref, gate_ref, o_ref,
                    sorted0_ref, sorted1_ref, sem):
    i = pl.program_id(0)
    n = pl.num_programs(0)
    sorted_refs = (sorted0_ref, sorted1_ref)

    def fetch(tile, b, enabled=None):
        _segment_copies(
            tile, cnt_ref, loc_ref, glb_ref,
            lambda loc, glb, size: pltpu.make_async_copy(
                _slab_at(ypad_ref, glb, size), _slab_at(sorted_refs[b], loc, size), sem.at[b]),
            enabled)

    pl.when(i == 0)(lambda: fetch(0, 0))

    def step(buf):
        pltpu.make_async_copy(_slab(ypad_ref, 0, N_SLOTS), sorted_refs[buf], sem.at[buf]).wait()
        fetch(jnp.minimum(i + 1, n - 1), 1 - buf, enabled=i + 1 < n)

        acc = h_ref[...]
        for c in range(N_SLOTS // PCHUNK):
            pos = lax.broadcasted_iota(jnp.int32, (TM, PCHUNK), 1) + c * PCHUNK
            weight = jnp.zeros((TM, PCHUNK), F32)
            for k in range(TOP_K):
                weight = weight + jnp.where(pos == slot_ref[:, k:k + 1], gate_ref[:, k:k + 1], 0.0)
            y = jnp.concatenate(
                [sorted_refs[buf][pl.ds(c * PCHUNK * ROW_SUB + sub, PCHUNK, stride=ROW_SUB), :].astype(BF16)
                 for sub in range(ROW_SUB)], axis=1)
            acc = acc + jnp.dot(weight.astype(BF16), y, preferred_element_type=F32)
        o_ref[...] = acc

    pl.when(i % 2 == 0)(lambda: step(0))
    pl.when(i % 2 == 1)(lambda: step(1))


def _combine(tcnt, tloc, tglb, y_pad, h, slots_tk, gates_tk):
    t, d = h.shape
    tile = lambda i, *_: (i, 0)
    return pl.pallas_call(
        _combine_kernel,
        grid_spec=pltpu.PrefetchScalarGridSpec(
            num_scalar_prefetch=3,
            grid=(t // TM,),
            in_specs=[
                pl.BlockSpec(memory_space=pl.ANY),
                pl.BlockSpec((TM, d), tile),
                pl.BlockSpec((TM, TOP_K), tile),
                pl.BlockSpec((TM, TOP_K), tile),
            ],
            out_specs=pl.BlockSpec((TM, d), tile),
            scratch_shapes=[pltpu.VMEM((N_SLOTS * ROW_SUB, LANES), F32),
                            pltpu.VMEM((N_SLOTS * ROW_SUB, LANES), F32),
                            pltpu.SemaphoreType.DMA((2,))],
        ),
        out_shape=jax.ShapeDtypeStruct((t, d), F32),
        compiler_params=pltpu.CompilerParams(
            dimension_semantics=("arbitrary",), vmem_limit_bytes=VMEM_LIMIT),
        name="combine",
    )(tcnt, tloc, tglb, y_pad, h, slots_tk, gates_tk)


def _rel_vector(rel_bias):
    n_heads = rel_bias.shape[0]
    far = LEFT - REL_CLIP
    falling = rel_bias[:, ::-1][:, 1:]
    n_fall = min(KWIN - 1 - far, 2 * REL_CLIP)
    vec = jnp.concatenate(
        [jnp.broadcast_to(rel_bias[:, -1:], (n_heads, far + 1)), falling[:, :n_fall],
         jnp.broadcast_to(rel_bias[:, :1], (n_heads, KWIN - 1 - far - n_fall)),
         jnp.broadcast_to(rel_bias[:, -1:], (n_heads, BIAS_LANES - KWIN))], axis=1)
    return vec.astype(F32) * LOG2_E


def kernel(x, norm_mix_g, w_in_proj, q_norm_g, k_norm_g, rel_bias, gmlp_v_norm_g, gmlp_w_s, gmlp_b_s,
           att_out_norm_g, gmlp_out_norm_g, w_out_proj, norm_moe_g, w_router, b_router,
           w_expert_in, b_expert_in, w_expert_out, b_expert_out):
    b, s, d = x.shape
    t = b * s

    head_of = jnp.arange(ATT_W) // HEAD_DIM
    block_diag = (head_of[:, None] == head_of[None, :]).astype(BF16)
    q, kt, v, gm = _inproj(
        x, norm_mix_g.reshape(1, d), w_in_proj.astype(BF16),
        jnp.tile(q_norm_g, N_HEADS).reshape(1, ATT_W), jnp.tile(k_norm_g, N_HEADS).reshape(1, ATT_W),
        block_diag, gmlp_v_norm_g, gmlp_w_s, gmlp_b_s.T, gmlp_out_norm_g.reshape(1, G_W))
    att = _attention(q, kt, v, _rel_vector(rel_bias), att_out_norm_g.reshape(1, ATT_W))

    wr_t = w_router.T
    wr_hi = wr_t.astype(BF16)
    wr_lo = (wr_t - wr_hi.astype(F32)).astype(BF16)
    h, xn, slots, gates, tile_cnt = _outproj(
        att.reshape(t, ATT_W), gm.reshape(t, G_W), x.reshape(t, d), w_out_proj.astype(BF16),
        norm_moe_g.reshape(1, d), wr_hi, wr_lo, b_router.reshape(N_EXPERTS, 1))

    tcnt = tile_cnt[:, :, 0]
    counts = jnp.sum(tcnt, axis=0)
    padded = ((counts + TME - 1) // TME) * TME
    pad_ends = jnp.cumsum(padded).astype(jnp.int32)
    pad_starts = pad_ends - padded
    tloc = jnp.cumsum(tcnt, axis=1) - tcnt
    tglb = pad_starts[None, :] + jnp.cumsum(tcnt, axis=0) - tcnt
    n_blocks = (t * TOP_K) // TME + N_EXPERTS
    nvb = pad_ends[-1] // TME
    block_start = jnp.arange(n_blocks, dtype=jnp.int32) * TME
    block_e = jnp.minimum(jnp.sum(pad_ends[None, :] <= block_start[:, None], axis=1), N_EXPERTS - 1)
    block_e = jnp.where(jnp.arange(n_blocks) < nvb, block_e, block_e[nvb - 1]).astype(jnp.int32)
    runs = tuple((a * ROW_SUB).reshape(-1).astype(jnp.int32) for a in (tcnt, tloc, tglb))
    present = padded > 0
    run_of_expert = jnp.cumsum(present) - 1
    experts_iota = jnp.arange(N_EXPERTS)
    block_run = jnp.sum(jnp.where(block_e[:, None] == experts_iota[None, :], run_of_expert[None, :], 0),
                        axis=1).astype(jnp.int32)
    run_expert = jnp.sum(jnp.where(present[None, :] & (run_of_expert[None, :] == experts_iota[:, None]),
                                   experts_iota[None, :], 0), axis=1).astype(jnp.int32)
    n_runs = jnp.sum(present).astype(jnp.int32).reshape(1)

    x_pad = _dispatch(*runs, pad_ends, xn, slots, n_blocks * TME)
    y_pad = _experts(block_e, nvb.reshape(1), block_run, run_expert, n_runs, x_pad, w_expert_in,
                     b_expert_in.reshape(N_EXPERTS, 1, 2 * D_EXPERT), w_expert_out,
                     b_expert_out.reshape(N_EXPERTS, 1, d))
    out = _combine(*runs, y_pad, h, slots.T, gates.T)
    return out.reshape(b, s, d)
```

```python
import jax
import jax.numpy as jnp
from jax import lax
from jax.experimental import pallas as pl
from jax.experimental.pallas import tpu as pltpu

D_MODEL = 1024
CHUNK = 64
LEFT = 8 * CHUNK
N_HEADS = 8
HEAD_DIM = 64
ATT_W = 512
REL_CLIP = 128
G_GROUPS = 4
G_DIM = 128
G_BLOCK = 128
G_W = 512
N_EXPERTS = 32
TOP_K = 4
D_EXPERT = 1024
SWIGLU_LIMIT = 7.0
SWIGLU_ALPHA = 1.702
EPS = 1e-6
NEG_INF = -1e30
LOG2_E = 1.4426950408889634

TM = 512
TQ = 256
KWIN = TQ + LEFT
BIAS_LANES = 1024
TME = 512
N_SLOTS = TM * TOP_K
PCHUNK = 256
SEG_BITS = TM.bit_length()
VMEM_LIMIT = 56 * 1024 * 1024
LANES = 128
ROW_SUB = D_MODEL // LANES

F32 = jnp.float32
BF16 = jnp.bfloat16


def _rms(x, gain):
    ms = jnp.mean(x * x, axis=-1, keepdims=True)
    return x * lax.rsqrt(ms + EPS) * gain


def _gelu(x):
    return 0.5 * x * (1.0 + lax.erf(x * (2.0 ** -0.5)))


def _slab(ref, row, n_rows=1):
    return ref.at[pl.ds(pl.multiple_of(row * ROW_SUB, ROW_SUB), n_rows * ROW_SUB)]


def _slab_at(ref, offset, n_rows):
    return ref.at[pl.ds(pl.multiple_of(offset, ROW_SUB), n_rows * ROW_SUB)]


def _segment_copies(tile, cnt_ref, loc_ref, glb_ref, make_copy, enabled=None):
    def per_expert(e, c):
        n = cnt_ref[tile * N_EXPERTS + e]
        if enabled is not None:
            n = jnp.where(enabled, n, 0)
        loc = loc_ref[tile * N_EXPERTS + e]
        glb = glb_ref[tile * N_EXPERTS + e]
        for bit in range(SEG_BITS):
            size = 1 << bit
            off = n & ~(2 * size * ROW_SUB - 1)

            @pl.when((n & (size * ROW_SUB)) != 0)
            def _():
                make_copy(loc + off, glb + off, size).start()
        return c

    if enabled is None:
        lax.fori_loop(0, N_EXPERTS, per_expert, 0, unroll=2)
    else:
        for e in range(N_EXPERTS):
            per_expert(e, 0)


def _inproj_kernel(x_ref, g_ref, w_ref, gq_ref, gk_ref, bd_ref, gvn_ref, ws_ref, bst_ref, gout_ref,
                   q_ref, kt_ref, v_ref, gm_ref):
    j = pl.program_id(1)

    @pl.when(j == 0)
    def _():
        kt_ref[...] = jnp.zeros_like(kt_ref)
        v_ref[...] = jnp.zeros_like(v_ref)

    @pl.when(j > 0)
    def _():
        z = _rms(x_ref[0], g_ref[...]).astype(BF16)

        def proj(c0, width):
            return jnp.dot(z, w_ref[:, c0:c0 + width], preferred_element_type=F32)

        def head_norm(t, gain):
            ssum = jnp.dot((t * t).astype(BF16), bd_ref[...], preferred_element_type=F32)
            return t * lax.rsqrt(ssum * (1.0 / HEAD_DIM) + EPS) * gain

        q = head_norm(proj(0, ATT_W), gq_ref[...]) * (HEAD_DIM ** -0.5 * LOG2_E)
        q_ref[0] = q.astype(BF16)
        k = head_norm(proj(ATT_W, ATT_W), gk_ref[...])
        kt_ref[0] = k.T.astype(BF16)
        v_ref[0] = proj(2 * ATT_W, ATT_W).astype(BF16)

        gu = _gelu(proj(3 * ATT_W, G_W))
        gv = _gelu(proj(3 * ATT_W + G_W, G_W))
        row = lax.broadcasted_iota(jnp.int32, (G_BLOCK, G_BLOCK), 0) // CHUNK
        col = lax.broadcasted_iota(jnp.int32, (G_BLOCK, G_BLOCK), 1) // CHUNK
        tri = row >= col
        cols = []
        for g in range(G_GROUPS):
            sl = slice(g * G_DIM, (g + 1) * G_DIM)
            vn = _rms(gv[:, sl], gvn_ref[g:g + 1, :]).astype(BF16)
            wm = jnp.where(tri, ws_ref[g], 0.0).astype(BF16)
            blocks = []
            for n in range(TM // G_BLOCK):
                rs = slice(n * G_BLOCK, (n + 1) * G_BLOCK)
                gate = jnp.dot(wm, vn[rs], preferred_element_type=F32) + bst_ref[:, g:g + 1]
                blocks.append(gu[rs, sl] * gate)
            cols.append(jnp.concatenate(blocks, axis=0))
        gm = jnp.concatenate(cols, axis=1)
        gm_ref[0] = _rms(gm, gout_ref[...]).astype(BF16)


def _inproj(x, g_mix, w_in, gq, gk, bd, gvn, ws, bst, gout):
    b, s, d = x.shape
    nt = s // TM
    xmap = lambda bi, j: (bi, jnp.maximum(j - 1, 0), 0)
    const2 = lambda bi, j: (0, 0)
    return pl.pallas_call(
        _inproj_kernel,
        grid=(b, nt + 1),
        in_specs=[
            pl.BlockSpec((1, TM, d), xmap),
            pl.BlockSpec((1, d), const2),
            pl.BlockSpec(w_in.shape, const2),
            pl.BlockSpec((1, ATT_W), const2),
            pl.BlockSpec((1, ATT_W), const2),
            pl.BlockSpec((ATT_W, ATT_W), const2),
            pl.BlockSpec((G_GROUPS, G_DIM), const2),
            pl.BlockSpec((G_GROUPS, G_BLOCK, G_BLOCK), lambda bi, j: (0, 0, 0)),
            pl.BlockSpec((G_BLOCK, G_GROUPS), const2),
            pl.BlockSpec((1, G_W), const2),
        ],
        out_specs=[
            pl.BlockSpec((1, TM, ATT_W), xmap),
            pl.BlockSpec((1, ATT_W, TM), lambda bi, j: (bi, 0, j)),
            pl.BlockSpec((1, TM, ATT_W), lambda bi, j: (bi, j, 0)),
            pl.BlockSpec((1, TM, G_W), xmap),
        ],
        out_shape=[
            jax.ShapeDtypeStruct((b, s, ATT_W), BF16),
            jax.ShapeDtypeStruct((b, ATT_W, s + LEFT), BF16),
            jax.ShapeDtypeStruct((b, s + LEFT, ATT_W), BF16),
            jax.ShapeDtypeStruct((b, s, G_W), BF16),
        ],
        compiler_params=pltpu.CompilerParams(
            dimension_semantics=("arbitrary", "arbitrary"), vmem_limit_bytes=VMEM_LIMIT),
        name="inproj",
    )(x, g_mix, w_in, gq, gk, bd, gvn, ws, bst, gout)


def _attn_kernel(q_ref, kt_ref, v_ref, relvec_ref, gout_ref, o_ref, bias_ref):
    i = pl.program_id(1)
    q0 = pl.multiple_of(i * TQ, TQ)
    win = pl.ds(q0, KWIN)
    lane = lax.broadcasted_iota(jnp.int32, (1, 256), 1)

    @pl.when((pl.program_id(0) == 0) & (i == 0))
    def _():
        r = lax.broadcasted_iota(jnp.int32, (TQ, KWIN), 0)
        j = lax.broadcasted_iota(jnp.int32, (TQ, KWIN), 1)
        lo = (r // CHUNK) * CHUNK
        in_band = (j >= lo) & (j < lo + LEFT + CHUNK)
        for h in range(N_HEADS):
            shifted = pltpu.roll(jnp.broadcast_to(relvec_ref[h:h + 1, :], (TQ, BIAS_LANES)), 0, 1,
                                 stride=1, stride_axis=0)
            bias_ref[h] = jnp.where(in_band, shifted[:, :KWIN], NEG_INF)

    def tile(has_left_padding):
        if has_left_padding:
            colpos = lax.broadcasted_iota(jnp.int32, (1, KWIN), 1)
            pad_bias = jnp.where(colpos >= LEFT - q0, 0.0, NEG_INF).astype(F32)
        outs = []
        for g in range(2):
            gs = slice(g * 256, (g + 1) * 256)
            qg = q_ref[0, :, gs]
            ktg = kt_ref[0, gs, win]
            vg = v_ref[0, win, gs]
            acc = jnp.zeros((TQ, 256), F32)
            for hh in range(4):
                h = 4 * g + hh
                hmask = (lane >= hh * HEAD_DIM) & (lane < (hh + 1) * HEAD_DIM)
                qh = jnp.where(hmask, qg, jnp.zeros_like(qg))
                s = jnp.dot(qh, ktg, preferred_element_type=F32) + bias_ref[h]
                if has_left_padding:
                    s = s + pad_bias
                m = jnp.max(s, axis=-1, keepdims=True)
                e = jnp.exp2(s - m)
                inv = 1.0 / jnp.sum(e, axis=-1, keepdims=True)
                pv = jnp.dot(e.astype(BF16), vg, preferred_element_type=F32)
                acc = jnp.where(hmask, pv * inv, acc)
            outs.append(acc)
        att = jnp.concatenate(outs, axis=1)
        o_ref[0] = _rms(att, gout_ref[...]).astype(BF16)

    pl.when(q0 < LEFT)(lambda: tile(True))
    pl.when(q0 >= LEFT)(lambda: tile(False))


def _attention(q, kt, v, relvec, gout):
    b, s, _ = q.shape
    return pl.pallas_call(
        _attn_kernel,
        grid=(b, s // TQ),
        in_specs=[
            pl.BlockSpec((1, TQ, ATT_W), lambda bi, i: (bi, i, 0)),
            pl.BlockSpec((1, ATT_W, s + LEFT), lambda bi, i: (bi, 0, 0)),
            pl.BlockSpec((1, s + LEFT, ATT_W), lambda bi, i: (bi, 0, 0)),
            pl.BlockSpec(relvec.shape, lambda bi, i: (0, 0)),
            pl.BlockSpec((1, ATT_W), lambda bi, i: (0, 0)),
        ],
        out_specs=pl.BlockSpec((1, TQ, ATT_W), lambda bi, i: (bi, i, 0)),
        out_shape=jax.ShapeDtypeStruct((b, s, ATT_W), BF16),
        scratch_shapes=[pltpu.VMEM((N_HEADS, TQ, KWIN), F32)],
        compiler_params=pltpu.CompilerParams(
            dimension_semantics=("arbitrary", "arbitrary"), vmem_limit_bytes=VMEM_LIMIT),
        name="attention",
    )(q, kt, v, relvec, gout)


def _outproj_kernel(a_ref, gm_ref, x_ref, wo_ref, gmoe_ref, wrh_ref, wrl_ref, br_ref,
                    h_ref, xn_ref, slot_ref, gate_ref, cnt_ref):
    h = (x_ref[...]
         + jnp.dot(a_ref[...], wo_ref[:ATT_W, :], preferred_element_type=F32)
         + jnp.dot(gm_ref[...], wo_ref[ATT_W:, :], preferred_element_type=F32))
    h_ref[...] = h
    xn = _rms(h, gmoe_ref[...])
    xhi = xn.astype(BF16)
    xlo = (xn - xhi.astype(F32)).astype(BF16)
    xn_ref[...] = xhi

    nt = (((1,), (1,)), ((), ()))
    logits = (lax.dot_general(wrh_ref[...], xhi, nt, preferred_element_type=F32)
              + lax.dot_general(wrh_ref[...], xlo, nt, preferred_element_type=F32)
              + lax.dot_general(wrl_ref[...], xhi, nt, preferred_element_type=F32)
              + br_ref[...])
    eidx = lax.broadcasted_iota(jnp.int32, (N_EXPERTS, TM), 0)
    vals, idxs = [], []
    cur = logits
    for _ in range(TOP_K):
        m = jnp.max(cur, axis=0, keepdims=True)
        ik = jnp.min(jnp.where(cur == m, eidx, N_EXPERTS), axis=0, keepdims=True)
        vals.append(m)
        idxs.append(ik)
        cur = jnp.where(eidx == ik, -jnp.inf, cur)
    exps = [jnp.exp(v - vals[0]) for v in vals]
    tot = exps[0] + exps[1] + exps[2] + exps[3]
    gate_ref[...] = jnp.concatenate([e / tot for e in exps], axis=0)

    onehots = [eidx == ik for ik in idxs]
    member = sum(oh.astype(F32) for oh in onehots).astype(BF16)
    earlier_tok = (lax.broadcasted_iota(jnp.int32, (TM, TM), 0)
                   < lax.broadcasted_iota(jnp.int32, (TM, TM), 1)).astype(BF16)
    earlier_exp = (lax.broadcasted_iota(jnp.int32, (N_EXPERTS, N_EXPERTS), 1)
                   < lax.broadcasted_iota(jnp.int32, (N_EXPERTS, N_EXPERTS), 0)).astype(BF16)
    rank = jnp.dot(member, earlier_tok, preferred_element_type=F32)
    lower = jnp.sum(jnp.dot(earlier_exp, member, preferred_element_type=F32), axis=1, keepdims=True)
    base = rank + lower
    slots = [jnp.sum(jnp.where(oh, base, 0.0), axis=0, keepdims=True) for oh in onehots]
    slot_ref[...] = jnp.concatenate(slots, axis=0).astype(jnp.int32)
    cnt_ref[0] = jnp.sum(member.astype(F32), axis=1, keepdims=True).astype(jnp.int32)


def _outproj(att, gm, x2, w_out, g_moe, wr_hi, wr_lo, br):
    t, d = x2.shape
    tile = lambda i: (i, 0)
    const = lambda i: (0, 0)
    lanes = lambda i: (0, i)
    return pl.pallas_call(
        _outproj_kernel,
        grid=(t // TM,),
        in_specs=[
            pl.BlockSpec((TM, ATT_W), tile),
            pl.BlockSpec((TM, G_W), tile),
            pl.BlockSpec((TM, d), tile),
            pl.BlockSpec(w_out.shape, const),
            pl.BlockSpec((1, d), const),
            pl.BlockSpec((N_EXPERTS, d), const),
            pl.BlockSpec((N_EXPERTS, d), const),
            pl.BlockSpec((N_EXPERTS, 1), const),
        ],
        out_specs=[
            pl.BlockSpec((TM, d), tile),
            pl.BlockSpec((TM, d), tile),
            pl.BlockSpec((TOP_K, TM), lanes),
            pl.BlockSpec((TOP_K, TM), lanes),
            pl.BlockSpec((1, N_EXPERTS, 1), lambda i: (i, 0, 0)),
        ],
        out_shape=[
            jax.ShapeDtypeStruct((t, d), F32),
            jax.ShapeDtypeStruct((t, d), BF16),
            jax.ShapeDtypeStruct((TOP_K, t), jnp.int32),
            jax.ShapeDtypeStruct((TOP_K, t), F32),
            jax.ShapeDtypeStruct((t // TM, N_EXPERTS, 1), jnp.int32),
        ],
        compiler_params=pltpu.CompilerParams(
            dimension_semantics=("arbitrary",), vmem_limit_bytes=VMEM_LIMIT),
        name="outproj_router",
    )(att, gm, x2, w_out, g_moe, wr_hi, wr_lo, br)


def _dispatch_kernel(cnt_ref, loc_ref, glb_ref, pend_ref, xn_ref, slot_ref, xpad_ref,
                     sorted0_ref, sorted1_ref, zero_ref, sem, zsem):
    i = pl.program_id(0)
    n = pl.num_programs(0)
    sorted_refs = (sorted0_ref, sorted1_ref)

    def drain(b):
        pltpu.make_async_copy(sorted_refs[b], _slab(xpad_ref, 0, N_SLOTS), sem.at[b]).wait()

    @pl.when(i == 0)
    def _():
        zero_ref[...] = jnp.zeros_like(zero_ref)

        def last_block(e):
            return pltpu.make_async_copy(zero_ref, _slab(xpad_ref, pend_ref[e] - TME, TME), zsem)

        def has_rows(e):
            return pend_ref[e] > jnp.where(e > 0, pend_ref[jnp.maximum(e - 1, 0)], 0)

        def start(e, c):
            pl.when(has_rows(e))(lambda: last_block(e).start())
            return c

        def wait(e, c):
            pl.when(has_rows(e))(lambda: last_block(e).wait())
            return c

        lax.fori_loop(0, N_EXPERTS, start, 0)
        lax.fori_loop(0, N_EXPERTS, wait, 0)

        def tail_block(blk):
            return pltpu.make_async_copy(zero_ref, _slab(xpad_ref, blk * TME, TME), zsem)

        first_tail = pend_ref[N_EXPERTS - 1] // TME
        n_blocks = xpad_ref.shape[0] // (TME * ROW_SUB)
        lax.fori_loop(first_tail, n_blocks, lambda blk, c: (tail_block(blk).start(), c)[1], 0)
        lax.fori_loop(first_tail, n_blocks, lambda blk, c: (tail_block(blk).wait(), c)[1], 0)

    def send(tile, b, enabled=None):
        _segment_copies(
            tile, cnt_ref, loc_ref, glb_ref,
            lambda loc, glb, size: pltpu.make_async_copy(
                _slab_at(sorted_refs[b], loc, size), _slab_at(xpad_ref, glb, size), sem.at[b]),
            enabled)

    def step(buf):
        pl.when(i >= 2)(lambda: drain(buf))

        xn = xn_ref[...]
        for c in range(N_SLOTS // PCHUNK):
            pos = lax.broadcasted_iota(jnp.int32, (PCHUNK, TM), 0) + c * PCHUNK
            hit = pos == slot_ref[0:1, :]
            for k in range(1, TOP_K):
                hit = hit | (pos == slot_ref[k:k + 1, :])
            rows = jnp.dot(hit.astype(F32).astype(BF16), xn, preferred_element_type=F32)
            for sub in range(ROW_SUB):
                sorted_refs[buf][pl.ds(c * PCHUNK * ROW_SUB + sub, PCHUNK, stride=ROW_SUB), :] = (
                    rows[:, sub * LANES:(sub + 1) * LANES])

        send(i, buf)

        @pl.when(i == n - 1)
        def _():
            pl.when(i >= 1)(lambda: drain(1 - buf))
            drain(buf)

    pl.when(i % 2 == 0)(lambda: step(0))
    pl.when(i % 2 == 1)(lambda: step(1))


def _dispatch(tcnt, tloc, tglb, pad_ends, xn, slots, n_rows):
    t, d = xn.shape
    return pl.pallas_call(
        _dispatch_kernel,
        grid_spec=pltpu.PrefetchScalarGridSpec(
            num_scalar_prefetch=4,
            grid=(t // TM,),
            in_specs=[pl.BlockSpec((TM, d), lambda i, *_: (i, 0)),
                      pl.BlockSpec((TOP_K, TM), lambda i, *_: (0, i))],
            out_specs=pl.BlockSpec(memory_space=pl.ANY),
            scratch_shapes=[pltpu.VMEM((N_SLOTS * ROW_SUB, LANES), F32),
                            pltpu.VMEM((N_SLOTS * ROW_SUB, LANES), F32),
                            pltpu.VMEM((TME * ROW_SUB, LANES), F32),
                            pltpu.SemaphoreType.DMA((2,)), pltpu.SemaphoreType.DMA],
        ),
        out_shape=jax.ShapeDtypeStruct((n_rows * ROW_SUB, LANES), F32),
        compiler_params=pltpu.CompilerParams(
            dimension_semantics=("arbitrary",), vmem_limit_bytes=VMEM_LIMIT),
        name="dispatch",
    )(tcnt, tloc, tglb, pad_ends, xn, slots)


def _expert_kernel(be_ref, nvb_ref, run_ref, rexp_ref, nrun_ref, x_ref, wi_hbm, bi_ref, wo_hbm, bo_ref,
                   y_ref, wi32_ref, wo32_ref, wib_ref, wob_ref, wsem):
    i = pl.program_id(0)
    run = run_ref[i]
    buf = run % 2

    def fetch(r, b):
        e = rexp_ref[r]
        return (pltpu.make_async_copy(wi_hbm.at[e], wi32_ref.at[b], wsem.at[b]),
                pltpu.make_async_copy(wo_hbm.at[e], wo32_ref.at[b], wsem.at[b]))

    @pl.when(i == 0)
    def _():
        for copy in fetch(0, 0):
            copy.start()

    @pl.when((i == 0) | (run != run_ref[jnp.maximum(i - 1, 0)]))
    def _():
        for copy in fetch(run, buf):
            copy.wait()

        @pl.when(run + 1 < nrun_ref[0])
        def _():
            for copy in fetch(run + 1, 1 - buf):
                copy.start()

        wib_ref[...] = wi32_ref[buf].astype(BF16)
        wob_ref[...] = wo32_ref[buf].astype(BF16)

    @pl.when(i < nvb_ref[0])
    def _():
        x = jnp.concatenate(
            [x_ref[pl.ds(sub, TME, stride=ROW_SUB), :].astype(BF16) for sub in range(ROW_SUB)],
            axis=1)
        hdn = jnp.dot(x, wib_ref[...], preferred_element_type=F32) + bi_ref[0]
        gl = jnp.minimum(hdn[:, :D_EXPERT], SWIGLU_LIMIT)
        lin = jnp.clip(hdn[:, D_EXPERT:], -SWIGLU_LIMIT, SWIGLU_LIMIT)
        act = gl * jax.nn.sigmoid(SWIGLU_ALPHA * gl) * (lin + 1.0)
        y = jnp.dot(act.astype(BF16), wob_ref[...], preferred_element_type=F32) + bo_ref[0]
        for sub in range(ROW_SUB):
            y_ref[pl.ds(sub, TME, stride=ROW_SUB), :] = y[:, sub * LANES:(sub + 1) * LANES]

    @pl.when(i >= nvb_ref[0])
    def _():
        y_ref[...] = jnp.zeros_like(y_ref)


def _experts(block_e, nvb, block_run, run_expert, n_runs, x_pad, w_in, b_in, w_out, b_out):
    n_rows = x_pad.shape[0] // ROW_SUB
    d = D_MODEL
    nb = n_rows // TME
    rows = lambda i, be, nv, *_: (jnp.minimum(i, nv[0] - 1), 0)
    wsel = lambda i, be, *_: (be[i], 0, 0)
    return pl.pallas_call(
        _expert_kernel,
        grid_spec=pltpu.PrefetchScalarGridSpec(
            num_scalar_prefetch=5,
            grid=(nb,),
            in_specs=[
                pl.BlockSpec((TME * ROW_SUB, LANES), rows),
                pl.BlockSpec(memory_space=pl.ANY),
                pl.BlockSpec((1, 1, 2 * D_EXPERT), wsel),
                pl.BlockSpec(memory_space=pl.ANY),
                pl.BlockSpec((1, 1, d), wsel),
            ],
            out_specs=pl.BlockSpec((TME * ROW_SUB, LANES), lambda i, *_: (i, 0)),
            scratch_shapes=[pltpu.VMEM((2, d, 2 * D_EXPERT), F32), pltpu.VMEM((2, D_EXPERT, d), F32),
                            pltpu.VMEM((d, 2 * D_EXPERT), BF16), pltpu.VMEM((D_EXPERT, d), BF16),
                            pltpu.SemaphoreType.DMA((2,))],
        ),
        out_shape=jax.ShapeDtypeStruct((n_rows * ROW_SUB, LANES), F32),
        compiler_params=pltpu.CompilerParams(
            dimension_semantics=("arbitrary",), vmem_limit_bytes=VMEM_LIMIT),
        name="experts",
    )(block_e, nvb, block_run, run_expert, n_runs, x_pad, w_in, b_in, w_out, b_out)


def _combine_kernel(cnt_ref, loc_ref, glb_ref, ypad_ref, h_ref, slot_ref, gate_ref, o_ref,
                    sorted0_ref, sorted1_ref, sem):
    i = pl.program_id(0)
    n = pl.num_programs(0)
    sorted_refs = (sorted0_ref, sorted1_ref)

    def fetch(tile, b, enabled=None):
        _segment_copies(
            tile, cnt_ref, loc_ref, glb_ref,
            lambda loc, glb, size: pltpu.make_async_copy(
                _slab_at(ypad_ref, glb, size), _slab_at(sorted_refs[b], loc, size), sem.at[b]),
            enabled)

    pl.when(i == 0)(lambda: fetch(0, 0))

    def step(buf):
        pltpu.make_async_copy(_slab(ypad_ref, 0, N_SLOTS), sorted_refs[buf], sem.at[buf]).wait()
        fetch(jnp.minimum(i + 1, n - 1), 1 - buf, enabled=i + 1 < n)

        acc = h_ref[...]
        for c in range(N_SLOTS // PCHUNK):
            pos = lax.broadcasted_iota(jnp.int32, (TM, PCHUNK), 1) + c * PCHUNK
            weight = jnp.zeros((TM, PCHUNK), F32)
            for k in range(TOP_K):
                weight = weight + jnp.where(pos == slot_ref[:, k:k + 1], gate_ref[:, k:k + 1], 0.0)
            y = jnp.concatenate(
                [sorted_refs[buf][pl.ds(c * PCHUNK * ROW_SUB + sub, PCHUNK, stride=ROW_SUB), :].astype(BF16)
                 for sub in range(ROW_SUB)], axis=1)
            acc = acc + jnp.dot(weight.astype(BF16), y, preferred_element_type=F32)
        o_ref[...] = acc

    pl.when(i % 2 == 0)(lambda: step(0))
    pl.when(i % 2 == 1)(lambda: step(1))


def _combine(tcnt, tloc, tglb, y_pad, h, slots_tk, gates_tk):
    t, d = h.shape
    tile = lambda i, *_: (i, 0)
    return pl.pallas_call(
        _combine_kernel,
        grid_spec=pltpu.PrefetchScalarGridSpec(
            num_scalar_prefetch=3,
            grid=(t // TM,),
            in_specs=[
                pl.BlockSpec(memory_space=pl.ANY),
                pl.BlockSpec((TM, d), tile),
                pl.BlockSpec((TM, TOP_K), tile),
                pl.BlockSpec((TM, TOP_K), tile),
            ],
            out_specs=pl.BlockSpec((TM, d), tile),
            scratch_shapes=[pltpu.VMEM((N_SLOTS * ROW_SUB, LANES), F32),
                            pltpu.VMEM((N_SLOTS * ROW_SUB, LANES), F32),
                            pltpu.SemaphoreType.DMA((2,))],
        ),
        out_shape=jax.ShapeDtypeStruct((t, d), F32),
        compiler_params=pltpu.CompilerParams(
            dimension_semantics=("arbitrary",), vmem_limit_bytes=VMEM_LIMIT),
        name="combine",
    )(tcnt, tloc, tglb, y_pad, h, slots_tk, gates_tk)


def _rel_vector(rel_bias):
    n_heads = rel_bias.shape[0]
    far = LEFT - REL_CLIP
    falling = rel_bias[:, ::-1][:, 1:]
    n_fall = min(KWIN - 1 - far, 2 * REL_CLIP)
    vec = jnp.concatenate(
        [jnp.broadcast_to(rel_bias[:, -1:], (n_heads, far + 1)), falling[:, :n_fall],
         jnp.broadcast_to(rel_bias[:, :1], (n_heads, KWIN - 1 - far - n_fall)),
         jnp.broadcast_to(rel_bias[:, -1:], (n_heads, BIAS_LANES - KWIN))], axis=1)
    return vec.astype(F32) * LOG2_E


def kernel(x, norm_mix_g, w_in_proj, q_norm_g, k_norm_g, rel_bias, gmlp_v_norm_g, gmlp_w_s, gmlp_b_s,
           att_out_norm_g, gmlp_out_norm_g, w_out_proj, norm_moe_g, w_router, b_router,
           w_expert_in, b_expert_in, w_expert_out, b_expert_out):
    b, s, d = x.shape
    t = b * s

    head_of = jnp.arange(ATT_W) // HEAD_DIM
    block_diag = (head_of[:, None] == head_of[None, :]).astype(BF16)
    q, kt, v, gm = _inproj(
        x, norm_mix_g.reshape(1, d), w_in_proj.astype(BF16),
        jnp.tile(q_norm_g, N_HEADS).reshape(1, ATT_W), jnp.tile(k_norm_g, N_HEADS).reshape(1, ATT_W),
        block_diag, gmlp_v_norm_g, gmlp_w_s, gmlp_b_s.T, gmlp_out_norm_g.reshape(1, G_W))
    att = _attention(q, kt, v, _rel_vector(rel_bias), att_out_norm_g.reshape(1, ATT_W))

    wr_t = w_router.T
    wr_hi = wr_t.astype(BF16)
    wr_lo = (wr_t - wr_hi.astype(F32)).astype(BF16)
    h, xn, slots, gates, tile_cnt = _outproj(
        att.reshape(t, ATT_W), gm.reshape(t, G_W), x.reshape(t, d), w_out_proj.astype(BF16),
        norm_moe_g.reshape(1, d), wr_hi, wr_lo, b_router.reshape(N_EXPERTS, 1))

    tcnt = tile_cnt[:, :, 0]
    counts = jnp.sum(tcnt, axis=0)
    padded = ((counts + TME - 1) // TME) * TME
    pad_ends = jnp.cumsum(padded).astype(jnp.int32)
    pad_starts = pad_ends - padded
    tloc = jnp.cumsum(tcnt, axis=1) - tcnt
    tglb = pad_starts[None, :] + jnp.cumsum(tcnt, axis=0) - tcnt
    n_blocks = (t * TOP_K) // TME + N_EXPERTS
    nvb = pad_ends[-1] // TME
    block_start = jnp.arange(n_blocks, dtype=jnp.int32) * TME
    block_e = jnp.minimum(jnp.sum(pad_ends[None, :] <= block_start[:, None], axis=1), N_EXPERTS - 1)
    block_e = jnp.where(jnp.arange(n_blocks) < nvb, block_e, block_e[nvb - 1]).astype(jnp.int32)
    runs = tuple((a * ROW_SUB).reshape(-1).astype(jnp.int32) for a in (tcnt, tloc, tglb))
    present = padded > 0
    run_of_expert = jnp.cumsum(present) - 1
    experts_iota = jnp.arange(N_EXPERTS)
    block_run = jnp.sum(jnp.where(block_e[:, None] == experts_iota[None, :], run_of_expert[None, :], 0),
                        axis=1).astype(jnp.int32)
    run_expert = jnp.sum(jnp.where(present[None, :] & (run_of_expert[None, :] == experts_iota[:, None]),
                                   experts_iota[None, :], 0), axis=1).astype(jnp.int32)
    n_runs = jnp.sum(present).astype(jnp.int32).reshape(1)

    x_pad = _dispatch(*runs, pad_ends, xn, slots, n_blocks * TME)
    y_pad = _experts(block_e, nvb.reshape(1), block_run, run_expert, n_runs, x_pad, w_expert_in,
                     b_expert_in.reshape(N_EXPERTS, 1, 2 * D_EXPERT), w_expert_out,
                     b_expert_out.reshape(N_EXPERTS, 1, d))
    out = _combine(*runs, y_pad, h, slots.T, gates.T)
    return out.reshape(b, s, d)
```

```python
import jax
import jax.numpy as jnp
from jax import lax
from jax.experimental import pallas as pl
from jax.experimental.pallas import tpu as pltpu

D_MODEL = 1024
CHUNK = 64
LEFT = 8 * CHUNK
N_HEADS = 8
HEAD_DIM = 64
ATT_W = 512
REL_CLIP = 128
G_GROUPS = 4
G_DIM = 128
G_BLOCK = 128
G_W = 512
N_EXPERTS = 32
TOP_K = 4
D_EXPERT = 1024
SWIGLU_LIMIT = 7.0
SWIGLU_ALPHA = 1.702
EPS = 1e-6
NEG_INF = -1e30
LOG2_E = 1.4426950408889634

TM = 512
TQ = 256
KWIN = TQ + LEFT
BIAS_LANES = 1024
TME = 512
N_SLOTS = TM * TOP_K
PCHUNK = 256
SEG_BITS = TM.bit_length()
VMEM_LIMIT = 56 * 1024 * 1024
LANES = 128
MXU_DIM = 256
ROW_SUB = D_MODEL // LANES

F32 = jnp.float32
BF16 = jnp.bfloat16


def _rms(x, gain):
    ms = jnp.mean(x * x, axis=-1, keepdims=True)
    return x * lax.rsqrt(ms + EPS) * gain


def _gelu(x):
    return 0.5 * x * (1.0 + lax.erf(x * (2.0 ** -0.5)))


def _slab(ref, row, n_rows=1):
    return ref.at[pl.ds(pl.multiple_of(row * ROW_SUB, ROW_SUB), n_rows * ROW_SUB)]


def _slab_at(ref, offset, n_rows):
    return ref.at[pl.ds(pl.multiple_of(offset, ROW_SUB), n_rows * ROW_SUB)]


def _segment_copies(tile, cnt_ref, loc_ref, glb_ref, make_copy, enabled=None):
    def per_expert(e, c):
        n = cnt_ref[tile * N_EXPERTS + e]
        if enabled is not None:
            n = jnp.where(enabled, n, 0)
        loc = loc_ref[tile * N_EXPERTS + e]
        glb = glb_ref[tile * N_EXPERTS + e]
        for bit in range(SEG_BITS):
            size = 1 << bit
            off = n & ~(2 * size * ROW_SUB - 1)

            @pl.when((n & (size * ROW_SUB)) != 0)
            def _():
                make_copy(loc + off, glb + off, size).start()
        return c

    if enabled is None:
        lax.fori_loop(0, N_EXPERTS, per_expert, 0, unroll=2)
    else:
        for e in range(N_EXPERTS):
            per_expert(e, 0)


def _inproj_kernel(x_ref, g_ref, w_ref, gq_ref, gk_ref, bd_ref, gvn_ref, ws_ref, bst_ref, gout_ref,
                   q_ref, kt_ref, v_ref, gm_ref):
    j = pl.program_id(1)

    @pl.when(j == 0)
    def _():
        kt_ref[...] = jnp.zeros_like(kt_ref)
        v_ref[...] = jnp.zeros_like(v_ref)

    @pl.when(j > 0)
    def _():
        z = _rms(x_ref[0], g_ref[...]).astype(BF16)

        def proj(c0, width):
            return jnp.dot(z, w_ref[:, c0:c0 + width], preferred_element_type=F32)

        def head_norm(t, gain):
            sq = (t * t).astype(BF16)
            half = bd_ref.shape[0]
            ssum = jnp.concatenate(
                [jnp.dot(sq[:, c:c + half], bd_ref[...], preferred_element_type=F32)
                 for c in range(0, ATT_W, half)], axis=1)
            return t * lax.rsqrt(ssum * (1.0 / HEAD_DIM) + EPS) * gain

        q = head_norm(proj(0, ATT_W), gq_ref[...]) * (HEAD_DIM ** -0.5 * LOG2_E)
        q_ref[0] = q.astype(BF16)
        k = head_norm(proj(ATT_W, ATT_W), gk_ref[...])
        kt_ref[0] = k.T.astype(BF16)
        v_ref[0] = proj(2 * ATT_W, ATT_W).astype(BF16)

        gu = _gelu(proj(3 * ATT_W, G_W))
        gv = _gelu(proj(3 * ATT_W + G_W, G_W))
        row = lax.broadcasted_iota(jnp.int32, (G_BLOCK, G_BLOCK), 0) // CHUNK
        col = lax.broadcasted_iota(jnp.int32, (G_BLOCK, G_BLOCK), 1) // CHUNK
        tri = row >= col
        cols = []
        for g in range(G_GROUPS):
            sl = slice(g * G_DIM, (g + 1) * G_DIM)
            vn = _rms(gv[:, sl], gvn_ref[g:g + 1, :]).astype(BF16)
            wm = jnp.where(tri, ws_ref[g], 0.0).astype(BF16)
            blocks = []
            for n in range(TM // G_BLOCK):
                rs = slice(n * G_BLOCK, (n + 1) * G_BLOCK)
                gate = jnp.dot(wm, vn[rs], preferred_element_type=F32) + bst_ref[:, g:g + 1]
                blocks.append(gu[rs, sl] * gate)
            cols.append(jnp.concatenate(blocks, axis=0))
        gm = jnp.concatenate(cols, axis=1)
        gm_ref[0] = _rms(gm, gout_ref[...]).astype(BF16)


def _inproj(x, g_mix, w_in, gq, gk, bd, gvn, ws, bst, gout):
    b, s, d = x.shape
    nt = s // TM
    xmap = lambda bi, j: (bi, jnp.maximum(j - 1, 0), 0)
    const2 = lambda bi, j: (0, 0)
    return pl.pallas_call(
        _inproj_kernel,
        grid=(b, nt + 1),
        in_specs=[
            pl.BlockSpec((1, TM, d), xmap),
            pl.BlockSpec((1, d), const2),
            pl.BlockSpec(w_in.shape, const2),
            pl.BlockSpec((1, ATT_W), const2),
            pl.BlockSpec((1, ATT_W), const2),
            pl.BlockSpec(bd.shape, const2),
            pl.BlockSpec((G_GROUPS, G_DIM), const2),
            pl.BlockSpec((G_GROUPS, G_BLOCK, G_BLOCK), lambda bi, j: (0, 0, 0)),
            pl.BlockSpec((G_BLOCK, G_GROUPS), const2),
            pl.BlockSpec((1, G_W), const2),
        ],
        out_specs=[
            pl.BlockSpec((1, TM, ATT_W), xmap),
            pl.BlockSpec((1, ATT_W, TM), lambda bi, j: (bi, 0, j)),
            pl.BlockSpec((1, TM, ATT_W), lambda bi, j: (bi, j, 0)),
            pl.BlockSpec((1, TM, G_W), xmap),
        ],
        out_shape=[
            jax.ShapeDtypeStruct((b, s, ATT_W), BF16),
            jax.ShapeDtypeStruct((b, ATT_W, s + LEFT), BF16),
            jax.ShapeDtypeStruct((b, s + LEFT, ATT_W), BF16),
            jax.ShapeDtypeStruct((b, s, G_W), BF16),
        ],
        compiler_params=pltpu.CompilerParams(
            dimension_semantics=("arbitrary", "arbitrary"), vmem_limit_bytes=VMEM_LIMIT),
        name="inproj",
    )(x, g_mix, w_in, gq, gk, bd, gvn, ws, bst, gout)


def _attn_kernel(q_ref, kt_ref, v_ref, relvec_ref, gout_ref, o_ref, bias_ref):
    i = pl.program_id(1)
    q0 = pl.multiple_of(i * TQ, TQ)
    win = pl.ds(q0, KWIN)
    lane = lax.broadcasted_iota(jnp.int32, (1, 256), 1)

    @pl.when((pl.program_id(0) == 0) & (i == 0))
    def _():
        r = lax.broadcasted_iota(jnp.int32, (TQ, KWIN), 0)
        j = lax.broadcasted_iota(jnp.int32, (TQ, KWIN), 1)
        lo = (r // CHUNK) * CHUNK
        in_band = (j >= lo) & (j < lo + LEFT + CHUNK)
        for h in range(N_HEADS):
            shifted = pltpu.roll(jnp.broadcast_to(relvec_ref[h:h + 1, :], (TQ, BIAS_LANES)), 0, 1,
                                 stride=1, stride_axis=0)
            bias_ref[h] = jnp.where(in_band, shifted[:, :KWIN], NEG_INF)

    def tile(has_left_padding):
        if has_left_padding:
            colpos = lax.broadcasted_iota(jnp.int32, (1, KWIN), 1)
            pad_bias = jnp.where(colpos >= LEFT - q0, 0.0, NEG_INF).astype(F32)
        outs = []
        for g in range(2):
            gs = slice(g * 256, (g + 1) * 256)
            qg = q_ref[0, :, gs]
            ktg = kt_ref[0, gs, win]
            vg = v_ref[0, win, gs]
            acc = jnp.zeros((TQ, 256), F32)
            for hh in range(4):
                h = 4 * g + hh
                hmask = (lane >= hh * HEAD_DIM) & (lane < (hh + 1) * HEAD_DIM)
                qh = jnp.where(hmask, qg, jnp.zeros_like(qg))
                s = jnp.dot(qh, ktg, preferred_element_type=F32) + bias_ref[h]
                if has_left_padding:
                    s = s + pad_bias
                m = jnp.max(s, axis=-1, keepdims=True)
                e = jnp.exp2(s - m)
                inv = 1.0 / jnp.sum(e, axis=-1, keepdims=True)
                pv = jnp.dot(e.astype(BF16), vg, preferred_element_type=F32)
                acc = jnp.where(hmask, pv * inv, acc)
            outs.append(acc)
        att = jnp.concatenate(outs, axis=1)
        o_ref[0] = _rms(att, gout_ref[...]).astype(BF16)

    pl.when(q0 < LEFT)(lambda: tile(True))
    pl.when(q0 >= LEFT)(lambda: tile(False))


def _attention(q, kt, v, relvec, gout):
    b, s, _ = q.shape
    return pl.pallas_call(
        _attn_kernel,
        grid=(b, s // TQ),
        in_specs=[
            pl.BlockSpec((1, TQ, ATT_W), lambda bi, i: (bi, i, 0)),
            pl.BlockSpec((1, ATT_W, s + LEFT), lambda bi, i: (bi, 0, 0)),
            pl.BlockSpec((1, s + LEFT, ATT_W), lambda bi, i: (bi, 0, 0)),
            pl.BlockSpec(relvec.shape, lambda bi, i: (0, 0)),
            pl.BlockSpec((1, ATT_W), lambda bi, i: (0, 0)),
        ],
        out_specs=pl.BlockSpec((1, TQ, ATT_W), lambda bi, i: (bi, i, 0)),
        out_shape=jax.ShapeDtypeStruct((b, s, ATT_W), BF16),
        scratch_shapes=[pltpu.VMEM((N_HEADS, TQ, KWIN), F32)],
        compiler_params=pltpu.CompilerParams(
            dimension_semantics=("arbitrary", "arbitrary"), vmem_limit_bytes=VMEM_LIMIT),
        name="attention",
    )(q, kt, v, relvec, gout)


def _outproj_kernel(a_ref, gm_ref, x_ref, wo_ref, gmoe_ref, wr_ref, br_ref,
                    h_ref, xn_ref, slot_ref, gate_ref, cnt_ref):
    h = (x_ref[...]
         + jnp.dot(a_ref[...], wo_ref[:ATT_W, :], preferred_element_type=F32)
         + jnp.dot(gm_ref[...], wo_ref[ATT_W:, :], preferred_element_type=F32))
    h_ref[...] = h
    xn = _rms(h, gmoe_ref[...])
    xhi = xn.astype(BF16)
    xlo = (xn - xhi.astype(F32)).astype(BF16)
    xn_ref[...] = xhi

    nt = (((1,), (1,)), ((), ()))
    by_xhi = lax.dot_general(wr_ref[...], xhi, nt, preferred_element_type=F32)
    by_xlo = lax.dot_general(wr_ref[:N_EXPERTS, :], xlo, nt, preferred_element_type=F32)
    logits = by_xhi[:N_EXPERTS] + by_xhi[N_EXPERTS:] + by_xlo + br_ref[...]
    eidx = lax.broadcasted_iota(jnp.int32, (N_EXPERTS, TM), 0)
    vals, idxs = [], []
    cur = logits
    for _ in range(TOP_K):
        m = jnp.max(cur, axis=0, keepdims=True)
        ik = jnp.min(jnp.where(cur == m, eidx, N_EXPERTS), axis=0, keepdims=True)
        vals.append(m)
        idxs.append(ik)
        cur = jnp.where(eidx == ik, -jnp.inf, cur)
    exps = [jnp.exp(v - vals[0]) for v in vals]
    tot = exps[0] + exps[1] + exps[2] + exps[3]
    gate_ref[...] = jnp.concatenate([e / tot for e in exps], axis=0)

    onehots = [eidx == ik for ik in idxs]
    member = sum(oh.astype(F32) for oh in onehots).astype(BF16)
    earlier_tok = (lax.broadcasted_iota(jnp.int32, (TM, TM), 0)
                   < lax.broadcasted_iota(jnp.int32, (TM, TM), 1)).astype(BF16)
    earlier_exp = (lax.broadcasted_iota(jnp.int32, (N_EXPERTS, N_EXPERTS), 1)
                   < lax.broadcasted_iota(jnp.int32, (N_EXPERTS, N_EXPERTS), 0)).astype(BF16)
    rank = jnp.dot(member, earlier_tok, preferred_element_type=F32)
    lower = jnp.sum(jnp.dot(earlier_exp, member, preferred_element_type=F32), axis=1, keepdims=True)
    base = rank + lower
    slots = [jnp.sum(jnp.where(oh, base, 0.0), axis=0, keepdims=True) for oh in onehots]
    slot_ref[...] = jnp.concatenate(slots, axis=0).astype(jnp.int32)
    cnt_ref[0] = jnp.sum(member.astype(F32), axis=1, keepdims=True).astype(jnp.int32)


def _outproj(att, gm, x2, w_out, g_moe, wr_parts, br):
    t, d = x2.shape
    tile = lambda i: (i, 0)
    const = lambda i: (0, 0)
    lanes = lambda i: (0, i)
    return pl.pallas_call(
        _outproj_kernel,
        grid=(t // TM,),
        in_specs=[
            pl.BlockSpec((TM, ATT_W), tile),
            pl.BlockSpec((TM, G_W), tile),
            pl.BlockSpec((TM, d), tile),
            pl.BlockSpec(w_out.shape, const),
            pl.BlockSpec((1, d), const),
            pl.BlockSpec((2 * N_EXPERTS, d), const),
            pl.BlockSpec((N_EXPERTS, 1), const),
        ],
        out_specs=[
            pl.BlockSpec((TM, d), tile),
            pl.BlockSpec((TM, d), tile),
            pl.BlockSpec((TOP_K, TM), lanes),
            pl.BlockSpec((TOP_K, TM), lanes),
            pl.BlockSpec((1, N_EXPERTS, 1), lambda i: (i, 0, 0)),
        ],
        out_shape=[
            jax.ShapeDtypeStruct((t, d), F32),
            jax.ShapeDtypeStruct((t, d), BF16),
            jax.ShapeDtypeStruct((TOP_K, t), jnp.int32),
            jax.ShapeDtypeStruct((TOP_K, t), F32),
            jax.ShapeDtypeStruct((t // TM, N_EXPERTS, 1), jnp.int32),
        ],
        compiler_params=pltpu.CompilerParams(
            dimension_semantics=("arbitrary",), vmem_limit_bytes=VMEM_LIMIT),
        name="outproj_router",
    )(att, gm, x2, w_out, g_moe, wr_parts, br)


def _dispatch_kernel(cnt_ref, loc_ref, glb_ref, pend_ref, padat_ref, padn_ref, xn_ref, slot_ref, xpad_ref,
                     sorted0_ref, sorted1_ref, zero_ref, sem, zsem):
    i = pl.program_id(0)
    n = pl.num_programs(0)
    sorted_refs = (sorted0_ref, sorted1_ref)

    def drain(b):
        pltpu.make_async_copy(sorted_refs[b], _slab(xpad_ref, 0, N_SLOTS), sem.at[b]).wait()

    @pl.when(i == 0)
    def _():
        zero_ref[...] = jnp.zeros_like(zero_ref)

        def padding(wait):
            def per_expert(e, c):
                n = padn_ref[e]
                for bit in range(TME.bit_length() - 1):
                    size = 1 << bit
                    off = n & ~(2 * size * ROW_SUB - 1)

                    @pl.when((n & (size * ROW_SUB)) != 0)
                    def _():
                        copy = pltpu.make_async_copy(
                            _slab(zero_ref, 0, size), _slab_at(xpad_ref, padat_ref[e] + off, size), zsem)
                        copy.wait() if wait else copy.start()
                return c

            lax.fori_loop(0, N_EXPERTS, per_expert, 0)

        padding(wait=False)
        padding(wait=True)

        def tail_block(blk):
            return pltpu.make_async_copy(zero_ref, _slab(xpad_ref, blk * TME, TME), zsem)

        first_tail = pend_ref[N_EXPERTS - 1] // TME
        n_blocks = xpad_ref.shape[0] // (TME * ROW_SUB)
        lax.fori_loop(first_tail, n_blocks, lambda blk, c: (tail_block(blk).start(), c)[1], 0)
        lax.fori_loop(first_tail, n_blocks, lambda blk, c: (tail_block(blk).wait(), c)[1], 0)

    def send(tile, b, enabled=None):
        _segment_copies(
            tile, cnt_ref, loc_ref, glb_ref,
            lambda loc, glb, size: pltpu.make_async_copy(
                _slab_at(sorted_refs[b], loc, size), _slab_at(xpad_ref, glb, size), sem.at[b]),
            enabled)

    def step(buf):
        pl.when(i >= 2)(lambda: drain(buf))

        xn = xn_ref[...]
        for c in range(N_SLOTS // PCHUNK):
            pos = lax.broadcasted_iota(jnp.int32, (PCHUNK, TM), 0) + c * PCHUNK
            hit = pos == slot_ref[0:1, :]
            for k in range(1, TOP_K):
                hit = hit | (pos == slot_ref[k:k + 1, :])
            rows = jnp.dot(hit.astype(F32).astype(BF16), xn, preferred_element_type=F32)
            for sub in range(ROW_SUB):
                sorted_refs[buf][pl.ds(c * PCHUNK * ROW_SUB + sub, PCHUNK, stride=ROW_SUB), :] = (
                    rows[:, sub * LANES:(sub + 1) * LANES])

        send(i, buf)

        @pl.when(i == n - 1)
        def _():
            pl.when(i >= 1)(lambda: drain(1 - buf))
            drain(buf)

    pl.when(i % 2 == 0)(lambda: step(0))
    pl.when(i % 2 == 1)(lambda: step(1))


def _dispatch(tcnt, tloc, tglb, pad_ends, pad_at, pad_n, xn, slots, n_rows):
    t, d = xn.shape
    return pl.pallas_call(
        _dispatch_kernel,
        grid_spec=pltpu.PrefetchScalarGridSpec(
            num_scalar_prefetch=6,
            grid=(t // TM,),
            in_specs=[pl.BlockSpec((TM, d), lambda i, *_: (i, 0)),
                      pl.BlockSpec((TOP_K, TM), lambda i, *_: (0, i))],
            out_specs=pl.BlockSpec(memory_space=pl.ANY),
            scratch_shapes=[pltpu.VMEM((N_SLOTS * ROW_SUB, LANES), F32),
                            pltpu.VMEM((N_SLOTS * ROW_SUB, LANES), F32),
                            pltpu.VMEM((TME * ROW_SUB, LANES), F32),
                            pltpu.SemaphoreType.DMA((2,)), pltpu.SemaphoreType.DMA],
        ),
        out_shape=jax.ShapeDtypeStruct((n_rows * ROW_SUB, LANES), F32),
        compiler_params=pltpu.CompilerParams(
            dimension_semantics=("arbitrary",), vmem_limit_bytes=VMEM_LIMIT),
        name="dispatch",
    )(tcnt, tloc, tglb, pad_ends, pad_at, pad_n, xn, slots)


def _expert_kernel(be_ref, nvb_ref, run_ref, rexp_ref, nrun_ref, x_ref, wi_hbm, bi_ref, wo_hbm, bo_ref,
                   y_ref, wi32_ref, wo32_ref, wib_ref, wob_ref, wsem):
    i = pl.program_id(0)
    run = run_ref[i]
    buf = run % 2

    def fetch(r, b):
        e = rexp_ref[r]
        return (pltpu.make_async_copy(wi_hbm.at[e], wi32_ref.at[b], wsem.at[b]),
                pltpu.make_async_copy(wo_hbm.at[e], wo32_ref.at[b], wsem.at[b]))

    @pl.when(i == 0)
    def _():
        for copy in fetch(0, 0):
            copy.start()

    @pl.when((i == 0) | (run != run_ref[jnp.maximum(i - 1, 0)]))
    def _():
        for copy in fetch(run, buf):
            copy.wait()

        @pl.when(run + 1 < nrun_ref[0])
        def _():
            for copy in fetch(run + 1, 1 - buf):
                copy.start()

        wib_ref[...] = wi32_ref[buf].astype(BF16)
        wob_ref[...] = wo32_ref[buf].astype(BF16)

    @pl.when(i < nvb_ref[0])
    def _():
        x = jnp.concatenate(
            [x_ref[pl.ds(sub, TME, stride=ROW_SUB), :].astype(BF16) for sub in range(ROW_SUB)],
            axis=1)
        hdn = jnp.dot(x, wib_ref[...], preferred_element_type=F32) + bi_ref[0]
        gl = jnp.minimum(hdn[:, :D_EXPERT], SWIGLU_LIMIT)
        lin = jnp.clip(hdn[:, D_EXPERT:], -SWIGLU_LIMIT, SWIGLU_LIMIT)
        act = gl * jax.nn.sigmoid(SWIGLU_ALPHA * gl) * (lin + 1.0)
        y = jnp.dot(act.astype(BF16), wob_ref[...], preferred_element_type=F32) + bo_ref[0]
        for sub in range(ROW_SUB):
            y_ref[pl.ds(sub, TME, stride=ROW_SUB), :] = y[:, sub * LANES:(sub + 1) * LANES]

    @pl.when(i >= nvb_ref[0])
    def _():
        y_ref[...] = jnp.zeros_like(y_ref)


def _experts(block_e, nvb, block_run, run_expert, n_runs, x_pad, w_in, b_in, w_out, b_out):
    n_rows = x_pad.shape[0] // ROW_SUB
    d = D_MODEL
    nb = n_rows // TME
    rows = lambda i, be, nv, *_: (jnp.minimum(i, nv[0] - 1), 0)
    wsel = lambda i, be, *_: (be[i], 0, 0)
    return pl.pallas_call(
        _expert_kernel,
        grid_spec=pltpu.PrefetchScalarGridSpec(
            num_scalar_prefetch=5,
            grid=(nb,),
            in_specs=[
                pl.BlockSpec((TME * ROW_SUB, LANES), rows),
                pl.BlockSpec(memory_space=pl.ANY),
                pl.BlockSpec((1, 1, 2 * D_EXPERT), wsel),
                pl.BlockSpec(memory_space=pl.ANY),
                pl.BlockSpec((1, 1, d), wsel),
            ],
            out_specs=pl.BlockSpec((TME * ROW_SUB, LANES), lambda i, *_: (i, 0)),
            scratch_shapes=[pltpu.VMEM((2, d, 2 * D_EXPERT), F32), pltpu.VMEM((2, D_EXPERT, d), F32),
                            pltpu.VMEM((d, 2 * D_EXPERT), BF16), pltpu.VMEM((D_EXPERT, d), BF16),
                            pltpu.SemaphoreType.DMA((2,))],
        ),
        out_shape=jax.ShapeDtypeStruct((n_rows * ROW_SUB, LANES), F32),
        compiler_params=pltpu.CompilerParams(
            dimension_semantics=("arbitrary",), vmem_limit_bytes=VMEM_LIMIT),
        name="experts",
    )(block_e, nvb, block_run, run_expert, n_runs, x_pad, w_in, b_in, w_out, b_out)


def _combine_kernel(cnt_ref, loc_ref, glb_ref, ypad_ref, h_ref, slot_ref, gate_ref, o_ref,
                    sorted0_ref, sorted1_ref, sem):
    i = pl.program_id(0)
    n = pl.num_programs(0)
    sorted_refs = (sorted0_ref, sorted1_ref)

    def fetch(tile, b, enabled=None):
        _segment_copies(
            tile, cnt_ref, loc_ref, glb_ref,
            lambda loc, glb, size: pltpu.make_async_copy(
                _slab_at(ypad_ref, glb, size), _slab_at(sorted_refs[b], loc, size), sem.at[b]),
            enabled)

    pl.when(i == 0)(lambda: fetch(0, 0))

    def step(buf):
        pltpu.make_async_copy(_slab(ypad_ref, 0, N_SLOTS), sorted_refs[buf], sem.at[buf]).wait()
        fetch(jnp.minimum(i + 1, n - 1), 1 - buf, enabled=i + 1 < n)

        acc = h_ref[...]
        for c in range(N_SLOTS // PCHUNK):
            pos = lax.broadcasted_iota(jnp.int32, (TM, PCHUNK), 1) + c * PCHUNK
            weight = jnp.zeros((TM, PCHUNK), F32)
            for k in range(TOP_K):
                weight = weight + jnp.where(pos == slot_ref[:, k:k + 1], gate_ref[:, k:k + 1], 0.0)
            y = jnp.concatenate(
                [sorted_refs[buf][pl.ds(c * PCHUNK * ROW_SUB + sub, PCHUNK, stride=ROW_SUB), :].astype(BF16)
                 for sub in range(ROW_SUB)], axis=1)
            acc = acc + jnp.dot(weight.astype(BF16), y, preferred_element_type=F32)
        o_ref[...] = acc

    pl.when(i % 2 == 0)(lambda: step(0))
    pl.when(i % 2 == 1)(lambda: step(1))


def _combine(tcnt, tloc, tglb, y_pad, h, slots_tk, gates_tk):
    t, d = h.shape
    tile = lambda i, *_: (i, 0)
    return pl.pallas_call(
        _combine_kernel,
        grid_spec=pltpu.PrefetchScalarGridSpec(
            num_scalar_prefetch=3,
            grid=(t // TM,),
            in_specs=[
                pl.BlockSpec(memory_space=pl.ANY),
                pl.BlockSpec((TM, d), tile),
                pl.BlockSpec((TM, TOP_K), tile),
                pl.BlockSpec((TM, TOP_K), tile),
            ],
            out_specs=pl.BlockSpec((TM, d), tile),
            scratch_shapes=[pltpu.VMEM((N_SLOTS * ROW_SUB, LANES), F32),
                            pltpu.VMEM((N_SLOTS * ROW_SUB, LANES), F32),
                            pltpu.SemaphoreType.DMA((2,))],
        ),
        out_shape=jax.ShapeDtypeStruct((t, d), F32),
        compiler_params=pltpu.CompilerParams(
            dimension_semantics=("arbitrary",), vmem_limit_bytes=VMEM_LIMIT),
        name="combine",
    )(tcnt, tloc, tglb, y_pad, h, slots_tk, gates_tk)


def _rel_vector(rel_bias):
    n_heads = rel_bias.shape[0]
    far = LEFT - REL_CLIP
    falling = rel_bias[:, ::-1][:, 1:]
    n_fall = min(KWIN - 1 - far, 2 * REL_CLIP)
    vec = jnp.concatenate(
        [jnp.broadcast_to(rel_bias[:, -1:], (n_heads, far + 1)), falling[:, :n_fall],
         jnp.broadcast_to(rel_bias[:, :1], (n_heads, KWIN - 1 - far - n_fall)),
         jnp.broadcast_to(rel_bias[:, -1:], (n_heads, BIAS_LANES - KWIN))], axis=1)
    return vec.astype(F32) * LOG2_E


def kernel(x, norm_mix_g, w_in_proj, q_norm_g, k_norm_g, rel_bias, gmlp_v_norm_g, gmlp_w_s, gmlp_b_s,
           att_out_norm_g, gmlp_out_norm_g, w_out_proj, norm_moe_g, w_router, b_router,
           w_expert_in, b_expert_in, w_expert_out, b_expert_out):
    b, s, d = x.shape
    t = b * s

    head_of = jnp.arange(MXU_DIM) // HEAD_DIM
    block_diag = (head_of[:, None] == head_of[None, :]).astype(BF16)
    q, kt, v, gm = _inproj(
        x, norm_mix_g.reshape(1, d), w_in_proj.astype(BF16),
        jnp.tile(q_norm_g, N_HEADS).reshape(1, ATT_W), jnp.tile(k_norm_g, N_HEADS).reshape(1, ATT_W),
        block_diag, gmlp_v_norm_g, gmlp_w_s, gmlp_b_s.T, gmlp_out_norm_g.reshape(1, G_W))
    att = _attention(q, kt, v, _rel_vector(rel_bias), att_out_norm_g.reshape(1, ATT_W))

    wr_t = w_router.T
    wr_hi = wr_t.astype(BF16)
    wr_lo = (wr_t - wr_hi.astype(F32)).astype(BF16)
    h, xn, slots, gates, tile_cnt = _outproj(
        att.reshape(t, ATT_W), gm.reshape(t, G_W), x.reshape(t, d), w_out_proj.astype(BF16),
        norm_moe_g.reshape(1, d), jnp.concatenate([wr_hi, wr_lo], axis=0), b_router.reshape(N_EXPERTS, 1))

    tcnt = tile_cnt[:, :, 0]
    counts = jnp.sum(tcnt, axis=0)
    padded = ((counts + TME - 1) // TME) * TME
    pad_ends = jnp.cumsum(padded).astype(jnp.int32)
    pad_starts = pad_ends - padded
    tloc = jnp.cumsum(tcnt, axis=1) - tcnt
    tglb = pad_starts[None, :] + jnp.cumsum(tcnt, axis=0) - tcnt
    n_blocks = (t * TOP_K) // TME + N_EXPERTS
    nvb = pad_ends[-1] // TME
    block_start = jnp.arange(n_blocks, dtype=jnp.int32) * TME
    block_e = jnp.minimum(jnp.sum(pad_ends[None, :] <= block_start[:, None], axis=1), N_EXPERTS - 1)
    block_e = jnp.where(jnp.arange(n_blocks) < nvb, block_e, block_e[nvb - 1]).astype(jnp.int32)
    runs = tuple((a * ROW_SUB).reshape(-1).astype(jnp.int32) for a in (tcnt, tloc, tglb))
    present = padded > 0
    run_of_expert = jnp.cumsum(present) - 1
    experts_iota = jnp.arange(N_EXPERTS)
    block_run = jnp.sum(jnp.where(block_e[:, None] == experts_iota[None, :], run_of_expert[None, :], 0),
                        axis=1).astype(jnp.int32)
    run_expert = jnp.sum(jnp.where(present[None, :] & (run_of_expert[None, :] == experts_iota[:, None]),
                                   experts_iota[None, :], 0), axis=1).astype(jnp.int32)
    n_runs = jnp.sum(present).astype(jnp.int32).reshape(1)

    pad_at = ((pad_starts + counts) * ROW_SUB).astype(jnp.int32)
    pad_n = ((padded - counts) * ROW_SUB).astype(jnp.int32)
    x_pad = _dispatch(*runs, pad_ends, pad_at, pad_n, xn, slots, n_blocks * TME)
    y_pad = _experts(block_e, nvb.reshape(1), block_run, run_expert, n_runs, x_pad, w_expert_in,
                     b_expert_in.reshape(N_EXPERTS, 1, 2 * D_EXPERT), w_expert_out,
                     b_expert_out.reshape(N_EXPERTS, 1, d))
    out = _combine(*runs, y_pad, h, slots.T, gates.T)
    return out.reshape(b, s, d)
```

```python
import jax
import jax.numpy as jnp
from jax import lax
from jax.experimental import pallas as pl
from jax.experimental.pallas import tpu as pltpu

D_MODEL = 1024
CHUNK = 64
LEFT = 8 * CHUNK
N_HEADS = 8
HEAD_DIM = 64
ATT_W = 512
REL_CLIP = 128
G_GROUPS = 4
G_DIM = 128
G_BLOCK = 128
G_W = 512
N_EXPERTS = 32
TOP_K = 4
D_EXPERT = 1024
SWIGLU_LIMIT = 7.0
SWIGLU_ALPHA = 1.702
EPS = 1e-6
NEG_INF = -1e30
LOG2_E = 1.4426950408889634

TM = 512
TQ = 256
KWIN = TQ + LEFT
BIAS_LANES = 1024
HALF_ROWS = TQ // 2
HALF_COLS = HALF_ROWS + LEFT
TME = 512
N_SLOTS = TM * TOP_K
PCHUNK = 256
SEG_BITS = TM.bit_length()
VMEM_LIMIT = 56 * 1024 * 1024
LANES = 128
MXU_DIM = 256
ROW_SUB = D_MODEL // LANES

F32 = jnp.float32
BF16 = jnp.bfloat16


def _rms(x, gain):
    ms = jnp.mean(x * x, axis=-1, keepdims=True)
    return x * lax.rsqrt(ms + EPS) * gain


def _gelu(x):
    return 0.5 * x * (1.0 + lax.erf(x * (2.0 ** -0.5)))


def _slab(ref, row, n_rows=1):
    return ref.at[pl.ds(pl.multiple_of(row * ROW_SUB, ROW_SUB), n_rows * ROW_SUB)]


def _slab_at(ref, offset, n_rows):
    return ref.at[pl.ds(pl.multiple_of(offset, ROW_SUB), n_rows * ROW_SUB)]


def _segment_copies(tile, cnt_ref, loc_ref, glb_ref, make_copy, enabled=None):
    def per_expert(e, c):
        n = cnt_ref[tile * N_EXPERTS + e]
        if enabled is not None:
            n = jnp.where(enabled, n, 0)
        loc = loc_ref[tile * N_EXPERTS + e]
        glb = glb_ref[tile * N_EXPERTS + e]
        for bit in range(SEG_BITS):
            size = 1 << bit
            off = n & ~(2 * size * ROW_SUB - 1)

            @pl.when((n & (size * ROW_SUB)) != 0)
            def _():
                make_copy(loc + off, glb + off, size).start()
        return c

    if enabled is None:
        lax.fori_loop(0, N_EXPERTS, per_expert, 0, unroll=2)
    else:
        for e in range(N_EXPERTS):
            per_expert(e, 0)


def _inproj_kernel(x_ref, g_ref, w_ref, gq_ref, gk_ref, bd_ref, gvn_ref, ws_ref, bst_ref, gout_ref,
                   q_ref, kt_ref, v_ref, gm_ref):
    j = pl.program_id(1)

    @pl.when(j == 0)
    def _():
        kt_ref[...] = jnp.zeros_like(kt_ref)
        v_ref[...] = jnp.zeros_like(v_ref)

    @pl.when(j > 0)
    def _():
        z = _rms(x_ref[0], g_ref[...]).astype(BF16)

        def proj(c0, width):
            return jnp.dot(z, w_ref[:, c0:c0 + width], preferred_element_type=F32)

        def head_norm(t, gain):
            sq = (t * t).astype(BF16)
            half = bd_ref.shape[0]
            ssum = jnp.concatenate(
                [jnp.dot(sq[:, c:c + half], bd_ref[...], preferred_element_type=F32)
                 for c in range(0, ATT_W, half)], axis=1)
            return t * lax.rsqrt(ssum * (1.0 / HEAD_DIM) + EPS) * gain

        q = head_norm(proj(0, ATT_W), gq_ref[...]) * (HEAD_DIM ** -0.5 * LOG2_E)
        q_ref[0] = q.astype(BF16)
        k = head_norm(proj(ATT_W, ATT_W), gk_ref[...])
        kt_ref[0] = k.T.astype(BF16)
        v_ref[0] = proj(2 * ATT_W, ATT_W).astype(BF16)

        gu = _gelu(proj(3 * ATT_W, G_W))
        gv = _gelu(proj(3 * ATT_W + G_W, G_W))
        row = lax.broadcasted_iota(jnp.int32, (G_BLOCK, G_BLOCK), 0) // CHUNK
        col = lax.broadcasted_iota(jnp.int32, (G_BLOCK, G_BLOCK), 1) // CHUNK
        tri = row >= col
        cols = []
        for g in range(G_GROUPS):
            sl = slice(g * G_DIM, (g + 1) * G_DIM)
            vn = _rms(gv[:, sl], gvn_ref[g:g + 1, :]).astype(BF16)
            wm = jnp.where(tri, ws_ref[g], 0.0).astype(BF16)
            blocks = []
            for n in range(TM // G_BLOCK):
                rs = slice(n * G_BLOCK, (n + 1) * G_BLOCK)
                gate = jnp.dot(wm, vn[rs], preferred_element_type=F32) + bst_ref[:, g:g + 1]
                blocks.append(gu[rs, sl] * gate)
            cols.append(jnp.concatenate(blocks, axis=0))
        gm = jnp.concatenate(cols, axis=1)
        gm_ref[0] = _rms(gm, gout_ref[...]).astype(BF16)


def _inproj(x, g_mix, w_in, gq, gk, bd, gvn, ws, bst, gout):
    b, s, d = x.shape
    nt = s // TM
    xmap = lambda bi, j: (bi, jnp.maximum(j - 1, 0), 0)
    const2 = lambda bi, j: (0, 0)
    return pl.pallas_call(
        _inproj_kernel,
        grid=(b, nt + 1),
        in_specs=[
            pl.BlockSpec((1, TM, d), xmap),
            pl.BlockSpec((1, d), const2),
            pl.BlockSpec(w_in.shape, const2),
            pl.BlockSpec((1, ATT_W), const2),
            pl.BlockSpec((1, ATT_W), const2),
            pl.BlockSpec(bd.shape, const2),
            pl.BlockSpec((G_GROUPS, G_DIM), const2),
            pl.BlockSpec((G_GROUPS, G_BLOCK, G_BLOCK), lambda bi, j: (0, 0, 0)),
            pl.BlockSpec((G_BLOCK, G_GROUPS), const2),
            pl.BlockSpec((1, G_W), const2),
        ],
        out_specs=[
            pl.BlockSpec((1, TM, ATT_W), xmap),
            pl.BlockSpec((1, ATT_W, TM), lambda bi, j: (bi, 0, j)),
            pl.BlockSpec((1, TM, ATT_W), lambda bi, j: (bi, j, 0)),
            pl.BlockSpec((1, TM, G_W), xmap),
        ],
        out_shape=[
            jax.ShapeDtypeStruct((b, s, ATT_W), BF16),
            jax.ShapeDtypeStruct((b, ATT_W, s + LEFT), BF16),
            jax.ShapeDtypeStruct((b, s + LEFT, ATT_W), BF16),
            jax.ShapeDtypeStruct((b, s, G_W), BF16),
        ],
        compiler_params=pltpu.CompilerParams(
            dimension_semantics=("arbitrary", "arbitrary"), vmem_limit_bytes=VMEM_LIMIT),
        name="inproj",
    )(x, g_mix, w_in, gq, gk, bd, gvn, ws, bst, gout)


def _attn_kernel(q_ref, kt_ref, v_ref, relvec_ref, gout_ref, o_ref, bias_ref):
    i = pl.program_id(1)
    q0 = pl.multiple_of(i * TQ, TQ)
    win = pl.ds(q0, KWIN)
    lane = lax.broadcasted_iota(jnp.int32, (1, 256), 1)

    @pl.when((pl.program_id(0) == 0) & (i == 0))
    def _():
        r = lax.broadcasted_iota(jnp.int32, (TQ, KWIN), 0)
        j = lax.broadcasted_iota(jnp.int32, (TQ, KWIN), 1)
        lo = (r // CHUNK) * CHUNK
        in_band = (j >= lo) & (j < lo + LEFT + CHUNK)
        for h in range(N_HEADS):
            shifted = pltpu.roll(jnp.broadcast_to(relvec_ref[h:h + 1, :], (TQ, BIAS_LANES)), 0, 1,
                                 stride=1, stride_axis=0)
            bias_ref[h] = jnp.where(in_band, shifted[:, :KWIN], NEG_INF)

    def tile(has_left_padding):
        if has_left_padding:
            colpos = lax.broadcasted_iota(jnp.int32, (1, KWIN), 1)
            pad_bias = jnp.where(colpos >= LEFT - q0, 0.0, NEG_INF).astype(F32)
        outs = []
        for g in range(2):
            gs = slice(g * 256, (g + 1) * 256)
            qg = q_ref[0, :, gs]
            ktg = kt_ref[0, gs, win]
            vg = v_ref[0, win, gs]
            acc = jnp.zeros((TQ, 256), F32)
            for hh in range(4):
                h = 4 * g + hh
                hmask = (lane >= hh * HEAD_DIM) & (lane < (hh + 1) * HEAD_DIM)
                qh = jnp.where(hmask, qg, jnp.zeros_like(qg))
                raw = jnp.dot(qh, ktg, preferred_element_type=F32)
                probs, invs = [], []
                for rows, cols in ((slice(0, HALF_ROWS), slice(0, HALF_COLS)),
                                   (slice(HALF_ROWS, TQ), slice(KWIN - HALF_COLS, KWIN))):
                    s = raw[rows, cols] + bias_ref[h, rows, cols]
                    if has_left_padding:
                        s = s + pad_bias[:, cols]
                    m = jnp.max(s, axis=-1, keepdims=True)
                    e = jnp.exp2(s - m)
                    invs.append(1.0 / jnp.sum(e, axis=-1, keepdims=True))
                    rest = jnp.zeros((HALF_ROWS, KWIN - HALF_COLS), BF16)
                    parts = [e.astype(BF16), rest] if cols.start == 0 else [rest, e.astype(BF16)]
                    probs.append(jnp.concatenate(parts, axis=1))
                p = jnp.concatenate(probs, axis=0)
                pv = jnp.dot(p, vg, preferred_element_type=F32)
                acc = jnp.where(hmask, pv * jnp.concatenate(invs, axis=0), acc)
            outs.append(acc)
        att = jnp.concatenate(outs, axis=1)
        o_ref[0] = _rms(att, gout_ref[...]).astype(BF16)

    pl.when(q0 < LEFT)(lambda: tile(True))
    pl.when(q0 >= LEFT)(lambda: tile(False))


def _attention(q, kt, v, relvec, gout):
    b, s, _ = q.shape
    return pl.pallas_call(
        _attn_kernel,
        grid=(b, s // TQ),
        in_specs=[
            pl.BlockSpec((1, TQ, ATT_W), lambda bi, i: (bi, i, 0)),
            pl.BlockSpec((1, ATT_W, s + LEFT), lambda bi, i: (bi, 0, 0)),
            pl.BlockSpec((1, s + LEFT, ATT_W), lambda bi, i: (bi, 0, 0)),
            pl.BlockSpec(relvec.shape, lambda bi, i: (0, 0)),
            pl.BlockSpec((1, ATT_W), lambda bi, i: (0, 0)),
        ],
        out_specs=pl.BlockSpec((1, TQ, ATT_W), lambda bi, i: (bi, i, 0)),
        out_shape=jax.ShapeDtypeStruct((b, s, ATT_W), BF16),
        scratch_shapes=[pltpu.VMEM((N_HEADS, TQ, KWIN), F32)],
        compiler_params=pltpu.CompilerParams(
            dimension_semantics=("arbitrary", "arbitrary"), vmem_limit_bytes=VMEM_LIMIT),
        name="attention",
    )(q, kt, v, relvec, gout)


def _outproj_kernel(a_ref, gm_ref, x_ref, wo_ref, gmoe_ref, wr_ref, br_ref,
                    h_ref, xn_ref, slot_ref, route_ref, cnt_ref):
    h = (x_ref[...]
         + jnp.dot(a_ref[...], wo_ref[:ATT_W, :], preferred_element_type=F32)
         + jnp.dot(gm_ref[...], wo_ref[ATT_W:, :], preferred_element_type=F32))
    h_ref[...] = h
    xn = _rms(h, gmoe_ref[...])
    xhi = xn.astype(BF16)
    xlo = (xn - xhi.astype(F32)).astype(BF16)
    xn_ref[...] = xhi

    nt = (((1,), (1,)), ((), ()))
    by_xhi = lax.dot_general(wr_ref[...], xhi, nt, preferred_element_type=F32)
    by_xlo = lax.dot_general(wr_ref[:N_EXPERTS, :], xlo, nt, preferred_element_type=F32)
    logits = by_xhi[:N_EXPERTS] + by_xhi[N_EXPERTS:] + by_xlo + br_ref[...]
    eidx = lax.broadcasted_iota(jnp.int32, (N_EXPERTS, TM), 0)
    vals, idxs = [], []
    cur = logits
    for _ in range(TOP_K):
        m = jnp.max(cur, axis=0, keepdims=True)
        ik = jnp.min(jnp.where(cur == m, eidx, N_EXPERTS), axis=0, keepdims=True)
        vals.append(m)
        idxs.append(ik)
        cur = jnp.where(eidx == ik, -jnp.inf, cur)
    exps = [jnp.exp(v - vals[0]) for v in vals]
    tot = exps[0] + exps[1] + exps[2] + exps[3]
    gates = [e / tot for e in exps]

    onehots = [eidx == ik for ik in idxs]
    member = sum(oh.astype(F32) for oh in onehots).astype(BF16)
    earlier_tok = (lax.broadcasted_iota(jnp.int32, (TM, TM), 0)
                   < lax.broadcasted_iota(jnp.int32, (TM, TM), 1)).astype(BF16)
    earlier_exp = (lax.broadcasted_iota(jnp.int32, (N_EXPERTS, N_EXPERTS), 1)
                   < lax.broadcasted_iota(jnp.int32, (N_EXPERTS, N_EXPERTS), 0)).astype(BF16)
    rank = jnp.dot(member, earlier_tok, preferred_element_type=F32)
    lower = jnp.sum(jnp.dot(earlier_exp, member, preferred_element_type=F32), axis=1, keepdims=True)
    base = rank + lower
    slots = [jnp.sum(jnp.where(oh, base, 0.0), axis=0, keepdims=True) for oh in onehots]
    slot_ref[...] = jnp.concatenate(slots, axis=0).astype(jnp.int32)
    route_ref[...] = jnp.concatenate(slots + gates, axis=0).T
    cnt_ref[0] = jnp.sum(member.astype(F32), axis=1, keepdims=True).astype(jnp.int32)


def _outproj(att, gm, x2, w_out, g_moe, wr_parts, br):
    t, d = x2.shape
    tile = lambda i: (i, 0)
    const = lambda i: (0, 0)
    lanes = lambda i: (0, i)
    return pl.pallas_call(
        _outproj_kernel,
        grid=(t // TM,),
        in_specs=[
            pl.BlockSpec((TM, ATT_W), tile),
            pl.BlockSpec((TM, G_W), tile),
            pl.BlockSpec((TM, d), tile),
            pl.BlockSpec(w_out.shape, const),
            pl.BlockSpec((1, d), const),
            pl.BlockSpec((2 * N_EXPERTS, d), const),
            pl.BlockSpec((N_EXPERTS, 1), const),
        ],
        out_specs=[
            pl.BlockSpec((TM, d), tile),
            pl.BlockSpec((TM, d), tile),
            pl.BlockSpec((TOP_K, TM), lanes),
            pl.BlockSpec((TM, 2 * TOP_K), tile),
            pl.BlockSpec((1, N_EXPERTS, 1), lambda i: (i, 0, 0)),
        ],
        out_shape=[
            jax.ShapeDtypeStruct((t, d), F32),
            jax.ShapeDtypeStruct((t, d), BF16),
            jax.ShapeDtypeStruct((TOP_K, t), jnp.int32),
            jax.ShapeDtypeStruct((t, 2 * TOP_K), F32),
            jax.ShapeDtypeStruct((t // TM, N_EXPERTS, 1), jnp.int32),
        ],
        compiler_params=pltpu.CompilerParams(
            dimension_semantics=("arbitrary",), vmem_limit_bytes=VMEM_LIMIT),
        name="outproj_router",
    )(att, gm, x2, w_out, g_moe, wr_parts, br)


def _dispatch_kernel(cnt_ref, loc_ref, glb_ref, pend_ref, padat_ref, padn_ref, xn_ref, slot_ref, xpad_ref,
                     sorted0_ref, sorted1_ref, zero_ref, sem, zsem):
    i = pl.program_id(0)
    n = pl.num_programs(0)
    sorted_refs = (sorted0_ref, sorted1_ref)

    def drain(b):
        pltpu.make_async_copy(sorted_refs[b], _slab(xpad_ref, 0, N_SLOTS), sem.at[b]).wait()

    @pl.when(i == 0)
    def _():
        zero_ref[...] = jnp.zeros_like(zero_ref)

        def padding(wait):
            def per_expert(e, c):
                n = padn_ref[e]
                for bit in range(TME.bit_length() - 1):
                    size = 1 << bit
                    off = n & ~(2 * size * ROW_SUB - 1)

                    @pl.when((n & (size * ROW_SUB)) != 0)
                    def _():
                        copy = pltpu.make_async_copy(
                            _slab(zero_ref, 0, size), _slab_at(xpad_ref, padat_ref[e] + off, size), zsem)
                        copy.wait() if wait else copy.start()
                return c

            lax.fori_loop(0, N_EXPERTS, per_expert, 0)

        padding(wait=False)
        padding(wait=True)

        def tail_block(blk):
            return pltpu.make_async_copy(zero_ref, _slab(xpad_ref, blk * TME, TME), zsem)

        first_tail = pend_ref[N_EXPERTS - 1] // TME
        n_blocks = xpad_ref.shape[0] // (TME * ROW_SUB)
        lax.fori_loop(first_tail, n_blocks, lambda blk, c: (tail_block(blk).start(), c)[1], 0)
        lax.fori_loop(first_tail, n_blocks, lambda blk, c: (tail_block(blk).wait(), c)[1], 0)

    def send(tile, b, enabled=None):
        _segment_copies(
            tile, cnt_ref, loc_ref, glb_ref,
            lambda loc, glb, size: pltpu.make_async_copy(
                _slab_at(sorted_refs[b], loc, size), _slab_at(xpad_ref, glb, size), sem.at[b]),
            enabled)

    def step(buf):
        pl.when(i >= 2)(lambda: drain(buf))

        xn = xn_ref[...]
        for c in range(N_SLOTS // PCHUNK):
            pos = lax.broadcasted_iota(jnp.int32, (PCHUNK, TM), 0) + c * PCHUNK
            hit = pos == slot_ref[0:1, :]
            for k in range(1, TOP_K):
                hit = hit | (pos == slot_ref[k:k + 1, :])
            rows = jnp.dot(hit.astype(F32).astype(BF16), xn, preferred_element_type=F32)
            for sub in range(ROW_SUB):
                sorted_refs[buf][pl.ds(c * PCHUNK * ROW_SUB + sub, PCHUNK, stride=ROW_SUB), :] = (
                    rows[:, sub * LANES:(sub + 1) * LANES])

        send(i, buf)

        @pl.when(i == n - 1)
        def _():
            pl.when(i >= 1)(lambda: drain(1 - buf))
            drain(buf)

    pl.when(i % 2 == 0)(lambda: step(0))
    pl.when(i % 2 == 1)(lambda: step(1))


def _dispatch(tcnt, tloc, tglb, pad_ends, pad_at, pad_n, xn, slots, n_rows):
    t, d = xn.shape
    return pl.pallas_call(
        _dispatch_kernel,
        grid_spec=pltpu.PrefetchScalarGridSpec(
            num_scalar_prefetch=6,
            grid=(t // TM,),
            in_specs=[pl.BlockSpec((TM, d), lambda i, *_: (i, 0)),
                      pl.BlockSpec((TOP_K, TM), lambda i, *_: (0, i))],
            out_specs=pl.BlockSpec(memory_space=pl.ANY),
            scratch_shapes=[pltpu.VMEM((N_SLOTS * ROW_SUB, LANES), F32),
                            pltpu.VMEM((N_SLOTS * ROW_SUB, LANES), F32),
                            pltpu.VMEM((TME * ROW_SUB, LANES), F32),
                            pltpu.SemaphoreType.DMA((2,)), pltpu.SemaphoreType.DMA],
        ),
        out_shape=jax.ShapeDtypeStruct((n_rows * ROW_SUB, LANES), F32),
        compiler_params=pltpu.CompilerParams(
            dimension_semantics=("arbitrary",), vmem_limit_bytes=VMEM_LIMIT),
        name="dispatch",
    )(tcnt, tloc, tglb, pad_ends, pad_at, pad_n, xn, slots)


def _expert_kernel(be_ref, nvb_ref, run_ref, rexp_ref, nrun_ref, x_ref, wi_hbm, bi_ref, wo_hbm, bo_ref,
                   y_ref, wi32_ref, wo32_ref, wib_ref, wob_ref, wsem):
    i = pl.program_id(0)
    run = run_ref[i]
    buf = run % 2

    def fetch(r, b):
        e = rexp_ref[r]
        return (pltpu.make_async_copy(wi_hbm.at[e], wi32_ref.at[b], wsem.at[b]),
                pltpu.make_async_copy(wo_hbm.at[e], wo32_ref.at[b], wsem.at[b]))

    @pl.when(i == 0)
    def _():
        for copy in fetch(0, 0):
            copy.start()

    @pl.when((i == 0) | (run != run_ref[jnp.maximum(i - 1, 0)]))
    def _():
        for copy in fetch(run, buf):
            copy.wait()

        @pl.when(run + 1 < nrun_ref[0])
        def _():
            for copy in fetch(run + 1, 1 - buf):
                copy.start()

        wib_ref[...] = wi32_ref[buf].astype(BF16)
        wob_ref[...] = wo32_ref[buf].astype(BF16)

    @pl.when(i < nvb_ref[0])
    def _():
        x = jnp.concatenate(
            [x_ref[pl.ds(sub, TME, stride=ROW_SUB), :].astype(BF16) for sub in range(ROW_SUB)],
            axis=1)
        hdn = jnp.dot(x, wib_ref[...], preferred_element_type=F32) + bi_ref[0]
        gl = jnp.minimum(hdn[:, :D_EXPERT], SWIGLU_LIMIT)
        lin = jnp.clip(hdn[:, D_EXPERT:], -SWIGLU_LIMIT, SWIGLU_LIMIT)
        act = gl * jax.nn.sigmoid(SWIGLU_ALPHA * gl) * (lin + 1.0)
        y = jnp.dot(act.astype(BF16), wob_ref[...], preferred_element_type=F32) + bo_ref[0]
        for sub in range(ROW_SUB):
            y_ref[pl.ds(sub, TME, stride=ROW_SUB), :] = y[:, sub * LANES:(sub + 1) * LANES]

    @pl.when(i >= nvb_ref[0])
    def _():
        y_ref[...] = jnp.zeros_like(y_ref)


def _experts(block_e, nvb, block_run, run_expert, n_runs, x_pad, w_in, b_in, w_out, b_out):
    n_rows = x_pad.shape[0] // ROW_SUB
    d = D_MODEL
    nb = n_rows // TME
    rows = lambda i, be, nv, *_: (jnp.minimum(i, nv[0] - 1), 0)
    wsel = lambda i, be, *_: (be[i], 0, 0)
    return pl.pallas_call(
        _expert_kernel,
        grid_spec=pltpu.PrefetchScalarGridSpec(
            num_scalar_prefetch=5,
            grid=(nb,),
            in_specs=[
                pl.BlockSpec((TME * ROW_SUB, LANES), rows),
                pl.BlockSpec(memory_space=pl.ANY),
                pl.BlockSpec((1, 1, 2 * D_EXPERT), wsel),
                pl.BlockSpec(memory_space=pl.ANY),
                pl.BlockSpec((1, 1, d), wsel),
            ],
            out_specs=pl.BlockSpec((TME * ROW_SUB, LANES), lambda i, *_: (i, 0)),
            scratch_shapes=[pltpu.VMEM((2, d, 2 * D_EXPERT), F32), pltpu.VMEM((2, D_EXPERT, d), F32),
                            pltpu.VMEM((d, 2 * D_EXPERT), BF16), pltpu.VMEM((D_EXPERT, d), BF16),
                            pltpu.SemaphoreType.DMA((2,))],
        ),
        out_shape=jax.ShapeDtypeStruct((n_rows * ROW_SUB, LANES), F32),
        compiler_params=pltpu.CompilerParams(
            dimension_semantics=("arbitrary",), vmem_limit_bytes=VMEM_LIMIT),
        name="experts",
    )(block_e, nvb, block_run, run_expert, n_runs, x_pad, w_in, b_in, w_out, b_out)


def _combine_kernel(cnt_ref, loc_ref, glb_ref, ypad_ref, h_ref, route_ref, o_ref,
                    sorted0_ref, sorted1_ref, sem):
    i = pl.program_id(0)
    n = pl.num_programs(0)
    sorted_refs = (sorted0_ref, sorted1_ref)

    def fetch(tile, b, enabled=None):
        _segment_copies(
            tile, cnt_ref, loc_ref, glb_ref,
            lambda loc, glb, size: pltpu.make_async_copy(
                _slab_at(ypad_ref, glb, size), _slab_at(sorted_refs[b], loc, size), sem.at[b]),
            enabled)

    pl.when(i == 0)(lambda: fetch(0, 0))

    def step(buf):
        pltpu.make_async_copy(_slab(ypad_ref, 0, N_SLOTS), sorted_refs[buf], sem.at[buf]).wait()
        fetch(jnp.minimum(i + 1, n - 1), 1 - buf, enabled=i + 1 < n)

        acc = h_ref[...]
        slot_cols = route_ref[:, :TOP_K].astype(jnp.int32)
        for c in range(N_SLOTS // PCHUNK):
            pos = lax.broadcasted_iota(jnp.int32, (TM, PCHUNK), 1) + c * PCHUNK
            weight = jnp.zeros((TM, PCHUNK), F32)
            for k in range(TOP_K):
                weight = weight + jnp.where(pos == slot_cols[:, k:k + 1],
                                            route_ref[:, TOP_K + k:TOP_K + k + 1], 0.0)
            y = jnp.concatenate(
                [sorted_refs[buf][pl.ds(c * PCHUNK * ROW_SUB + sub, PCHUNK, stride=ROW_SUB), :].astype(BF16)
                 for sub in range(ROW_SUB)], axis=1)
            acc = acc + jnp.dot(weight.astype(BF16), y, preferred_element_type=F32)
        o_ref[...] = acc

    pl.when(i % 2 == 0)(lambda: step(0))
    pl.when(i % 2 == 1)(lambda: step(1))


def _combine(tcnt, tloc, tglb, y_pad, h, route):
    t, d = h.shape
    tile = lambda i, *_: (i, 0)
    return pl.pallas_call(
        _combine_kernel,
        grid_spec=pltpu.PrefetchScalarGridSpec(
            num_scalar_prefetch=3,
            grid=(t // TM,),
            in_specs=[
                pl.BlockSpec(memory_space=pl.ANY),
                pl.BlockSpec((TM, d), tile),
                pl.BlockSpec((TM, 2 * TOP_K), tile),
            ],
            out_specs=pl.BlockSpec((TM, d), tile),
            scratch_shapes=[pltpu.VMEM((N_SLOTS * ROW_SUB, LANES), F32),
                            pltpu.VMEM((N_SLOTS * ROW_SUB, LANES), F32),
                            pltpu.SemaphoreType.DMA((2,))],
        ),
        out_shape=jax.ShapeDtypeStruct((t, d), F32),
        compiler_params=pltpu.CompilerParams(
            dimension_semantics=("arbitrary",), vmem_limit_bytes=VMEM_LIMIT),
        name="combine",
    )(tcnt, tloc, tglb, y_pad, h, route)


def _rel_vector(rel_bias):
    n_heads = rel_bias.shape[0]
    far = LEFT - REL_CLIP
    falling = rel_bias[:, ::-1][:, 1:]
    n_fall = min(KWIN - 1 - far, 2 * REL_CLIP)
    vec = jnp.concatenate(
        [jnp.broadcast_to(rel_bias[:, -1:], (n_heads, far + 1)), falling[:, :n_fall],
         jnp.broadcast_to(rel_bias[:, :1], (n_heads, KWIN - 1 - far - n_fall)),
         jnp.broadcast_to(rel_bias[:, -1:], (n_heads, BIAS_LANES - KWIN))], axis=1)
    return vec.astype(F32) * LOG2_E


def kernel(x, norm_mix_g, w_in_proj, q_norm_g, k_norm_g, rel_bias, gmlp_v_norm_g, gmlp_w_s, gmlp_b_s,
           att_out_norm_g, gmlp_out_norm_g, w_out_proj, norm_moe_g, w_router, b_router,
           w_expert_in, b_expert_in, w_expert_out, b_expert_out):
    b, s, d = x.shape
    t = b * s

    head_of = jnp.arange(MXU_DIM) // HEAD_DIM
    block_diag = (head_of[:, None] == head_of[None, :]).astype(BF16)
    q, kt, v, gm = _inproj(
        x, norm_mix_g.reshape(1, d), w_in_proj.astype(BF16),
        jnp.tile(q_norm_g, N_HEADS).reshape(1, ATT_W), jnp.tile(k_norm_g, N_HEADS).reshape(1, ATT_W),
        block_diag, gmlp_v_norm_g, gmlp_w_s, gmlp_b_s.T, gmlp_out_norm_g.reshape(1, G_W))
    att = _attention(q, kt, v, _rel_vector(rel_bias), att_out_norm_g.reshape(1, ATT_W))

    wr_t = w_router.T
    wr_hi = wr_t.astype(BF16)
    wr_lo = (wr_t - wr_hi.astype(F32)).astype(BF16)
    h, xn, slots, route, tile_cnt = _outproj(
        att.reshape(t, ATT_W), gm.reshape(t, G_W), x.reshape(t, d), w_out_proj.astype(BF16),
        norm_moe_g.reshape(1, d), jnp.concatenate([wr_hi, wr_lo], axis=0), b_router.reshape(N_EXPERTS, 1))

    tcnt = tile_cnt[:, :, 0]
    counts = jnp.sum(tcnt, axis=0)
    padded = ((counts + TME - 1) // TME) * TME
    pad_ends = jnp.cumsum(padded).astype(jnp.int32)
    pad_starts = pad_ends - padded
    tloc = jnp.cumsum(tcnt, axis=1) - tcnt
    tglb = pad_starts[None, :] + jnp.cumsum(tcnt, axis=0) - tcnt
    n_blocks = (t * TOP_K) // TME + N_EXPERTS
    nvb = pad_ends[-1] // TME
    block_start = jnp.arange(n_blocks, dtype=jnp.int32) * TME
    block_e = jnp.minimum(jnp.sum(pad_ends[None, :] <= block_start[:, None], axis=1), N_EXPERTS - 1)
    block_e = jnp.where(jnp.arange(n_blocks) < nvb, block_e, block_e[nvb - 1]).astype(jnp.int32)
    runs = tuple((a * ROW_SUB).reshape(-1).astype(jnp.int32) for a in (tcnt, tloc, tglb))
    present = padded > 0
    run_of_expert = jnp.cumsum(present) - 1
    experts_iota = jnp.arange(N_EXPERTS)
    block_run = jnp.sum(jnp.where(block_e[:, None] == experts_iota[None, :], run_of_expert[None, :], 0),
                        axis=1).astype(jnp.int32)
    run_expert = jnp.sum(jnp.where(present[None, :] & (run_of_expert[None, :] == experts_iota[:, None]),
                                   experts_iota[None, :], 0), axis=1).astype(jnp.int32)
    n_runs = jnp.sum(present).astype(jnp.int32).reshape(1)

    pad_at = ((pad_starts + counts) * ROW_SUB).astype(jnp.int32)
    pad_n = ((padded - counts) * ROW_SUB).astype(jnp.int32)
    x_pad = _dispatch(*runs, pad_ends, pad_at, pad_n, xn, slots, n_blocks * TME)
    y_pad = _experts(block_e, nvb.reshape(1), block_run, run_expert, n_runs, x_pad, w_expert_in,
                     b_expert_in.reshape(N_EXPERTS, 1, 2 * D_EXPERT), w_expert_out,
                     b_expert_out.reshape(N_EXPERTS, 1, d))
    out = _combine(*runs, y_pad, h, route)
    return out.reshape(b, s, d)
```

```python
import jax
import jax.numpy as jnp
from jax import lax
from jax.experimental import pallas as pl
from jax.experimental.pallas import tpu as pltpu

D_MODEL = 1024
CHUNK = 64
LEFT = 8 * CHUNK
N_HEADS = 8
HEAD_DIM = 64
ATT_W = 512
REL_CLIP = 128
G_GROUPS = 4
G_DIM = 128
G_BLOCK = 128
G_W = 512
N_EXPERTS = 32
TOP_K = 4
D_EXPERT = 1024
SWIGLU_LIMIT = 7.0
SWIGLU_ALPHA = 1.702
EPS = 1e-6
NEG_INF = -1e30
LOG2_E = 1.4426950408889634

TM = 512
TQ = 256
KWIN = TQ + LEFT
BIAS_LANES = 1024
HALF_ROWS = TQ // 2
HALF_COLS = HALF_ROWS + LEFT
TME = 512
N_SLOTS = TM * TOP_K
PCHUNK = 256
SEG_BITS = TM.bit_length()
VMEM_LIMIT = 56 * 1024 * 1024
LANES = 128
MXU_DIM = 256
ROW_SUB = D_MODEL // LANES

F32 = jnp.float32
BF16 = jnp.bfloat16


def _rms(x, gain):
    ms = jnp.mean(x * x, axis=-1, keepdims=True)
    return x * lax.rsqrt(ms + EPS) * gain


def _gelu(x):
    return 0.5 * x * (1.0 + lax.erf(x * (2.0 ** -0.5)))


def _slab(ref, row, n_rows=1):
    return ref.at[pl.ds(pl.multiple_of(row * ROW_SUB, ROW_SUB), n_rows * ROW_SUB)]


def _slab_at(ref, offset, n_rows):
    return ref.at[pl.ds(pl.multiple_of(offset, ROW_SUB), n_rows * ROW_SUB)]


def _segment_copies(tile, cnt_ref, loc_ref, glb_ref, make_copy, enabled=None):
    def per_expert(e, c):
        n = cnt_ref[tile * N_EXPERTS + e]
        if enabled is not None:
            n = jnp.where(enabled, n, 0)
        loc = loc_ref[tile * N_EXPERTS + e]
        glb = glb_ref[tile * N_EXPERTS + e]
        for bit in range(SEG_BITS):
            size = 1 << bit
            off = n & ~(2 * size * ROW_SUB - 1)

            @pl.when((n & (size * ROW_SUB)) != 0)
            def _():
                make_copy(loc + off, glb + off, size).start()
        return c

    if enabled is None:
        lax.fori_loop(0, N_EXPERTS, per_expert, 0, unroll=2)
    else:
        for e in range(N_EXPERTS):
            per_expert(e, 0)


def _inproj_kernel(x_ref, g_ref, w_ref, gq_ref, gk_ref, bd_ref, gvn_ref, ws_ref, bst_ref, gout_ref,
                   q_ref, kt_ref, v_ref, gm_ref):
    j = pl.program_id(1)

    @pl.when(j == 0)
    def _():
        kt_ref[...] = jnp.zeros_like(kt_ref)
        v_ref[...] = jnp.zeros_like(v_ref)

    @pl.when(j > 0)
    def _():
        z = _rms(x_ref[0], g_ref[...]).astype(BF16)

        def proj(c0, width):
            return jnp.dot(z, w_ref[:, c0:c0 + width], preferred_element_type=F32)

        def head_norm(t, gain):
            sq = (t * t).astype(BF16)
            half = bd_ref.shape[0]
            ssum = jnp.concatenate(
                [jnp.dot(sq[:, c:c + half], bd_ref[...], preferred_element_type=F32)
                 for c in range(0, ATT_W, half)], axis=1)
            return t * lax.rsqrt(ssum * (1.0 / HEAD_DIM) + EPS) * gain

        q = head_norm(proj(0, ATT_W), gq_ref[...]) * (HEAD_DIM ** -0.5 * LOG2_E)
        q_ref[0] = q.astype(BF16)
        k = head_norm(proj(ATT_W, ATT_W), gk_ref[...])
        kt_ref[0] = k.T.astype(BF16)
        v_ref[0] = proj(2 * ATT_W, ATT_W).astype(BF16)

        gu = _gelu(proj(3 * ATT_W, G_W))
        gv = _gelu(proj(3 * ATT_W + G_W, G_W))
        row = lax.broadcasted_iota(jnp.int32, (G_BLOCK, G_BLOCK), 0) // CHUNK
        col = lax.broadcasted_iota(jnp.int32, (G_BLOCK, G_BLOCK), 1) // CHUNK
        tri = row >= col
        cols = []
        for g in range(G_GROUPS):
            sl = slice(g * G_DIM, (g + 1) * G_DIM)
            vn = _rms(gv[:, sl], gvn_ref[g:g + 1, :]).astype(BF16)
            wm = jnp.where(tri, ws_ref[g], 0.0).astype(BF16)
            blocks = []
            for n in range(TM // G_BLOCK):
                rs = slice(n * G_BLOCK, (n + 1) * G_BLOCK)
                gate = jnp.dot(wm, vn[rs], preferred_element_type=F32) + bst_ref[:, g:g + 1]
                blocks.append(gu[rs, sl] * gate)
            cols.append(jnp.concatenate(blocks, axis=0))
        gm = jnp.concatenate(cols, axis=1)
        gm_ref[0] = _rms(gm, gout_ref[...]).astype(BF16)


def _inproj(x, g_mix, w_in, gq, gk, bd, gvn, ws, bst, gout):
    b, s, d = x.shape
    nt = s // TM
    xmap = lambda bi, j: (bi, jnp.maximum(j - 1, 0), 0)
    const2 = lambda bi, j: (0, 0)
    return pl.pallas_call(
        _inproj_kernel,
        grid=(b, nt + 1),
        in_specs=[
            pl.BlockSpec((1, TM, d), xmap),
            pl.BlockSpec((1, d), const2),
            pl.BlockSpec(w_in.shape, const2),
            pl.BlockSpec((1, ATT_W), const2),
            pl.BlockSpec((1, ATT_W), const2),
            pl.BlockSpec(bd.shape, const2),
            pl.BlockSpec((G_GROUPS, G_DIM), const2),
            pl.BlockSpec((G_GROUPS, G_BLOCK, G_BLOCK), lambda bi, j: (0, 0, 0)),
            pl.BlockSpec((G_BLOCK, G_GROUPS), const2),
            pl.BlockSpec((1, G_W), const2),
        ],
        out_specs=[
            pl.BlockSpec((1, TM, ATT_W), xmap),
            pl.BlockSpec((1, ATT_W, TM), lambda bi, j: (bi, 0, j)),
            pl.BlockSpec((1, TM, ATT_W), lambda bi, j: (bi, j, 0)),
            pl.BlockSpec((1, TM, G_W), xmap),
        ],
        out_shape=[
            jax.ShapeDtypeStruct((b, s, ATT_W), BF16),
            jax.ShapeDtypeStruct((b, ATT_W, s + LEFT), BF16),
            jax.ShapeDtypeStruct((b, s + LEFT, ATT_W), BF16),
            jax.ShapeDtypeStruct((b, s, G_W), BF16),
        ],
        compiler_params=pltpu.CompilerParams(
            dimension_semantics=("arbitrary", "arbitrary"), vmem_limit_bytes=VMEM_LIMIT),
        name="inproj",
    )(x, g_mix, w_in, gq, gk, bd, gvn, ws, bst, gout)


def _attn_kernel(q_ref, kt_hbm, v_hbm, relvec_ref, gout_ref, o_ref, bias_ref, kt_ref, v_ref, kv_sem):
    b = pl.program_id(0)
    i = pl.program_id(1)
    buf = b % 2
    q0 = pl.multiple_of(i * TQ, TQ)
    win = pl.ds(q0, KWIN)
    lane = lax.broadcasted_iota(jnp.int32, (1, 256), 1)

    def fetch(batch, slot):
        return (pltpu.make_async_copy(kt_hbm.at[batch], kt_ref.at[slot], kv_sem.at[slot]),
                pltpu.make_async_copy(v_hbm.at[batch], v_ref.at[slot], kv_sem.at[slot]))

    @pl.when(i == 0)
    def _():
        @pl.when(b == 0)
        def _():
            for copy in fetch(0, 0):
                copy.start()

        for copy in fetch(b, buf):
            copy.wait()

        @pl.when(b + 1 < pl.num_programs(0))
        def _():
            for copy in fetch(b + 1, 1 - buf):
                copy.start()

    @pl.when((pl.program_id(0) == 0) & (i == 0))
    def _():
        r = lax.broadcasted_iota(jnp.int32, (TQ, KWIN), 0)
        j = lax.broadcasted_iota(jnp.int32, (TQ, KWIN), 1)
        lo = (r // CHUNK) * CHUNK
        in_band = (j >= lo) & (j < lo + LEFT + CHUNK)
        for h in range(N_HEADS):
            shifted = pltpu.roll(jnp.broadcast_to(relvec_ref[h:h + 1, :], (TQ, BIAS_LANES)), 0, 1,
                                 stride=1, stride_axis=0)
            bias_ref[h] = jnp.where(in_band, shifted[:, :KWIN], NEG_INF)

    def tile(has_left_padding):
        if has_left_padding:
            colpos = lax.broadcasted_iota(jnp.int32, (1, KWIN), 1)
            pad_bias = jnp.where(colpos >= LEFT - q0, 0.0, NEG_INF).astype(F32)
        outs = []
        for g in range(2):
            gs = slice(g * 256, (g + 1) * 256)
            qg = q_ref[0, :, gs]
            ktg = kt_ref[buf, gs, win]
            vg = v_ref[buf, win, gs]
            acc = jnp.zeros((TQ, 256), F32)
            for hh in range(4):
                h = 4 * g + hh
                hmask = (lane >= hh * HEAD_DIM) & (lane < (hh + 1) * HEAD_DIM)
                qh = jnp.where(hmask, qg, jnp.zeros_like(qg))
                raw = jnp.dot(qh, ktg, preferred_element_type=F32)
                probs, invs = [], []
                for rows, cols in ((slice(0, HALF_ROWS), slice(0, HALF_COLS)),
                                   (slice(HALF_ROWS, TQ), slice(KWIN - HALF_COLS, KWIN))):
                    s = raw[rows, cols] + bias_ref[h, rows, cols]
                    if has_left_padding:
                        s = s + pad_bias[:, cols]
                    m = jnp.max(s, axis=-1, keepdims=True)
                    e = jnp.exp2(s - m)
                    invs.append(1.0 / jnp.sum(e, axis=-1, keepdims=True))
                    rest = jnp.zeros((HALF_ROWS, KWIN - HALF_COLS), BF16)
                    parts = [e.astype(BF16), rest] if cols.start == 0 else [rest, e.astype(BF16)]
                    probs.append(jnp.concatenate(parts, axis=1))
                p = jnp.concatenate(probs, axis=0)
                pv = jnp.dot(p, vg, preferred_element_type=F32)
                acc = jnp.where(hmask, pv * jnp.concatenate(invs, axis=0), acc)
            outs.append(acc)
        att = jnp.concatenate(outs, axis=1)
        o_ref[0] = _rms(att, gout_ref[...]).astype(BF16)

    pl.when(q0 < LEFT)(lambda: tile(True))
    pl.when(q0 >= LEFT)(lambda: tile(False))


def _attention(q, kt, v, relvec, gout):
    b, s, _ = q.shape
    return pl.pallas_call(
        _attn_kernel,
        grid=(b, s // TQ),
        in_specs=[
            pl.BlockSpec((1, TQ, ATT_W), lambda bi, i: (bi, i, 0)),
            pl.BlockSpec(memory_space=pl.ANY),
            pl.BlockSpec(memory_space=pl.ANY),
            pl.BlockSpec(relvec.shape, lambda bi, i: (0, 0)),
            pl.BlockSpec((1, ATT_W), lambda bi, i: (0, 0)),
        ],
        out_specs=pl.BlockSpec((1, TQ, ATT_W), lambda bi, i: (bi, i, 0)),
        out_shape=jax.ShapeDtypeStruct((b, s, ATT_W), BF16),
        scratch_shapes=[pltpu.VMEM((N_HEADS, TQ, KWIN), F32),
                        pltpu.VMEM((2, ATT_W, s + LEFT), BF16), pltpu.VMEM((2, s + LEFT, ATT_W), BF16),
                        pltpu.SemaphoreType.DMA((2,))],
        compiler_params=pltpu.CompilerParams(
            dimension_semantics=("arbitrary", "arbitrary"), vmem_limit_bytes=VMEM_LIMIT),
        name="attention",
    )(q, kt, v, relvec, gout)


def _outproj_kernel(a_ref, gm_ref, x_ref, wo_ref, gmoe_ref, wr_ref, br_ref,
                    h_ref, xn_ref, slot_ref, route_ref, cnt_ref, earlier_ref):
    h = (x_ref[...]
         + jnp.dot(a_ref[...], wo_ref[:ATT_W, :], preferred_element_type=F32)
         + jnp.dot(gm_ref[...], wo_ref[ATT_W:, :], preferred_element_type=F32))
    h_ref[...] = h
    xn = _rms(h, gmoe_ref[...])
    xhi = xn.astype(BF16)
    xlo = (xn - xhi.astype(F32)).astype(BF16)
    xn_ref[...] = xhi

    nt = (((1,), (1,)), ((), ()))
    by_xhi = lax.dot_general(wr_ref[...], xhi, nt, preferred_element_type=F32)
    by_xlo = lax.dot_general(wr_ref[:N_EXPERTS, :], xlo, nt, preferred_element_type=F32)
    logits = by_xhi[:N_EXPERTS] + by_xhi[N_EXPERTS:] + by_xlo + br_ref[...]
    eidx = lax.broadcasted_iota(jnp.int32, (N_EXPERTS, TM), 0)
    vals, idxs = [], []
    cur = logits
    for _ in range(TOP_K):
        m = jnp.max(cur, axis=0, keepdims=True)
        ik = jnp.min(jnp.where(cur == m, eidx, N_EXPERTS), axis=0, keepdims=True)
        vals.append(m)
        idxs.append(ik)
        cur = jnp.where(eidx == ik, -jnp.inf, cur)
    exps = [jnp.exp(v - vals[0]) for v in vals]
    tot = exps[0] + exps[1] + exps[2] + exps[3]
    gates = [e / tot for e in exps]

    onehots = [eidx == ik for ik in idxs]
    member = sum(oh.astype(F32) for oh in onehots).astype(BF16)

    @pl.when(pl.program_id(0) == 0)
    def _():
        earlier_ref[...] = (lax.broadcasted_iota(jnp.int32, (TM, TM), 0)
                            < lax.broadcasted_iota(jnp.int32, (TM, TM), 1)).astype(BF16)

    earlier_exp = (lax.broadcasted_iota(jnp.int32, (N_EXPERTS, N_EXPERTS), 1)
                   < lax.broadcasted_iota(jnp.int32, (N_EXPERTS, N_EXPERTS), 0)).astype(BF16)
    rank = jnp.dot(member, earlier_ref[...], preferred_element_type=F32)
    lower = jnp.sum(jnp.dot(earlier_exp, member, preferred_element_type=F32), axis=1, keepdims=True)
    base = rank + lower
    slots = [jnp.sum(jnp.where(oh, base, 0.0), axis=0, keepdims=True) for oh in onehots]
    slot_ref[...] = jnp.concatenate(slots, axis=0).astype(jnp.int32)
    route_ref[...] = jnp.concatenate(slots + gates, axis=0).T
    cnt_ref[0] = jnp.sum(member.astype(F32), axis=1, keepdims=True).astype(jnp.int32)


def _outproj(att, gm, x2, w_out, g_moe, wr_parts, br):
    t, d = x2.shape
    tile = lambda i: (i, 0)
    const = lambda i: (0, 0)
    lanes = lambda i: (0, i)
    return pl.pallas_call(
        _outproj_kernel,
        grid=(t // TM,),
        in_specs=[
            pl.BlockSpec((TM, ATT_W), tile),
            pl.BlockSpec((TM, G_W), tile),
            pl.BlockSpec((TM, d), tile),
            pl.BlockSpec(w_out.shape, const),
            pl.BlockSpec((1, d), const),
            pl.BlockSpec((2 * N_EXPERTS, d), const),
            pl.BlockSpec((N_EXPERTS, 1), const),
        ],
        out_specs=[
            pl.BlockSpec((TM, d), tile),
            pl.BlockSpec((TM, d), tile),
            pl.BlockSpec((TOP_K, TM), lanes),
            pl.BlockSpec((TM, 2 * TOP_K), tile),
            pl.BlockSpec((1, N_EXPERTS, 1), lambda i: (i, 0, 0)),
        ],
        out_shape=[
            jax.ShapeDtypeStruct((t, d), F32),
            jax.ShapeDtypeStruct((t, d), BF16),
            jax.ShapeDtypeStruct((TOP_K, t), jnp.int32),
            jax.ShapeDtypeStruct((t, 2 * TOP_K), F32),
            jax.ShapeDtypeStruct((t // TM, N_EXPERTS, 1), jnp.int32),
        ],
        scratch_shapes=[pltpu.VMEM((TM, TM), BF16)],
        compiler_params=pltpu.CompilerParams(
            dimension_semantics=("arbitrary",), vmem_limit_bytes=VMEM_LIMIT),
        name="outproj_router",
    )(att, gm, x2, w_out, g_moe, wr_parts, br)


def _dispatch_kernel(cnt_ref, loc_ref, glb_ref, pend_ref, padat_ref, padn_ref, xn_ref, slot_ref, xpad_ref,
                     sorted0_ref, sorted1_ref, zero_ref, sem, zsem):
    i = pl.program_id(0)
    n = pl.num_programs(0)
    sorted_refs = (sorted0_ref, sorted1_ref)

    def drain(b):
        pltpu.make_async_copy(sorted_refs[b], _slab(xpad_ref, 0, N_SLOTS), sem.at[b]).wait()

    @pl.when(i == 0)
    def _():
        zero_ref[...] = jnp.zeros_like(zero_ref)

        def padding(wait):
            def per_expert(e, c):
                n = padn_ref[e]
                for bit in range(TME.bit_length() - 1):
                    size = 1 << bit
                    off = n & ~(2 * size * ROW_SUB - 1)

                    @pl.when((n & (size * ROW_SUB)) != 0)
                    def _():
                        copy = pltpu.make_async_copy(
                            _slab(zero_ref, 0, size), _slab_at(xpad_ref, padat_ref[e] + off, size), zsem)
                        copy.wait() if wait else copy.start()
                return c

            lax.fori_loop(0, N_EXPERTS, per_expert, 0)

        padding(wait=False)
        padding(wait=True)

        def tail_block(blk):
            return pltpu.make_async_copy(zero_ref, _slab(xpad_ref, blk * TME, TME), zsem)

        first_tail = pend_ref[N_EXPERTS - 1] // TME
        n_blocks = xpad_ref.shape[0] // (TME * ROW_SUB)
        lax.fori_loop(first_tail, n_blocks, lambda blk, c: (tail_block(blk).start(), c)[1], 0)
        lax.fori_loop(first_tail, n_blocks, lambda blk, c: (tail_block(blk).wait(), c)[1], 0)

    def send(tile, b, enabled=None):
        _segment_copies(
            tile, cnt_ref, loc_ref, glb_ref,
            lambda loc, glb, size: pltpu.make_async_copy(
                _slab_at(sorted_refs[b], loc, size), _slab_at(xpad_ref, glb, size), sem.at[b]),
            enabled)

    def step(buf):
        pl.when(i >= 2)(lambda: drain(buf))

        xn = xn_ref[...]
        for c in range(N_SLOTS // PCHUNK):
            pos = lax.broadcasted_iota(jnp.int32, (PCHUNK, TM), 0) + c * PCHUNK
            hit = pos == slot_ref[0:1, :]
            for k in range(1, TOP_K):
                hit = hit | (pos == slot_ref[k:k + 1, :])
            rows = jnp.dot(hit.astype(F32).astype(BF16), xn, preferred_element_type=F32)
            for sub in range(ROW_SUB):
                sorted_refs[buf][pl.ds(c * PCHUNK * ROW_SUB + sub, PCHUNK, stride=ROW_SUB), :] = (
                    rows[:, sub * LANES:(sub + 1) * LANES])

        send(i, buf)

        @pl.when(i == n - 1)
        def _():
            pl.when(i >= 1)(lambda: drain(1 - buf))
            drain(buf)

    pl.when(i % 2 == 0)(lambda: step(0))
    pl.when(i % 2 == 1)(lambda: step(1))


def _dispatch(tcnt, tloc, tglb, pad_ends, pad_at, pad_n, xn, slots, n_rows):
    t, d = xn.shape
    return pl.pallas_call(
        _dispatch_kernel,
        grid_spec=pltpu.PrefetchScalarGridSpec(
            num_scalar_prefetch=6,
            grid=(t // TM,),
            in_specs=[pl.BlockSpec((TM, d), lambda i, *_: (i, 0)),
                      pl.BlockSpec((TOP_K, TM), lambda i, *_: (0, i))],
            out_specs=pl.BlockSpec(memory_space=pl.ANY),
            scratch_shapes=[pltpu.VMEM((N_SLOTS * ROW_SUB, LANES), F32),
                            pltpu.VMEM((N_SLOTS * ROW_SUB, LANES), F32),
                            pltpu.VMEM((TME * ROW_SUB, LANES), F32),
                            pltpu.SemaphoreType.DMA((2,)), pltpu.SemaphoreType.DMA],
        ),
        out_shape=jax.ShapeDtypeStruct((n_rows * ROW_SUB, LANES), F32),
        compiler_params=pltpu.CompilerParams(
            dimension_semantics=("arbitrary",), vmem_limit_bytes=VMEM_LIMIT),
        name="dispatch",
    )(tcnt, tloc, tglb, pad_ends, pad_at, pad_n, xn, slots)


def _expert_kernel(be_ref, nvb_ref, run_ref, rexp_ref, nrun_ref, x_ref, wi_hbm, bi_ref, wo_hbm, bo_ref,
                   y_ref, wi32_ref, wo32_ref, wib_ref, wob_ref, wsem):
    i = pl.program_id(0)
    run = run_ref[i]
    buf = run % 2

    def fetch(r, b):
        e = rexp_ref[r]
        return (pltpu.make_async_copy(wi_hbm.at[e], wi32_ref.at[b], wsem.at[b]),
                pltpu.make_async_copy(wo_hbm.at[e], wo32_ref.at[b], wsem.at[b]))

    @pl.when(i == 0)
    def _():
        for copy in fetch(0, 0):
            copy.start()

    @pl.when((i == 0) | (run != run_ref[jnp.maximum(i - 1, 0)]))
    def _():
        for copy in fetch(run, buf):
            copy.wait()

        @pl.when(run + 1 < nrun_ref[0])
        def _():
            for copy in fetch(run + 1, 1 - buf):
                copy.start()

        wib_ref[...] = wi32_ref[buf].astype(BF16)
        wob_ref[...] = wo32_ref[buf].astype(BF16)

    @pl.when(i < nvb_ref[0])
    def _():
        x = jnp.concatenate(
            [x_ref[pl.ds(sub, TME, stride=ROW_SUB), :].astype(BF16) for sub in range(ROW_SUB)],
            axis=1)
        hdn = jnp.dot(x, wib_ref[...], preferred_element_type=F32) + bi_ref[0]
        gl = jnp.minimum(hdn[:, :D_EXPERT], SWIGLU_LIMIT)
        lin = jnp.clip(hdn[:, D_EXPERT:], -SWIGLU_LIMIT, SWIGLU_LIMIT)
        act = gl * jax.nn.sigmoid(SWIGLU_ALPHA * gl) * (lin + 1.0)
        y = jnp.dot(act.astype(BF16), wob_ref[...], preferred_element_type=F32) + bo_ref[0]
        for sub in range(ROW_SUB):
            y_ref[pl.ds(sub, TME, stride=ROW_SUB), :] = y[:, sub * LANES:(sub + 1) * LANES]

    @pl.when(i >= nvb_ref[0])
    def _():
        y_ref[...] = jnp.zeros_like(y_ref)


def _experts(block_e, nvb, block_run, run_expert, n_runs, x_pad, w_in, b_in, w_out, b_out):
    n_rows = x_pad.shape[0] // ROW_SUB
    d = D_MODEL
    nb = n_rows // TME
    rows = lambda i, be, nv, *_: (jnp.minimum(i, nv[0] - 1), 0)
    wsel = lambda i, be, *_: (be[i], 0, 0)
    return pl.pallas_call(
        _expert_kernel,
        grid_spec=pltpu.PrefetchScalarGridSpec(
            num_scalar_prefetch=5,
            grid=(nb,),
            in_specs=[
                pl.BlockSpec((TME * ROW_SUB, LANES), rows),
                pl.BlockSpec(memory_space=pl.ANY),
                pl.BlockSpec((1, 1, 2 * D_EXPERT), wsel),
                pl.BlockSpec(memory_space=pl.ANY),
                pl.BlockSpec((1, 1, d), wsel),
            ],
            out_specs=pl.BlockSpec((TME * ROW_SUB, LANES), lambda i, *_: (i, 0)),
            scratch_shapes=[pltpu.VMEM((2, d, 2 * D_EXPERT), F32), pltpu.VMEM((2, D_EXPERT, d), F32),
                            pltpu.VMEM((d, 2 * D_EXPERT), BF16), pltpu.VMEM((D_EXPERT, d), BF16),
                            pltpu.SemaphoreType.DMA((2,))],
        ),
        out_shape=jax.ShapeDtypeStruct((n_rows * ROW_SUB, LANES), F32),
        compiler_params=pltpu.CompilerParams(
            dimension_semantics=("arbitrary",), vmem_limit_bytes=VMEM_LIMIT),
        name="experts",
    )(block_e, nvb, block_run, run_expert, n_runs, x_pad, w_in, b_in, w_out, b_out)


def _combine_kernel(cnt_ref, loc_ref, glb_ref, ypad_ref, h_ref, route_ref, o_ref,
                    sorted0_ref, sorted1_ref, sem):
    i = pl.program_id(0)
    n = pl.num_programs(0)
    sorted_refs = (sorted0_ref, sorted1_ref)

    def fetch(tile, b, enabled=None):
        _segment_copies(
            tile, cnt_ref, loc_ref, glb_ref,
            lambda loc, glb, size: pltpu.make_async_copy(
                _slab_at(ypad_ref, glb, size), _slab_at(sorted_refs[b], loc, size), sem.at[b]),
            enabled)

    pl.when(i == 0)(lambda: fetch(0, 0))

    def step(buf):
        pltpu.make_async_copy(_slab(ypad_ref, 0, N_SLOTS), sorted_refs[buf], sem.at[buf]).wait()
        fetch(jnp.minimum(i + 1, n - 1), 1 - buf, enabled=i + 1 < n)

        acc = h_ref[...]
        slot_cols = route_ref[:, :TOP_K].astype(jnp.int32)
        for c in range(N_SLOTS // PCHUNK):
            pos = lax.broadcasted_iota(jnp.int32, (TM, PCHUNK), 1) + c * PCHUNK
            weight = jnp.zeros((TM, PCHUNK), F32)
            for k in range(TOP_K):
                weight = weight + jnp.where(pos == slot_cols[:, k:k + 1],
                                            route_ref[:, TOP_K + k:TOP_K + k + 1], 0.0)
            y = jnp.concatenate(
                [sorted_refs[buf][pl.ds(c * PCHUNK * ROW_SUB + sub, PCHUNK, stride=ROW_SUB), :].astype(BF16)
                 for sub in range(ROW_SUB)], axis=1)
            acc = acc + jnp.dot(weight.astype(BF16), y, preferred_element_type=F32)
        o_ref[...] = acc

    pl.when(i % 2 == 0)(lambda: step(0))
    pl.when(i % 2 == 1)(lambda: step(1))


def _combine(tcnt, tloc, tglb, y_pad, h, route):
    t, d = h.shape
    tile = lambda i, *_: (i, 0)
    return pl.pallas_call(
        _combine_kernel,
        grid_spec=pltpu.PrefetchScalarGridSpec(
            num_scalar_prefetch=3,
            grid=(t // TM,),
            in_specs=[
                pl.BlockSpec(memory_space=pl.ANY),
                pl.BlockSpec((TM, d), tile),
                pl.BlockSpec((TM, 2 * TOP_K), tile),
            ],
            out_specs=pl.BlockSpec((TM, d), tile),
            scratch_shapes=[pltpu.VMEM((N_SLOTS * ROW_SUB, LANES), F32),
                            pltpu.VMEM((N_SLOTS * ROW_SUB, LANES), F32),
                            pltpu.SemaphoreType.DMA((2,))],
        ),
        out_shape=jax.ShapeDtypeStruct((t, d), F32),
        compiler_params=pltpu.CompilerParams(
            dimension_semantics=("arbitrary",), vmem_limit_bytes=VMEM_LIMIT),
        name="combine",
    )(tcnt, tloc, tglb, y_pad, h, route)


def _rel_vector(rel_bias):
    n_heads = rel_bias.shape[0]
    far = LEFT - REL_CLIP
    falling = rel_bias[:, ::-1][:, 1:]
    n_fall = min(KWIN - 1 - far, 2 * REL_CLIP)
    vec = jnp.concatenate(
        [jnp.broadcast_to(rel_bias[:, -1:], (n_heads, far + 1)), falling[:, :n_fall],
         jnp.broadcast_to(rel_bias[:, :1], (n_heads, KWIN - 1 - far - n_fall)),
         jnp.broadcast_to(rel_bias[:, -1:], (n_heads, BIAS_LANES - KWIN))], axis=1)
    return vec.astype(F32) * LOG2_E


def kernel(x, norm_mix_g, w_in_proj, q_norm_g, k_norm_g, rel_bias, gmlp_v_norm_g, gmlp_w_s, gmlp_b_s,
           att_out_norm_g, gmlp_out_norm_g, w_out_proj, norm_moe_g, w_router, b_router,
           w_expert_in, b_expert_in, w_expert_out, b_expert_out):
    b, s, d = x.shape
    t = b * s

    head_of = jnp.arange(MXU_DIM) // HEAD_DIM
    block_diag = (head_of[:, None] == head_of[None, :]).astype(BF16)
    q, kt, v, gm = _inproj(
        x, norm_mix_g.reshape(1, d), w_in_proj.astype(BF16),
        jnp.tile(q_norm_g, N_HEADS).reshape(1, ATT_W), jnp.tile(k_norm_g, N_HEADS).reshape(1, ATT_W),
        block_diag, gmlp_v_norm_g, gmlp_w_s, gmlp_b_s.T, gmlp_out_norm_g.reshape(1, G_W))
    att = _attention(q, kt, v, _rel_vector(rel_bias), att_out_norm_g.reshape(1, ATT_W))

    wr_t = w_router.T
    wr_hi = wr_t.astype(BF16)
    wr_lo = (wr_t - wr_hi.astype(F32)).astype(BF16)
    h, xn, slots, route, tile_cnt = _outproj(
        att.reshape(t, ATT_W), gm.reshape(t, G_W), x.reshape(t, d), w_out_proj.astype(BF16),
        norm_moe_g.reshape(1, d), jnp.concatenate([wr_hi, wr_lo], axis=0), b_router.reshape(N_EXPERTS, 1))

    tcnt = tile_cnt[:, :, 0]
    counts = jnp.sum(tcnt, axis=0)
    padded = ((counts + TME - 1) // TME) * TME
    pad_ends = jnp.cumsum(padded).astype(jnp.int32)
    pad_starts = pad_ends - padded
    tloc = jnp.cumsum(tcnt, axis=1) - tcnt
    tglb = pad_starts[None, :] + jnp.cumsum(tcnt, axis=0) - tcnt
    n_blocks = (t * TOP_K) // TME + N_EXPERTS
    nvb = pad_ends[-1] // TME
    block_start = jnp.arange(n_blocks, dtype=jnp.int32) * TME
    block_e = jnp.minimum(jnp.sum(pad_ends[None, :] <= block_start[:, None], axis=1), N_EXPERTS - 1)
    block_e = jnp.where(jnp.arange(n_blocks) < nvb, block_e, block_e[nvb - 1]).astype(jnp.int32)
    runs = tuple((a * ROW_SUB).reshape(-1).astype(jnp.int32) for a in (tcnt, tloc, tglb))
    present = padded > 0
    run_of_expert = jnp.cumsum(present) - 1
    experts_iota = jnp.arange(N_EXPERTS)
    block_run = jnp.sum(jnp.where(block_e[:, None] == experts_iota[None, :], run_of_expert[None, :], 0),
                        axis=1).astype(jnp.int32)
    run_expert = jnp.sum(jnp.where(present[None, :] & (run_of_expert[None, :] == experts_iota[:, None]),
                                   experts_iota[None, :], 0), axis=1).astype(jnp.int32)
    n_runs = jnp.sum(present).astype(jnp.int32).reshape(1)

    pad_at = ((pad_starts + counts) * ROW_SUB).astype(jnp.int32)
    pad_n = ((padded - counts) * ROW_SUB).astype(jnp.int32)
    x_pad = _dispatch(*runs, pad_ends, pad_at, pad_n, xn, slots, n_blocks * TME)
    y_pad = _experts(block_e, nvb.reshape(1), block_run, run_expert, n_runs, x_pad, w_expert_in,
                     b_expert_in.reshape(N_EXPERTS, 1, 2 * D_EXPERT), w_expert_out,
                     b_expert_out.reshape(N_EXPERTS, 1, d))
    out = _combine(*runs, y_pad, h, route)
    return out.reshape(b, s, d)
```

```python
import jax
import jax.numpy as jnp
from jax import lax
from jax.experimental import pallas as pl
from jax.experimental.pallas import tpu as pltpu

D_MODEL = 1024
CHUNK = 64
LEFT = 8 * CHUNK
N_HEADS = 8
HEAD_DIM = 64
ATT_W = 512
REL_CLIP = 128
G_GROUPS = 4
G_DIM = 128
G_BLOCK = 128
G_W = 512
N_EXPERTS = 32
TOP_K = 4
D_EXPERT = 1024
SWIGLU_LIMIT = 7.0
SWIGLU_ALPHA = 1.702
EPS = 1e-6
NEG_INF = -1e30
LOG2_E = 1.4426950408889634

TM = 512
TQ = 256
KWIN = TQ + LEFT
BIAS_LANES = 1024
HALF_ROWS = TQ // 2
HALF_COLS = HALF_ROWS + LEFT
TME = 512
N_SLOTS = TM * TOP_K
PCHUNK = 256
SEG_BITS = TM.bit_length()
VMEM_LIMIT = 56 * 1024 * 1024
LANES = 128
MXU_DIM = 256
ROW_SUB = D_MODEL // LANES

F32 = jnp.float32
BF16 = jnp.bfloat16


def _rms(x, gain):
    ms = jnp.mean(x * x, axis=-1, keepdims=True)
    return x * lax.rsqrt(ms + EPS) * gain


def _gelu(x):
    return 0.5 * x * (1.0 + lax.erf(x * (2.0 ** -0.5)))


def _slab(ref, row, n_rows=1):
    return ref.at[pl.ds(pl.multiple_of(row * ROW_SUB, ROW_SUB), n_rows * ROW_SUB)]


def _slab_at(ref, offset, n_rows):
    return ref.at[pl.ds(pl.multiple_of(offset, ROW_SUB), n_rows * ROW_SUB)]


def _segment_copies(tile, cnt_ref, loc_ref, glb_ref, make_copy, enabled=None):
    def per_expert(e, c):
        n = cnt_ref[tile * N_EXPERTS + e]
        if enabled is not None:
            n = jnp.where(enabled, n, 0)
        loc = loc_ref[tile * N_EXPERTS + e]
        glb = glb_ref[tile * N_EXPERTS + e]
        for bit in range(SEG_BITS):
            size = 1 << bit
            off = n & ~(2 * size * ROW_SUB - 1)

            @pl.when((n & (size * ROW_SUB)) != 0)
            def _():
                make_copy(loc + off, glb + off, size).start()
        return c

    if enabled is None:
        lax.fori_loop(0, N_EXPERTS, per_expert, 0, unroll=2)
    else:
        for e in range(N_EXPERTS):
            per_expert(e, 0)


def _inproj_kernel(x_ref, g_ref, w_ref, gq_ref, gk_ref, bd_ref, gvn_ref, ws_ref, bst_ref, gout_ref,
                   q_ref, kt_ref, v_ref, gm_ref):
    j = pl.program_id(1)

    @pl.when(j == 0)
    def _():
        kt_ref[...] = jnp.zeros_like(kt_ref)
        v_ref[...] = jnp.zeros_like(v_ref)

    @pl.when(j > 0)
    def _():
        z = _rms(x_ref[0], g_ref[...]).astype(BF16)

        def proj(c0, width):
            return jnp.dot(z, w_ref[:, c0:c0 + width], preferred_element_type=F32)

        def head_norm(t, gain):
            sq = (t * t).astype(BF16)
            half = bd_ref.shape[0]
            ssum = jnp.concatenate(
                [jnp.dot(sq[:, c:c + half], bd_ref[...], preferred_element_type=F32)
                 for c in range(0, ATT_W, half)], axis=1)
            return t * lax.rsqrt(ssum * (1.0 / HEAD_DIM) + EPS) * gain

        q = head_norm(proj(0, ATT_W), gq_ref[...]) * (HEAD_DIM ** -0.5 * LOG2_E)
        q_ref[0] = q.astype(BF16)
        k = head_norm(proj(ATT_W, ATT_W), gk_ref[...])
        kt_ref[0] = k.T.astype(BF16)
        v_ref[0] = proj(2 * ATT_W, ATT_W).astype(BF16)

        gu = _gelu(proj(3 * ATT_W, G_W))
        gv = _gelu(proj(3 * ATT_W + G_W, G_W))
        row = lax.broadcasted_iota(jnp.int32, (G_BLOCK, G_BLOCK), 0) // CHUNK
        col = lax.broadcasted_iota(jnp.int32, (G_BLOCK, G_BLOCK), 1) // CHUNK
        tri = row >= col
        cols = []
        for g in range(G_GROUPS):
            sl = slice(g * G_DIM, (g + 1) * G_DIM)
            vn = _rms(gv[:, sl], gvn_ref[g:g + 1, :]).astype(BF16)
            wm = jnp.where(tri, ws_ref[g], 0.0).astype(BF16)
            blocks = []
            for n in range(TM // G_BLOCK):
                rs = slice(n * G_BLOCK, (n + 1) * G_BLOCK)
                gate = jnp.dot(wm, vn[rs], preferred_element_type=F32) + bst_ref[:, g:g + 1]
                blocks.append(gu[rs, sl] * gate)
            cols.append(jnp.concatenate(blocks, axis=0))
        gm = jnp.concatenate(cols, axis=1)
        gm_ref[0] = _rms(gm, gout_ref[...]).astype(BF16)


def _inproj(x, g_mix, w_in, gq, gk, bd, gvn, ws, bst, gout):
    b, s, d = x.shape
    nt = s // TM
    xmap = lambda bi, j: (bi, jnp.maximum(j - 1, 0), 0)
    const2 = lambda bi, j: (0, 0)
    return pl.pallas_call(
        _inproj_kernel,
        grid=(b, nt + 1),
        in_specs=[
            pl.BlockSpec((1, TM, d), xmap),
            pl.BlockSpec((1, d), const2),
            pl.BlockSpec(w_in.shape, const2),
            pl.BlockSpec((1, ATT_W), const2),
            pl.BlockSpec((1, ATT_W), const2),
            pl.BlockSpec(bd.shape, const2),
            pl.BlockSpec((G_GROUPS, G_DIM), const2),
            pl.BlockSpec((G_GROUPS, G_BLOCK, G_BLOCK), lambda bi, j: (0, 0, 0)),
            pl.BlockSpec((G_BLOCK, G_GROUPS), const2),
            pl.BlockSpec((1, G_W), const2),
        ],
        out_specs=[
            pl.BlockSpec((1, TM, ATT_W), xmap),
            pl.BlockSpec((1, ATT_W, TM), lambda bi, j: (bi, 0, j)),
            pl.BlockSpec((1, TM, ATT_W), lambda bi, j: (bi, j, 0)),
            pl.BlockSpec((1, TM, G_W), xmap),
        ],
        out_shape=[
            jax.ShapeDtypeStruct((b, s, ATT_W), BF16),
            jax.ShapeDtypeStruct((b, ATT_W, s + LEFT), BF16),
            jax.ShapeDtypeStruct((b, s + LEFT, ATT_W), BF16),
            jax.ShapeDtypeStruct((b, s, G_W), BF16),
        ],
        compiler_params=pltpu.CompilerParams(
            dimension_semantics=("arbitrary", "arbitrary"), vmem_limit_bytes=VMEM_LIMIT),
        name="inproj",
    )(x, g_mix, w_in, gq, gk, bd, gvn, ws, bst, gout)


def _attn_kernel(q_ref, kt_hbm, v_hbm, relvec_ref, gout_ref, o_ref, bias_ref, kt_ref, v_ref, kv_sem):
    b = pl.program_id(0)
    i = pl.program_id(1)
    buf = b % 2
    q0 = pl.multiple_of(i * TQ, TQ)
    win = pl.ds(q0, KWIN)
    lane = lax.broadcasted_iota(jnp.int32, (1, 256), 1)

    def fetch(batch, slot):
        return (pltpu.make_async_copy(kt_hbm.at[batch], kt_ref.at[slot], kv_sem.at[slot]),
                pltpu.make_async_copy(v_hbm.at[batch], v_ref.at[slot], kv_sem.at[slot]))

    @pl.when(i == 0)
    def _():
        @pl.when(b == 0)
        def _():
            for copy in fetch(0, 0):
                copy.start()

        for copy in fetch(b, buf):
            copy.wait()

        @pl.when(b + 1 < pl.num_programs(0))
        def _():
            for copy in fetch(b + 1, 1 - buf):
                copy.start()

    @pl.when((pl.program_id(0) == 0) & (i == 0))
    def _():
        r = lax.broadcasted_iota(jnp.int32, (TQ, KWIN), 0)
        j = lax.broadcasted_iota(jnp.int32, (TQ, KWIN), 1)
        lo = (r // CHUNK) * CHUNK
        in_band = (j >= lo) & (j < lo + LEFT + CHUNK)
        for h in range(N_HEADS):
            shifted = pltpu.roll(jnp.broadcast_to(relvec_ref[h:h + 1, :], (TQ, BIAS_LANES)), 0, 1,
                                 stride=1, stride_axis=0)
            bias_ref[h] = jnp.where(in_band, shifted[:, :KWIN], NEG_INF)

    def tile(has_left_padding):
        if has_left_padding:
            colpos = lax.broadcasted_iota(jnp.int32, (1, KWIN), 1)
            pad_bias = jnp.where(colpos >= LEFT - q0, 0.0, NEG_INF).astype(F32)
        outs = []
        for g in range(2):
            gs = slice(g * 256, (g + 1) * 256)
            qg = q_ref[0, :, gs]
            ktg = kt_ref[buf, gs, win]
            vg = v_ref[buf, win, gs]
            acc = jnp.zeros((TQ, 256), F32)
            for hh in range(4):
                h = 4 * g + hh
                hmask = (lane >= hh * HEAD_DIM) & (lane < (hh + 1) * HEAD_DIM)
                qh = jnp.where(hmask, qg, jnp.zeros_like(qg))
                raw = jnp.dot(qh, ktg, preferred_element_type=F32)
                probs, invs = [], []
                for rows, cols in ((slice(0, HALF_ROWS), slice(0, HALF_COLS)),
                                   (slice(HALF_ROWS, TQ), slice(KWIN - HALF_COLS, KWIN))):
                    s = raw[rows, cols] + bias_ref[h, rows, cols]
                    if has_left_padding:
                        s = s + pad_bias[:, cols]
                    m = jnp.max(s, axis=-1, keepdims=True)
                    e = jnp.exp2(s - m)
                    invs.append(1.0 / jnp.sum(e, axis=-1, keepdims=True))
                    rest = jnp.zeros((HALF_ROWS, KWIN - HALF_COLS), BF16)
                    parts = [e.astype(BF16), rest] if cols.start == 0 else [rest, e.astype(BF16)]
                    probs.append(jnp.concatenate(parts, axis=1))
                p = jnp.concatenate(probs, axis=0)
                pv = jnp.dot(p, vg, preferred_element_type=F32)
                acc = jnp.where(hmask, pv * jnp.concatenate(invs, axis=0), acc)
            outs.append(acc)
        att = jnp.concatenate(outs, axis=1)
        o_ref[0] = _rms(att, gout_ref[...]).astype(BF16)

    pl.when(q0 < LEFT)(lambda: tile(True))
    pl.when(q0 >= LEFT)(lambda: tile(False))


def _attention(q, kt, v, relvec, gout):
    b, s, _ = q.shape
    return pl.pallas_call(
        _attn_kernel,
        grid=(b, s // TQ),
        in_specs=[
            pl.BlockSpec((1, TQ, ATT_W), lambda bi, i: (bi, i, 0)),
            pl.BlockSpec(memory_space=pl.ANY),
            pl.BlockSpec(memory_space=pl.ANY),
            pl.BlockSpec(relvec.shape, lambda bi, i: (0, 0)),
            pl.BlockSpec((1, ATT_W), lambda bi, i: (0, 0)),
        ],
        out_specs=pl.BlockSpec((1, TQ, ATT_W), lambda bi, i: (bi, i, 0)),
        out_shape=jax.ShapeDtypeStruct((b, s, ATT_W), BF16),
        scratch_shapes=[pltpu.VMEM((N_HEADS, TQ, KWIN), F32),
                        pltpu.VMEM((2, ATT_W, s + LEFT), BF16), pltpu.VMEM((2, s + LEFT, ATT_W), BF16),
                        pltpu.SemaphoreType.DMA((2,))],
        compiler_params=pltpu.CompilerParams(
            dimension_semantics=("arbitrary", "arbitrary"), vmem_limit_bytes=VMEM_LIMIT),
        name="attention",
    )(q, kt, v, relvec, gout)


def _outproj_kernel(a_ref, gm_ref, x_ref, wo_ref, gmoe_ref, wr_ref, br_ref,
                    h_ref, xn_ref, slot_ref, route_ref, cnt_ref):
    h = (x_ref[...]
         + jnp.dot(a_ref[...], wo_ref[:ATT_W, :], preferred_element_type=F32)
         + jnp.dot(gm_ref[...], wo_ref[ATT_W:, :], preferred_element_type=F32))
    h_ref[...] = h
    xn = _rms(h, gmoe_ref[...])
    xhi = xn.astype(BF16)
    xlo = (xn - xhi.astype(F32)).astype(BF16)
    xn_ref[...] = xhi

    nt = (((1,), (1,)), ((), ()))
    by_xhi = lax.dot_general(wr_ref[...], xhi, nt, preferred_element_type=F32)
    by_xlo = lax.dot_general(wr_ref[:N_EXPERTS, :], xlo, nt, preferred_element_type=F32)
    logits = by_xhi[:N_EXPERTS] + by_xhi[N_EXPERTS:] + by_xlo + br_ref[...]
    eidx = lax.broadcasted_iota(jnp.int32, (N_EXPERTS, TM), 0)
    vals, idxs = [], []
    cur = logits
    for _ in range(TOP_K):
        m = jnp.max(cur, axis=0, keepdims=True)
        ik = jnp.min(jnp.where(cur == m, eidx, N_EXPERTS), axis=0, keepdims=True)
        vals.append(m)
        idxs.append(ik)
        cur = jnp.where(eidx == ik, -jnp.inf, cur)
    exps = [jnp.exp(v - vals[0]) for v in vals]
    tot = exps[0] + exps[1] + exps[2] + exps[3]
    gates = [e / tot for e in exps]

    onehots = [eidx == ik for ik in idxs]
    member = sum(oh.astype(F32) for oh in onehots).astype(BF16)

    earlier_tok = (lax.broadcasted_iota(jnp.int32, (TM, TM), 0)
                   < lax.broadcasted_iota(jnp.int32, (TM, TM), 1)).astype(BF16)
    earlier_exp = (lax.broadcasted_iota(jnp.int32, (N_EXPERTS, N_EXPERTS), 1)
                   < lax.broadcasted_iota(jnp.int32, (N_EXPERTS, N_EXPERTS), 0)).astype(BF16)
    rank = jnp.dot(member, earlier_tok, preferred_element_type=F32)
    lower = jnp.sum(jnp.dot(earlier_exp, member, preferred_element_type=F32), axis=1, keepdims=True)
    base = rank + lower
    slots = [jnp.sum(jnp.where(oh, base, 0.0), axis=0, keepdims=True) for oh in onehots]
    slot_ref[...] = jnp.concatenate(slots, axis=0).astype(jnp.int32)
    route_ref[...] = jnp.concatenate(slots + gates, axis=0).T
    cnt_ref[0] = jnp.sum(member.astype(F32), axis=1, keepdims=True).astype(jnp.int32)


def _outproj(att, gm, x2, w_out, g_moe, wr_parts, br):
    t, d = x2.shape
    tile = lambda i: (i, 0)
    const = lambda i: (0, 0)
    lanes = lambda i: (0, i)
    return pl.pallas_call(
        _outproj_kernel,
        grid=(t // TM,),
        in_specs=[
            pl.BlockSpec((TM, ATT_W), tile),
            pl.BlockSpec((TM, G_W), tile),
            pl.BlockSpec((TM, d), tile),
            pl.BlockSpec(w_out.shape, const),
            pl.BlockSpec((1, d), const),
            pl.BlockSpec((2 * N_EXPERTS, d), const),
            pl.BlockSpec((N_EXPERTS, 1), const),
        ],
        out_specs=[
            pl.BlockSpec((TM, d), tile),
            pl.BlockSpec((TM, d), tile),
            pl.BlockSpec((TOP_K, TM), lanes),
            pl.BlockSpec((TM, 2 * TOP_K), tile),
            pl.BlockSpec((1, N_EXPERTS, 1), lambda i: (i, 0, 0)),
        ],
        out_shape=[
            jax.ShapeDtypeStruct((t, d), F32),
            jax.ShapeDtypeStruct((t, d), BF16),
            jax.ShapeDtypeStruct((TOP_K, t), jnp.int32),
            jax.ShapeDtypeStruct((t, 2 * TOP_K), F32),
            jax.ShapeDtypeStruct((t // TM, N_EXPERTS, 1), jnp.int32),
        ],
        compiler_params=pltpu.CompilerParams(
            dimension_semantics=("arbitrary",), vmem_limit_bytes=VMEM_LIMIT),
        name="outproj_router",
    )(att, gm, x2, w_out, g_moe, wr_parts, br)


def _dispatch_kernel(cnt_ref, loc_ref, glb_ref, pend_ref, padat_ref, padn_ref, xn_ref, slot_ref, xpad_ref,
                     sorted0_ref, sorted1_ref, zero_ref, sem, zsem):
    i = pl.program_id(0)
    n = pl.num_programs(0)
    sorted_refs = (sorted0_ref, sorted1_ref)

    def drain(b):
        pltpu.make_async_copy(sorted_refs[b], _slab(xpad_ref, 0, N_SLOTS), sem.at[b]).wait()

    @pl.when(i == 0)
    def _():
        zero_ref[...] = jnp.zeros_like(zero_ref)

        def padding(wait):
            def per_expert(e, c):
                n = padn_ref[e]
                for bit in range(TME.bit_length() - 1):
                    size = 1 << bit
                    off = n & ~(2 * size * ROW_SUB - 1)

                    @pl.when((n & (size * ROW_SUB)) != 0)
                    def _():
                        copy = pltpu.make_async_copy(
                            _slab(zero_ref, 0, size), _slab_at(xpad_ref, padat_ref[e] + off, size), zsem)
                        copy.wait() if wait else copy.start()
                return c

            lax.fori_loop(0, N_EXPERTS, per_expert, 0)

        padding(wait=False)
        padding(wait=True)

        def tail_block(blk):
            return pltpu.make_async_copy(zero_ref, _slab(xpad_ref, blk * TME, TME), zsem)

        first_tail = pend_ref[N_EXPERTS - 1] // TME
        n_blocks = xpad_ref.shape[0] // (TME * ROW_SUB)
        lax.fori_loop(first_tail, n_blocks, lambda blk, c: (tail_block(blk).start(), c)[1], 0)
        lax.fori_loop(first_tail, n_blocks, lambda blk, c: (tail_block(blk).wait(), c)[1], 0)

    def send(tile, b, enabled=None):
        _segment_copies(
            tile, cnt_ref, loc_ref, glb_ref,
            lambda loc, glb, size: pltpu.make_async_copy(
                _slab_at(sorted_refs[b], loc, size), _slab_at(xpad_ref, glb, size), sem.at[b]),
            enabled)

    def step(buf):
        pl.when(i >= 2)(lambda: drain(buf))

        xn = xn_ref[...]
        for c in range(N_SLOTS // PCHUNK):
            pos = lax.broadcasted_iota(jnp.int32, (PCHUNK, TM), 0) + c * PCHUNK
            hit = pos == slot_ref[0:1, :]
            for k in range(1, TOP_K):
                hit = hit | (pos == slot_ref[k:k + 1, :])
            rows = jnp.dot(hit.astype(F32).astype(BF16), xn, preferred_element_type=F32)
            for sub in range(ROW_SUB):
                sorted_refs[buf][pl.ds(c * PCHUNK * ROW_SUB + sub, PCHUNK, stride=ROW_SUB), :] = (
                    rows[:, sub * LANES:(sub + 1) * LANES])

        send(i, buf)

        @pl.when(i == n - 1)
        def _():
            pl.when(i >= 1)(lambda: drain(1 - buf))
            drain(buf)

    pl.when(i % 2 == 0)(lambda: step(0))
    pl.when(i % 2 == 1)(lambda: step(1))


def _dispatch(tcnt, tloc, tglb, pad_ends, pad_at, pad_n, xn, slots, n_rows):
    t, d = xn.shape
    return pl.pallas_call(
        _dispatch_kernel,
        grid_spec=pltpu.PrefetchScalarGridSpec(
            num_scalar_prefetch=6,
            grid=(t // TM,),
            in_specs=[pl.BlockSpec((TM, d), lambda i, *_: (i, 0)),
                      pl.BlockSpec((TOP_K, TM), lambda i, *_: (0, i))],
            out_specs=pl.BlockSpec(memory_space=pl.ANY),
            scratch_shapes=[pltpu.VMEM((N_SLOTS * ROW_SUB, LANES), F32),
                            pltpu.VMEM((N_SLOTS * ROW_SUB, LANES), F32),
                            pltpu.VMEM((TME * ROW_SUB, LANES), F32),
                            pltpu.SemaphoreType.DMA((2,)), pltpu.SemaphoreType.DMA],
        ),
        out_shape=jax.ShapeDtypeStruct((n_rows * ROW_SUB, LANES), F32),
        compiler_params=pltpu.CompilerParams(
            dimension_semantics=("arbitrary",), vmem_limit_bytes=VMEM_LIMIT),
        name="dispatch",
    )(tcnt, tloc, tglb, pad_ends, pad_at, pad_n, xn, slots)


def _expert_kernel(be_ref, nvb_ref, run_ref, rexp_ref, nrun_ref, x_ref, wi_hbm, bi_ref, wo_hbm, bo_ref,
                   y_ref, wi32_ref, wo32_ref, wib_ref, wob_ref, wsem):
    i = pl.program_id(0)
    run = run_ref[i]
    buf = run % 2

    def fetch(r, b):
        e = rexp_ref[r]
        return (pltpu.make_async_copy(wi_hbm.at[e], wi32_ref.at[b], wsem.at[b]),
                pltpu.make_async_copy(wo_hbm.at[e], wo32_ref.at[b], wsem.at[b]))

    @pl.when(i == 0)
    def _():
        for copy in fetch(0, 0):
            copy.start()

    @pl.when((i == 0) | (run != run_ref[jnp.maximum(i - 1, 0)]))
    def _():
        for copy in fetch(run, buf):
            copy.wait()

        @pl.when(run + 1 < nrun_ref[0])
        def _():
            for copy in fetch(run + 1, 1 - buf):
                copy.start()

        wib_ref[...] = wi32_ref[buf].astype(BF16)
        wob_ref[...] = wo32_ref[buf].astype(BF16)

    @pl.when(i < nvb_ref[0])
    def _():
        x = jnp.concatenate(
            [x_ref[pl.ds(sub, TME, stride=ROW_SUB), :].astype(BF16) for sub in range(ROW_SUB)],
            axis=1)
        hdn = jnp.dot(x, wib_ref[...], preferred_element_type=F32) + bi_ref[0]
        gl = jnp.minimum(hdn[:, :D_EXPERT], SWIGLU_LIMIT)
        lin = jnp.clip(hdn[:, D_EXPERT:], -SWIGLU_LIMIT, SWIGLU_LIMIT)
        act = gl * jax.nn.sigmoid(SWIGLU_ALPHA * gl) * (lin + 1.0)
        y = jnp.dot(act.astype(BF16), wob_ref[...], preferred_element_type=F32) + bo_ref[0]
        for sub in range(ROW_SUB):
            y_ref[pl.ds(sub, TME, stride=ROW_SUB), :] = y[:, sub * LANES:(sub + 1) * LANES]

    @pl.when(i >= nvb_ref[0])
    def _():
        y_ref[...] = jnp.zeros_like(y_ref)


def _experts(block_e, nvb, block_run, run_expert, n_runs, x_pad, w_in, b_in, w_out, b_out):
    n_rows = x_pad.shape[0] // ROW_SUB
    d = D_MODEL
    nb = n_rows // TME
    rows = lambda i, be, nv, *_: (jnp.minimum(i, nv[0] - 1), 0)
    wsel = lambda i, be, *_: (be[i], 0, 0)
    return pl.pallas_call(
        _expert_kernel,
        grid_spec=pltpu.PrefetchScalarGridSpec(
            num_scalar_prefetch=5,
            grid=(nb,),
            in_specs=[
                pl.BlockSpec((TME * ROW_SUB, LANES), rows),
                pl.BlockSpec(memory_space=pl.ANY),
                pl.BlockSpec((1, 1, 2 * D_EXPERT), wsel),
                pl.BlockSpec(memory_space=pl.ANY),
                pl.BlockSpec((1, 1, d), wsel),
            ],
            out_specs=pl.BlockSpec((TME * ROW_SUB, LANES), lambda i, *_: (i, 0)),
            scratch_shapes=[pltpu.VMEM((2, d, 2 * D_EXPERT), F32), pltpu.VMEM((2, D_EXPERT, d), F32),
                            pltpu.VMEM((d, 2 * D_EXPERT), BF16), pltpu.VMEM((D_EXPERT, d), BF16),
                            pltpu.SemaphoreType.DMA((2,))],
        ),
        out_shape=jax.ShapeDtypeStruct((n_rows * ROW_SUB, LANES), F32),
        compiler_params=pltpu.CompilerParams(
            dimension_semantics=("arbitrary",), vmem_limit_bytes=VMEM_LIMIT),
        name="experts",
    )(block_e, nvb, block_run, run_expert, n_runs, x_pad, w_in, b_in, w_out, b_out)


def _combine_kernel(cnt_ref, loc_ref, glb_ref, ypad_ref, h_ref, route_ref, o_ref,
                    sorted0_ref, sorted1_ref, sem):
    i = pl.program_id(0)
    n = pl.num_programs(0)
    sorted_refs = (sorted0_ref, sorted1_ref)

    def fetch(tile, b, enabled=None):
        _segment_copies(
            tile, cnt_ref, loc_ref, glb_ref,
            lambda loc, glb, size: pltpu.make_async_copy(
                _slab_at(ypad_ref, glb, size), _slab_at(sorted_refs[b], loc, size), sem.at[b]),
            enabled)

    pl.when(i == 0)(lambda: fetch(0, 0))

    def step(buf):
        pltpu.make_async_copy(_slab(ypad_ref, 0, N_SLOTS), sorted_refs[buf], sem.at[buf]).wait()
        fetch(jnp.minimum(i + 1, n - 1), 1 - buf, enabled=i + 1 < n)

        acc = h_ref[...]
        slot_cols = route_ref[:, :TOP_K].astype(jnp.int32)
        for c in range(N_SLOTS // PCHUNK):
            pos = lax.broadcasted_iota(jnp.int32, (TM, PCHUNK), 1) + c * PCHUNK
            weight = jnp.zeros((TM, PCHUNK), F32)
            for k in range(TOP_K):
                weight = weight + jnp.where(pos == slot_cols[:, k:k + 1],
                                            route_ref[:, TOP_K + k:TOP_K + k + 1], 0.0)
            y = jnp.concatenate(
                [sorted_refs[buf][pl.ds(c * PCHUNK * ROW_SUB + sub, PCHUNK, stride=ROW_SUB), :].astype(BF16)
                 for sub in range(ROW_SUB)], axis=1)
            acc = acc + jnp.dot(weight.astype(BF16), y, preferred_element_type=F32)
        o_ref[...] = acc

    pl.when(i % 2 == 0)(lambda: step(0))
    pl.when(i % 2 == 1)(lambda: step(1))


def _combine(tcnt, tloc, tglb, y_pad, h, route):
    t, d = h.shape
    tile = lambda i, *_: (i, 0)
    return pl.pallas_call(
        _combine_kernel,
        grid_spec=pltpu.PrefetchScalarGridSpec(
            num_scalar_prefetch=3,
            grid=(t // TM,),
            in_specs=[
                pl.BlockSpec(memory_space=pl.ANY),
                pl.BlockSpec((TM, d), tile),
                pl.BlockSpec((TM, 2 * TOP_K), tile),
            ],
            out_specs=pl.BlockSpec((TM, d), tile),
            scratch_shapes=[pltpu.VMEM((N_SLOTS * ROW_SUB, LANES), F32),
                            pltpu.VMEM((N_SLOTS * ROW_SUB, LANES), F32),
                            pltpu.SemaphoreType.DMA((2,))],
        ),
        out_shape=jax.ShapeDtypeStruct((t, d), F32),
        compiler_params=pltpu.CompilerParams(
            dimension_semantics=("arbitrary",), vmem_limit_bytes=VMEM_LIMIT),
        name="combine",
    )(tcnt, tloc, tglb, y_pad, h, route)


def _rel_vector(rel_bias):
    n_heads = rel_bias.shape[0]
    far = LEFT - REL_CLIP
    falling = rel_bias[:, ::-1][:, 1:]
    n_fall = min(KWIN - 1 - far, 2 * REL_CLIP)
    vec = jnp.concatenate(
        [jnp.broadcast_to(rel_bias[:, -1:], (n_heads, far + 1)), falling[:, :n_fall],
         jnp.broadcast_to(rel_bias[:, :1], (n_heads, KWIN - 1 - far - n_fall)),
         jnp.broadcast_to(rel_bias[:, -1:], (n_heads, BIAS_LANES - KWIN))], axis=1)
    return vec.astype(F32) * LOG2_E


def kernel(x, norm_mix_g, w_in_proj, q_norm_g, k_norm_g, rel_bias, gmlp_v_norm_g, gmlp_w_s, gmlp_b_s,
           att_out_norm_g, gmlp_out_norm_g, w_out_proj, norm_moe_g, w_router, b_router,
           w_expert_in, b_expert_in, w_expert_out, b_expert_out):
    b, s, d = x.shape
    t = b * s

    head_of = jnp.arange(MXU_DIM) // HEAD_DIM
    block_diag = (head_of[:, None] == head_of[None, :]).astype(BF16)
    q, kt, v, gm = _inproj(
        x, norm_mix_g.reshape(1, d), w_in_proj.astype(BF16),
        jnp.tile(q_norm_g, N_HEADS).reshape(1, ATT_W), jnp.tile(k_norm_g, N_HEADS).reshape(1, ATT_W),
        block_diag, gmlp_v_norm_g, gmlp_w_s, gmlp_b_s.T, gmlp_out_norm_g.reshape(1, G_W))
    att = _attention(q, kt, v, _rel_vector(rel_bias), att_out_norm_g.reshape(1, ATT_W))

    wr_t = w_router.T
    wr_hi = wr_t.astype(BF16)
    wr_lo = (wr_t - wr_hi.astype(F32)).astype(BF16)
    h, xn, slots, route, tile_cnt = _outproj(
        att.reshape(t, ATT_W), gm.reshape(t, G_W), x.reshape(t, d), w_out_proj.astype(BF16),
        norm_moe_g.reshape(1, d), jnp.concatenate([wr_hi, wr_lo], axis=0), b_router.reshape(N_EXPERTS, 1))

    tcnt = tile_cnt[:, :, 0]
    counts = jnp.sum(tcnt, axis=0)
    padded = ((counts + TME - 1) // TME) * TME
    pad_ends = jnp.cumsum(padded).astype(jnp.int32)
    pad_starts = pad_ends - padded
    tloc = jnp.cumsum(tcnt, axis=1) - tcnt
    tglb = pad_starts[None, :] + jnp.cumsum(tcnt, axis=0) - tcnt
    n_blocks = (t * TOP_K) // TME + N_EXPERTS
    nvb = pad_ends[-1] // TME
    block_start = jnp.arange(n_blocks, dtype=jnp.int32) * TME
    block_e = jnp.minimum(jnp.sum(pad_ends[None, :] <= block_start[:, None], axis=1), N_EXPERTS - 1)
    block_e = jnp.where(jnp.arange(n_blocks) < nvb, block_e, block_e[nvb - 1]).astype(jnp.int32)
    runs = tuple((a * ROW_SUB).reshape(-1).astype(jnp.int32) for a in (tcnt, tloc, tglb))
    present = padded > 0
    run_of_expert = jnp.cumsum(present) - 1
    experts_iota = jnp.arange(N_EXPERTS)
    block_run = jnp.sum(jnp.where(block_e[:, None] == experts_iota[None, :], run_of_expert[None, :], 0),
                        axis=1).astype(jnp.int32)
    run_expert = jnp.sum(jnp.where(present[None, :] & (run_of_expert[None, :] == experts_iota[:, None]),
                                   experts_iota[None, :], 0), axis=1).astype(jnp.int32)
    n_runs = jnp.sum(present).astype(jnp.int32).reshape(1)

    pad_at = ((pad_starts + counts) * ROW_SUB).astype(jnp.int32)
    pad_n = ((padded - counts) * ROW_SUB).astype(jnp.int32)
    x_pad = _dispatch(*runs, pad_ends, pad_at, pad_n, xn, slots, n_blocks * TME)
    y_pad = _experts(block_e, nvb.reshape(1), block_run, run_expert, n_runs, x_pad, w_expert_in,
                     b_expert_in.reshape(N_EXPERTS, 1, 2 * D_EXPERT), w_expert_out,
                     b_expert_out.reshape(N_EXPERTS, 1, d))
    out = _combine(*runs, y_pad, h, route)
    return out.reshape(b, s, d)
```

```python
import jax
import jax.numpy as jnp
from jax import lax
from jax.experimental import pallas as pl
from jax.experimental.pallas import tpu as pltpu

D_MODEL = 1024
CHUNK = 64
LEFT = 8 * CHUNK
N_HEADS = 8
HEAD_DIM = 64
ATT_W = 512
REL_CLIP = 128
G_GROUPS = 4
G_DIM = 128
G_BLOCK = 128
G_W = 512
N_EXPERTS = 32
TOP_K = 4
D_EXPERT = 1024
SWIGLU_LIMIT = 7.0
SWIGLU_ALPHA = 1.702
EPS = 1e-6
NEG_INF = -1e30
LOG2_E = 1.4426950408889634

TM = 512
TQ = 256
KWIN = TQ + LEFT
BIAS_LANES = 1024
HALF_ROWS = TQ // 2
HALF_COLS = HALF_ROWS + LEFT
TME = 512
N_SLOTS = TM * TOP_K
PCHUNK = 256
SEG_BITS = TM.bit_length()
VMEM_LIMIT = 56 * 1024 * 1024
LANES = 128
MXU_DIM = 256
ROW_SUB = D_MODEL // LANES

F32 = jnp.float32
BF16 = jnp.bfloat16


def _rms(x, gain):
    ms = jnp.mean(x * x, axis=-1, keepdims=True)
    return x * lax.rsqrt(ms + EPS) * gain


def _gelu(x):
    return 0.5 * x * (1.0 + lax.erf(x * (2.0 ** -0.5)))


def _slab(ref, row, n_rows=1):
    return ref.at[pl.ds(pl.multiple_of(row * ROW_SUB, ROW_SUB), n_rows * ROW_SUB)]


def _slab_at(ref, offset, n_rows):
    return ref.at[pl.ds(pl.multiple_of(offset, ROW_SUB), n_rows * ROW_SUB)]


def _segment_copies(tile, cnt_ref, loc_ref, glb_ref, make_copy, enabled=None):
    def per_expert(e, c):
        n = cnt_ref[tile * N_EXPERTS + e]
        if enabled is not None:
            n = jnp.where(enabled, n, 0)
        loc = loc_ref[tile * N_EXPERTS + e]
        glb = glb_ref[tile * N_EXPERTS + e]
        for bit in range(SEG_BITS):
            size = 1 << bit
            off = n & ~(2 * size * ROW_SUB - 1)

            @pl.when((n & (size * ROW_SUB)) != 0)
            def _():
                make_copy(loc + off, glb + off, size).start()
        return c

    if enabled is None:
        lax.fori_loop(0, N_EXPERTS, per_expert, 0, unroll=2)
    else:
        for e in range(N_EXPERTS):
            per_expert(e, 0)


def _inproj_kernel(x_ref, g_ref, w_ref, gq_ref, gk_ref, bd_ref, gvn_ref, ws_ref, bst_ref, gout_ref,
                   q_ref, kt_ref, v_ref, gm_ref):
    j = pl.program_id(1)

    @pl.when(j == 0)
    def _():
        kt_ref[...] = jnp.zeros_like(kt_ref)
        v_ref[...] = jnp.zeros_like(v_ref)

    @pl.when(j > 0)
    def _():
        z = _rms(x_ref[0], g_ref[...]).astype(BF16)

        def proj(c0, width):
            return jnp.dot(z, w_ref[:, c0:c0 + width], preferred_element_type=F32)

        def head_norm(t, gain):
            sq = (t * t).astype(BF16)
            half = bd_ref.shape[0]
            ssum = jnp.concatenate(
                [jnp.dot(sq[:, c:c + half], bd_ref[...], preferred_element_type=F32)
                 for c in range(0, ATT_W, half)], axis=1)
            return t * lax.rsqrt(ssum * (1.0 / HEAD_DIM) + EPS) * gain

        q = head_norm(proj(0, ATT_W), gq_ref[...]) * (HEAD_DIM ** -0.5 * LOG2_E)
        q_ref[0] = q.astype(BF16)
        k = head_norm(proj(ATT_W, ATT_W), gk_ref[...])
        kt_ref[0] = k.T.astype(BF16)
        v_ref[0] = proj(2 * ATT_W, ATT_W).astype(BF16)

        gu = _gelu(proj(3 * ATT_W, G_W))
        gv = _gelu(proj(3 * ATT_W + G_W, G_W))
        row = lax.broadcasted_iota(jnp.int32, (G_BLOCK, G_BLOCK), 0) // CHUNK
        col = lax.broadcasted_iota(jnp.int32, (G_BLOCK, G_BLOCK), 1) // CHUNK
        tri = row >= col
        cols = []
        for g in range(G_GROUPS):
            sl = slice(g * G_DIM, (g + 1) * G_DIM)
            vn = _rms(gv[:, sl], gvn_ref[g:g + 1, :]).astype(BF16)
            wm = jnp.where(tri, ws_ref[g], 0.0).astype(BF16)
            blocks = []
            for n in range(TM // G_BLOCK):
                rs = slice(n * G_BLOCK, (n + 1) * G_BLOCK)
                gate = jnp.dot(wm, vn[rs], preferred_element_type=F32) + bst_ref[:, g:g + 1]
                blocks.append(gu[rs, sl] * gate)
            cols.append(jnp.concatenate(blocks, axis=0))
        gm = jnp.concatenate(cols, axis=1)
        gm_ref[0] = _rms(gm, gout_ref[...]).astype(BF16)


def _inproj(x, g_mix, w_in, gq, gk, bd, gvn, ws, bst, gout):
    b, s, d = x.shape
    nt = s // TM
    xmap = lambda bi, j: (bi, jnp.maximum(j - 1, 0), 0)
    const2 = lambda bi, j: (0, 0)
    return pl.pallas_call(
        _inproj_kernel,
        grid=(b, nt + 1),
        in_specs=[
            pl.BlockSpec((1, TM, d), xmap),
            pl.BlockSpec((1, d), const2),
            pl.BlockSpec(w_in.shape, const2),
            pl.BlockSpec((1, ATT_W), const2),
            pl.BlockSpec((1, ATT_W), const2),
            pl.BlockSpec(bd.shape, const2),
            pl.BlockSpec((G_GROUPS, G_DIM), const2),
            pl.BlockSpec((G_GROUPS, G_BLOCK, G_BLOCK), lambda bi, j: (0, 0, 0)),
            pl.BlockSpec((G_BLOCK, G_GROUPS), const2),
            pl.BlockSpec((1, G_W), const2),
        ],
        out_specs=[
            pl.BlockSpec((1, TM, ATT_W), xmap),
            pl.BlockSpec((1, ATT_W, TM), lambda bi, j: (bi, 0, j)),
            pl.BlockSpec((1, TM, ATT_W), lambda bi, j: (bi, j, 0)),
            pl.BlockSpec((1, TM, G_W), xmap),
        ],
        out_shape=[
            jax.ShapeDtypeStruct((b, s, ATT_W), BF16),
            jax.ShapeDtypeStruct((b, ATT_W, s + LEFT), BF16),
            jax.ShapeDtypeStruct((b, s + LEFT, ATT_W), BF16),
            jax.ShapeDtypeStruct((b, s, G_W), BF16),
        ],
        compiler_params=pltpu.CompilerParams(
            dimension_semantics=("arbitrary", "arbitrary"), vmem_limit_bytes=VMEM_LIMIT),
        name="inproj",
    )(x, g_mix, w_in, gq, gk, bd, gvn, ws, bst, gout)


def _attn_kernel(q_ref, kt_hbm, v_hbm, relvec_ref, gout_ref, o_ref, bias_ref, kt_ref, v_ref, kv_sem):
    b = pl.program_id(0)
    i = pl.program_id(1)
    buf = b % 2
    q0 = pl.multiple_of(i * TQ, TQ)
    win = pl.ds(q0, KWIN)
    lane = lax.broadcasted_iota(jnp.int32, (1, 256), 1)

    def fetch(batch, slot):
        return (pltpu.make_async_copy(kt_hbm.at[batch], kt_ref.at[slot], kv_sem.at[slot]),
                pltpu.make_async_copy(v_hbm.at[batch], v_ref.at[slot], kv_sem.at[slot]))

    @pl.when(i == 0)
    def _():
        @pl.when(b == 0)
        def _():
            for copy in fetch(0, 0):
                copy.start()

        for copy in fetch(b, buf):
            copy.wait()

        @pl.when(b + 1 < pl.num_programs(0))
        def _():
            for copy in fetch(b + 1, 1 - buf):
                copy.start()

    @pl.when((pl.program_id(0) == 0) & (i == 0))
    def _():
        r = lax.broadcasted_iota(jnp.int32, (TQ, KWIN), 0)
        j = lax.broadcasted_iota(jnp.int32, (TQ, KWIN), 1)
        lo = (r // CHUNK) * CHUNK
        in_band = (j >= lo) & (j < lo + LEFT + CHUNK)
        for h in range(N_HEADS):
            shifted = pltpu.roll(jnp.broadcast_to(relvec_ref[h:h + 1, :], (TQ, BIAS_LANES)), 0, 1,
                                 stride=1, stride_axis=0)
            bias_ref[h] = jnp.where(in_band, shifted[:, :KWIN], NEG_INF)

    def tile(has_left_padding):
        if has_left_padding:
            colpos = lax.broadcasted_iota(jnp.int32, (1, KWIN), 1)
            pad_bias = jnp.where(colpos >= LEFT - q0, 0.0, NEG_INF).astype(F32)
        outs = []
        for g in range(2):
            gs = slice(g * 256, (g + 1) * 256)
            qg = q_ref[0, :, gs]
            ktg = kt_ref[buf, gs, win]
            vg = v_ref[buf, win, gs]
            acc = jnp.zeros((TQ, 256), F32)
            for hh in range(4):
                h = 4 * g + hh
                hmask = (lane >= hh * HEAD_DIM) & (lane < (hh + 1) * HEAD_DIM)
                qh = jnp.where(hmask, qg, jnp.zeros_like(qg))
                raw = jnp.dot(qh, ktg, preferred_element_type=F32)
                probs, invs = [], []
                for rows, cols in ((slice(0, HALF_ROWS), slice(0, HALF_COLS)),
                                   (slice(HALF_ROWS, TQ), slice(KWIN - HALF_COLS, KWIN))):
                    s = raw[rows, cols] + bias_ref[h, rows, cols]
                    if has_left_padding:
                        s = s + pad_bias[:, cols]
                    m = jnp.max(s, axis=-1, keepdims=True)
                    e = jnp.exp2(s - m)
                    invs.append(1.0 / jnp.sum(e, axis=-1, keepdims=True))
                    rest = jnp.zeros((HALF_ROWS, KWIN - HALF_COLS), BF16)
                    parts = [e.astype(BF16), rest] if cols.start == 0 else [rest, e.astype(BF16)]
                    probs.append(jnp.concatenate(parts, axis=1))
                p = jnp.concatenate(probs, axis=0)
                pv = jnp.dot(p, vg, preferred_element_type=F32)
                acc = jnp.where(hmask, pv * jnp.concatenate(invs, axis=0), acc)
            outs.append(acc)
        att = jnp.concatenate(outs, axis=1)
        o_ref[0] = _rms(att, gout_ref[...]).astype(BF16)

    pl.when(q0 < LEFT)(lambda: tile(True))
    pl.when(q0 >= LEFT)(lambda: tile(False))


def _attention(q, kt, v, relvec, gout):
    b, s, _ = q.shape
    return pl.pallas_call(
        _attn_kernel,
        grid=(b, s // TQ),
        in_specs=[
            pl.BlockSpec((1, TQ, ATT_W), lambda bi, i: (bi, i, 0)),
            pl.BlockSpec(memory_space=pl.ANY),
            pl.BlockSpec(memory_space=pl.ANY),
            pl.BlockSpec(relvec.shape, lambda bi, i: (0, 0)),
            pl.BlockSpec((1, ATT_W), lambda bi, i: (0, 0)),
        ],
        out_specs=pl.BlockSpec((1, TQ, ATT_W), lambda bi, i: (bi, i, 0)),
        out_shape=jax.ShapeDtypeStruct((b, s, ATT_W), BF16),
        scratch_shapes=[pltpu.VMEM((N_HEADS, TQ, KWIN), F32),
                        pltpu.VMEM((2, ATT_W, s + LEFT), BF16), pltpu.VMEM((2, s + LEFT, ATT_W), BF16),
                        pltpu.SemaphoreType.DMA((2,))],
        compiler_params=pltpu.CompilerParams(
            dimension_semantics=("arbitrary", "arbitrary"), vmem_limit_bytes=VMEM_LIMIT),
        name="attention",
    )(q, kt, v, relvec, gout)


def _outproj_kernel(a_ref, gm_ref, x_ref, wo_ref, gmoe_ref, wr_ref, br_ref,
                    h_ref, xn_ref, slot_ref, route_ref, cnt_ref):
    h = (x_ref[...]
         + jnp.dot(a_ref[...], wo_ref[:ATT_W, :], preferred_element_type=F32)
         + jnp.dot(gm_ref[...], wo_ref[ATT_W:, :], preferred_element_type=F32))
    h_ref[...] = h
    xn = _rms(h, gmoe_ref[...])
    xhi = xn.astype(BF16)
    xlo = (xn - xhi.astype(F32)).astype(BF16)
    xn_ref[...] = xhi

    nt = (((1,), (1,)), ((), ()))
    by_xhi = lax.dot_general(wr_ref[...], xhi, nt, preferred_element_type=F32)
    by_xlo = lax.dot_general(wr_ref[:N_EXPERTS, :], xlo, nt, preferred_element_type=F32)
    logits = by_xhi[:N_EXPERTS] + by_xhi[N_EXPERTS:] + by_xlo + br_ref[...]
    eidx = lax.broadcasted_iota(jnp.int32, (N_EXPERTS, TM), 0)
    vals, idxs = [], []
    cur = logits
    for _ in range(TOP_K):
        m = jnp.max(cur, axis=0, keepdims=True)
        ik = jnp.min(jnp.where(cur == m, eidx, N_EXPERTS), axis=0, keepdims=True)
        vals.append(m)
        idxs.append(ik)
        cur = jnp.where(eidx == ik, -jnp.inf, cur)
    exps = [jnp.exp(v - vals[0]) for v in vals]
    tot = exps[0] + exps[1] + exps[2] + exps[3]
    gates = [e / tot for e in exps]

    onehots = [eidx == ik for ik in idxs]
    member = sum(oh.astype(F32) for oh in onehots).astype(BF16)

    earlier_tok = (lax.broadcasted_iota(jnp.int32, (TM, TM), 0)
                   < lax.broadcasted_iota(jnp.int32, (TM, TM), 1)).astype(BF16)
    earlier_exp = (lax.broadcasted_iota(jnp.int32, (N_EXPERTS, N_EXPERTS), 1)
                   < lax.broadcasted_iota(jnp.int32, (N_EXPERTS, N_EXPERTS), 0)).astype(BF16)
    rank = jnp.dot(member, earlier_tok, preferred_element_type=F32)
    lower = jnp.sum(jnp.dot(earlier_exp, member, preferred_element_type=F32), axis=1, keepdims=True)
    base = rank + lower
    slots = [jnp.sum(jnp.where(oh, base, 0.0), axis=0, keepdims=True) for oh in onehots]
    slot_ref[...] = jnp.concatenate(slots, axis=0).astype(jnp.int32)
    route_ref[...] = jnp.concatenate(slots + gates, axis=0).T
    cnt_ref[0] = jnp.sum(member.astype(F32), axis=1, keepdims=True).astype(jnp.int32)


def _outproj(att, gm, x2, w_out, g_moe, wr_parts, br):
    t, d = x2.shape
    tile = lambda i: (i, 0)
    const = lambda i: (0, 0)
    lanes = lambda i: (0, i)
    return pl.pallas_call(
        _outproj_kernel,
        grid=(t // TM,),
        in_specs=[
            pl.BlockSpec((TM, ATT_W), tile),
            pl.BlockSpec((TM, G_W), tile),
            pl.BlockSpec((TM, d), tile),
            pl.BlockSpec(w_out.shape, const),
            pl.BlockSpec((1, d), const),
            pl.BlockSpec((2 * N_EXPERTS, d), const),
            pl.BlockSpec((N_EXPERTS, 1), const),
        ],
        out_specs=[
            pl.BlockSpec((TM, d), tile),
            pl.BlockSpec((TM, d), tile),
            pl.BlockSpec((TOP_K, TM), lanes),
            pl.BlockSpec((TM, 2 * TOP_K), tile),
            pl.BlockSpec((1, N_EXPERTS, 1), lambda i: (i, 0, 0)),
        ],
        out_shape=[
            jax.ShapeDtypeStruct((t, d), F32),
            jax.ShapeDtypeStruct((t, d), BF16),
            jax.ShapeDtypeStruct((TOP_K, t), jnp.int32),
            jax.ShapeDtypeStruct((t, 2 * TOP_K), F32),
            jax.ShapeDtypeStruct((t // TM, N_EXPERTS, 1), jnp.int32),
        ],
        compiler_params=pltpu.CompilerParams(
            dimension_semantics=("arbitrary",), vmem_limit_bytes=VMEM_LIMIT),
        name="outproj_router",
    )(att, gm, x2, w_out, g_moe, wr_parts, br)


def _dispatch_kernel(cnt_ref, loc_ref, glb_ref, pend_ref, padat_ref, padn_ref, xn_ref, slot_ref, xpad_ref,
                     sorted0_ref, sorted1_ref, zero_ref, sem, zsem):
    i = pl.program_id(0)
    n = pl.num_programs(0)
    sorted_refs = (sorted0_ref, sorted1_ref)

    def drain(b):
        pltpu.make_async_copy(sorted_refs[b], _slab(xpad_ref, 0, N_SLOTS), sem.at[b]).wait()

    @pl.when(i == 0)
    def _():
        zero_ref[...] = jnp.zeros_like(zero_ref)

        def padding(wait):
            def per_expert(e, c):
                n = padn_ref[e]
                for bit in range(TME.bit_length() - 1):
                    size = 1 << bit
                    off = n & ~(2 * size * ROW_SUB - 1)

                    @pl.when((n & (size * ROW_SUB)) != 0)
                    def _():
                        copy = pltpu.make_async_copy(
                            _slab(zero_ref, 0, size), _slab_at(xpad_ref, padat_ref[e] + off, size), zsem)
                        copy.wait() if wait else copy.start()
                return c

            lax.fori_loop(0, N_EXPERTS, per_expert, 0)

        padding(wait=False)
        padding(wait=True)

        def tail_block(blk):
            return pltpu.make_async_copy(zero_ref, _slab(xpad_ref, blk * TME, TME), zsem)

        first_tail = pend_ref[N_EXPERTS - 1] // TME
        n_blocks = xpad_ref.shape[0] // (TME * ROW_SUB)
        lax.fori_loop(first_tail, n_blocks, lambda blk, c: (tail_block(blk).start(), c)[1], 0)
        lax.fori_loop(first_tail, n_blocks, lambda blk, c: (tail_block(blk).wait(), c)[1], 0)

    def send(tile, b, enabled=None):
        _segment_copies(
            tile, cnt_ref, loc_ref, glb_ref,
            lambda loc, glb, size: pltpu.make_async_copy(
                _slab_at(sorted_refs[b], loc, size), _slab_at(xpad_ref, glb, size), sem.at[b]),
            enabled)

    def step(buf):
        pl.when(i >= 2)(lambda: drain(buf))

        xn = xn_ref[...]
        for c in range(N_SLOTS // PCHUNK):
            pos = lax.broadcasted_iota(jnp.int32, (PCHUNK, TM), 0) + c * PCHUNK
            hit = pos == slot_ref[0:1, :]
            for k in range(1, TOP_K):
                hit = hit | (pos == slot_ref[k:k + 1, :])
            rows = jnp.dot(hit.astype(F32).astype(BF16), xn, preferred_element_type=F32)
            for sub in range(ROW_SUB):
                sorted_refs[buf][pl.ds(c * PCHUNK * ROW_SUB + sub, PCHUNK, stride=ROW_SUB), :] = (
                    rows[:, sub * LANES:(sub + 1) * LANES])

        send(i, buf)

        @pl.when(i == n - 1)
        def _():
            pl.when(i >= 1)(lambda: drain(1 - buf))
            drain(buf)

    pl.when(i % 2 == 0)(lambda: step(0))
    pl.when(i % 2 == 1)(lambda: step(1))


def _dispatch(tcnt, tloc, tglb, pad_ends, pad_at, pad_n, xn, slots, n_rows):
    t, d = xn.shape
    return pl.pallas_call(
        _dispatch_kernel,
        grid_spec=pltpu.PrefetchScalarGridSpec(
            num_scalar_prefetch=6,
            grid=(t // TM,),
            in_specs=[pl.BlockSpec((TM, d), lambda i, *_: (i, 0)),
                      pl.BlockSpec((TOP_K, TM), lambda i, *_: (0, i))],
            out_specs=pl.BlockSpec(memory_space=pl.ANY),
            scratch_shapes=[pltpu.VMEM((N_SLOTS * ROW_SUB, LANES), F32),
                            pltpu.VMEM((N_SLOTS * ROW_SUB, LANES), F32),
                            pltpu.VMEM((TME * ROW_SUB, LANES), F32),
                            pltpu.SemaphoreType.DMA((2,)), pltpu.SemaphoreType.DMA],
        ),
        out_shape=jax.ShapeDtypeStruct((n_rows * ROW_SUB, LANES), F32),
        compiler_params=pltpu.CompilerParams(
            dimension_semantics=("arbitrary",), vmem_limit_bytes=VMEM_LIMIT),
        name="dispatch",
    )(tcnt, tloc, tglb, pad_ends, pad_at, pad_n, xn, slots)


def _expert_kernel(be_ref, nvb_ref, run_ref, rexp_ref, nrun_ref, used_ref, x_ref, wi_hbm, bi_ref, wo_hbm,
                   bo_ref, y_ref, wi32_ref, wo32_ref, wib_ref, wob_ref, wsem):
    i = pl.program_id(0)
    run = run_ref[i]
    buf = run % 2

    def fetch(r, b):
        e = rexp_ref[r]
        return (pltpu.make_async_copy(wi_hbm.at[e], wi32_ref.at[b], wsem.at[b]),
                pltpu.make_async_copy(wo_hbm.at[e], wo32_ref.at[b], wsem.at[b]))

    @pl.when(i == 0)
    def _():
        for copy in fetch(0, 0):
            copy.start()

    @pl.when((i == 0) | (run != run_ref[jnp.maximum(i - 1, 0)]))
    def _():
        for copy in fetch(run, buf):
            copy.wait()

        @pl.when(run + 1 < nrun_ref[0])
        def _():
            for copy in fetch(run + 1, 1 - buf):
                copy.start()

        wib_ref[...] = wi32_ref[buf].astype(BF16)
        wob_ref[...] = wo32_ref[buf].astype(BF16)

    def mlp(n_rows):
        x = jnp.concatenate(
            [x_ref[pl.ds(sub, n_rows, stride=ROW_SUB), :].astype(BF16) for sub in range(ROW_SUB)],
            axis=1)
        hdn = jnp.dot(x, wib_ref[...], preferred_element_type=F32) + bi_ref[0]
        gl = jnp.minimum(hdn[:, :D_EXPERT], SWIGLU_LIMIT)
        lin = jnp.clip(hdn[:, D_EXPERT:], -SWIGLU_LIMIT, SWIGLU_LIMIT)
        act = gl * jax.nn.sigmoid(SWIGLU_ALPHA * gl) * (lin + 1.0)
        y = jnp.dot(act.astype(BF16), wob_ref[...], preferred_element_type=F32) + bo_ref[0]
        for sub in range(ROW_SUB):
            y_ref[pl.ds(sub, n_rows, stride=ROW_SUB), :] = y[:, sub * LANES:(sub + 1) * LANES]
        if n_rows < TME:
            y_ref[pl.ds(n_rows * ROW_SUB, (TME - n_rows) * ROW_SUB), :] = jnp.zeros(
                ((TME - n_rows) * ROW_SUB, LANES), F32)

    valid = i < nvb_ref[0]
    pl.when(valid & (used_ref[i] > TME // 2))(lambda: mlp(TME))
    pl.when(valid & (used_ref[i] <= TME // 2))(lambda: mlp(TME // 2))

    @pl.when(i >= nvb_ref[0])
    def _():
        y_ref[...] = jnp.zeros_like(y_ref)


def _experts(block_e, nvb, block_run, run_expert, n_runs, block_used, x_pad, w_in, b_in, w_out, b_out):
    n_rows = x_pad.shape[0] // ROW_SUB
    d = D_MODEL
    nb = n_rows // TME
    rows = lambda i, be, nv, *_: (jnp.minimum(i, nv[0] - 1), 0)
    wsel = lambda i, be, *_: (be[i], 0, 0)
    return pl.pallas_call(
        _expert_kernel,
        grid_spec=pltpu.PrefetchScalarGridSpec(
            num_scalar_prefetch=6,
            grid=(nb,),
            in_specs=[
                pl.BlockSpec((TME * ROW_SUB, LANES), rows),
                pl.BlockSpec(memory_space=pl.ANY),
                pl.BlockSpec((1, 1, 2 * D_EXPERT), wsel),
                pl.BlockSpec(memory_space=pl.ANY),
                pl.BlockSpec((1, 1, d), wsel),
            ],
            out_specs=pl.BlockSpec((TME * ROW_SUB, LANES), lambda i, *_: (i, 0)),
            scratch_shapes=[pltpu.VMEM((2, d, 2 * D_EXPERT), F32), pltpu.VMEM((2, D_EXPERT, d), F32),
                            pltpu.VMEM((d, 2 * D_EXPERT), BF16), pltpu.VMEM((D_EXPERT, d), BF16),
                            pltpu.SemaphoreType.DMA((2,))],
        ),
        out_shape=jax.ShapeDtypeStruct((n_rows * ROW_SUB, LANES), F32),
        compiler_params=pltpu.CompilerParams(
            dimension_semantics=("arbitrary",), vmem_limit_bytes=VMEM_LIMIT),
        name="experts",
    )(block_e, nvb, block_run, run_expert, n_runs, block_used, x_pad, w_in, b_in, w_out, b_out)


def _combine_kernel(cnt_ref, loc_ref, glb_ref, ypad_ref, h_ref, route_ref, o_ref,
                    sorted0_ref, sorted1_ref, sem):
    i = pl.program_id(0)
    n = pl.num_programs(0)
    sorted_refs = (sorted0_ref, sorted1_ref)

    def fetch(tile, b, enabled=None):
        _segment_copies(
            tile, cnt_ref, loc_ref, glb_ref,
            lambda loc, glb, size: pltpu.make_async_copy(
                _slab_at(ypad_ref, glb, size), _slab_at(sorted_refs[b], loc, size), sem.at[b]),
            enabled)

    pl.when(i == 0)(lambda: fetch(0, 0))

    def step(buf):
        pltpu.make_async_copy(_slab(ypad_ref, 0, N_SLOTS), sorted_refs[buf], sem.at[buf]).wait()
        fetch(jnp.minimum(i + 1, n - 1), 1 - buf, enabled=i + 1 < n)

        acc = h_ref[...]
        slot_cols = route_ref[:, :TOP_K].astype(jnp.int32)
        for c in range(N_SLOTS // PCHUNK):
            pos = lax.broadcasted_iota(jnp.int32, (TM, PCHUNK), 1) + c * PCHUNK
            weight = jnp.zeros((TM, PCHUNK), F32)
            for k in range(TOP_K):
                weight = weight + jnp.where(pos == slot_cols[:, k:k + 1],
                                            route_ref[:, TOP_K + k:TOP_K + k + 1], 0.0)
            y = jnp.concatenate(
                [sorted_refs[buf][pl.ds(c * PCHUNK * ROW_SUB + sub, PCHUNK, stride=ROW_SUB), :].astype(BF16)
                 for sub in range(ROW_SUB)], axis=1)
            acc = acc + jnp.dot(weight.astype(BF16), y, preferred_element_type=F32)
        o_ref[...] = acc

    pl.when(i % 2 == 0)(lambda: step(0))
    pl.when(i % 2 == 1)(lambda: step(1))


def _combine(tcnt, tloc, tglb, y_pad, h, route):
    t, d = h.shape
    tile = lambda i, *_: (i, 0)
    return pl.pallas_call(
        _combine_kernel,
        grid_spec=pltpu.PrefetchScalarGridSpec(
            num_scalar_prefetch=3,
            grid=(t // TM,),
            in_specs=[
                pl.BlockSpec(memory_space=pl.ANY),
                pl.BlockSpec((TM, d), tile),
                pl.BlockSpec((TM, 2 * TOP_K), tile),
            ],
            out_specs=pl.BlockSpec((TM, d), tile),
            scratch_shapes=[pltpu.VMEM((N_SLOTS * ROW_SUB, LANES), F32),
                            pltpu.VMEM((N_SLOTS * ROW_SUB, LANES), F32),
                            pltpu.SemaphoreType.DMA((2,))],
        ),
        out_shape=jax.ShapeDtypeStruct((t, d), F32),
        compiler_params=pltpu.CompilerParams(
            dimension_semantics=("arbitrary",), vmem_limit_bytes=VMEM_LIMIT),
        name="combine",
    )(tcnt, tloc, tglb, y_pad, h, route)


def _rel_vector(rel_bias):
    n_heads = rel_bias.shape[0]
    far = LEFT - REL_CLIP
    falling = rel_bias[:, ::-1][:, 1:]
    n_fall = min(KWIN - 1 - far, 2 * REL_CLIP)
    vec = jnp.concatenate(
        [jnp.broadcast_to(rel_bias[:, -1:], (n_heads, far + 1)), falling[:, :n_fall],
         jnp.broadcast_to(rel_bias[:, :1], (n_heads, KWIN - 1 - far - n_fall)),
         jnp.broadcast_to(rel_bias[:, -1:], (n_heads, BIAS_LANES - KWIN))], axis=1)
    return vec.astype(F32) * LOG2_E


def kernel(x, norm_mix_g, w_in_proj, q_norm_g, k_norm_g, rel_bias, gmlp_v_norm_g, gmlp_w_s, gmlp_b_s,
           att_out_norm_g, gmlp_out_norm_g, w_out_proj, norm_moe_g, w_router, b_router,
           w_expert_in, b_expert_in, w_expert_out, b_expert_out):
    b, s, d = x.shape
    t = b * s

    head_of = jnp.arange(MXU_DIM) // HEAD_DIM
    block_diag = (head_of[:, None] == head_of[None, :]).astype(BF16)
    q, kt, v, gm = _inproj(
        x, norm_mix_g.reshape(1, d), w_in_proj.astype(BF16),
        jnp.tile(q_norm_g, N_HEADS).reshape(1, ATT_W), jnp.tile(k_norm_g, N_HEADS).reshape(1, ATT_W),
        block_diag, gmlp_v_norm_g, gmlp_w_s, gmlp_b_s.T, gmlp_out_norm_g.reshape(1, G_W))
    att = _attention(q, kt, v, _rel_vector(rel_bias), att_out_norm_g.reshape(1, ATT_W))

    wr_t = w_router.T
    wr_hi = wr_t.astype(BF16)
    wr_lo = (wr_t - wr_hi.astype(F32)).astype(BF16)
    h, xn, slots, route, tile_cnt = _outproj(
        att.reshape(t, ATT_W), gm.reshape(t, G_W), x.reshape(t, d), w_out_proj.astype(BF16),
        norm_moe_g.reshape(1, d), jnp.concatenate([wr_hi, wr_lo], axis=0), b_router.reshape(N_EXPERTS, 1))

    tcnt = tile_cnt[:, :, 0]
    counts = jnp.sum(tcnt, axis=0)
    padded = ((counts + TME - 1) // TME) * TME
    pad_ends = jnp.cumsum(padded).astype(jnp.int32)
    pad_starts = pad_ends - padded
    tloc = jnp.cumsum(tcnt, axis=1) - tcnt
    tglb = pad_starts[None, :] + jnp.cumsum(tcnt, axis=0) - tcnt
    n_blocks = (t * TOP_K) // TME + N_EXPERTS
    nvb = pad_ends[-1] // TME
    block_start = jnp.arange(n_blocks, dtype=jnp.int32) * TME
    block_e = jnp.minimum(jnp.sum(pad_ends[None, :] <= block_start[:, None], axis=1), N_EXPERTS - 1)
    block_e = jnp.where(jnp.arange(n_blocks) < nvb, block_e, block_e[nvb - 1]).astype(jnp.int32)
    runs = tuple((a * ROW_SUB).reshape(-1).astype(jnp.int32) for a in (tcnt, tloc, tglb))
    present = padded > 0
    run_of_expert = jnp.cumsum(present) - 1
    experts_iota = jnp.arange(N_EXPERTS)
    block_run = jnp.sum(jnp.where(block_e[:, None] == experts_iota[None, :], run_of_expert[None, :], 0),
                        axis=1).astype(jnp.int32)
    run_expert = jnp.sum(jnp.where(present[None, :] & (run_of_expert[None, :] == experts_iota[:, None]),
                                   experts_iota[None, :], 0), axis=1).astype(jnp.int32)
    n_runs = jnp.sum(present).astype(jnp.int32).reshape(1)
    of_block = block_e[:, None] == experts_iota[None, :]
    rows_left = jnp.sum(jnp.where(of_block, (pad_starts + counts)[None, :], 0), axis=1) - block_start
    block_used = jnp.clip(rows_left, 0, TME).astype(jnp.int32)

    pad_at = ((pad_starts + counts) * ROW_SUB).astype(jnp.int32)
    pad_n = ((padded - counts) * ROW_SUB).astype(jnp.int32)
    x_pad = _dispatch(*runs, pad_ends, pad_at, pad_n, xn, slots, n_blocks * TME)
    y_pad = _experts(block_e, nvb.reshape(1), block_run, run_expert, n_runs, block_used, x_pad, w_expert_in,
                     b_expert_in.reshape(N_EXPERTS, 1, 2 * D_EXPERT), w_expert_out,
                     b_expert_out.reshape(N_EXPERTS, 1, d))
    out = _combine(*runs, y_pad, h, route)
    return out.reshape(b, s, d)
```

```python
import functools

import jax
import jax.numpy as jnp
from jax import lax
from jax.experimental import pallas as pl
from jax.experimental.pallas import tpu as pltpu

D_MODEL = 1024
CHUNK = 64
LEFT = 8 * CHUNK
N_HEADS = 8
HEAD_DIM = 64
ATT_W = 512
REL_CLIP = 128
G_GROUPS = 4
G_DIM = 128
G_BLOCK = 128
G_W = 512
N_EXPERTS = 32
TOP_K = 4
D_EXPERT = 1024
SWIGLU_LIMIT = 7.0
SWIGLU_ALPHA = 1.702
EPS = 1e-6
NEG_INF = -1e30
LOG2_E = 1.4426950408889634

TM = 512
TQ = 256
KWIN = TQ + LEFT
BIAS_LANES = 1024
HALF_ROWS = TQ // 2
HALF_COLS = HALF_ROWS + LEFT
TME = 512
N_SLOTS = TM * TOP_K
PCHUNK = 256
SEG_BITS = TM.bit_length()
VMEM_LIMIT = 56 * 1024 * 1024
LANES = 128
MXU_DIM = 256
ROW_SUB = D_MODEL // LANES

F32 = jnp.float32
BF16 = jnp.bfloat16


def _rms(x, gain):
    ms = jnp.mean(x * x, axis=-1, keepdims=True)
    return x * lax.rsqrt(ms + EPS) * gain


def _gelu(x):
    return 0.5 * x * (1.0 + lax.erf(x * (2.0 ** -0.5)))


def _slab(ref, row, n_rows=1):
    return ref.at[pl.ds(pl.multiple_of(row * ROW_SUB, ROW_SUB), n_rows * ROW_SUB)]


def _slab_at(ref, offset, n_rows):
    return ref.at[pl.ds(pl.multiple_of(offset, ROW_SUB), n_rows * ROW_SUB)]


def _segment_copies(tile, cnt_ref, loc_ref, glb_ref, make_copy, enabled=None):
    def per_expert(e, c):
        n = cnt_ref[tile * N_EXPERTS + e]
        if enabled is not None:
            n = jnp.where(enabled, n, 0)
        loc = loc_ref[tile * N_EXPERTS + e]
        glb = glb_ref[tile * N_EXPERTS + e]
        for bit in range(SEG_BITS):
            size = 1 << bit
            off = n & ~(2 * size * ROW_SUB - 1)

            @pl.when((n & (size * ROW_SUB)) != 0)
            def _():
                make_copy(loc + off, glb + off, size).start()
        return c

    if enabled is None:
        lax.fori_loop(0, N_EXPERTS, per_expert, 0, unroll=2)
    else:
        for e in range(N_EXPERTS):
            per_expert(e, 0)


def _inproj_kernel(x_ref, g_ref, w_ref, gq_ref, gk_ref, bd_ref, gvn_ref, ws_ref, bst_ref, gout_ref,
                   q_ref, kt_ref, v_ref, gm_ref):
    j = pl.program_id(1)

    @pl.when(j == 0)
    def _():
        kt_ref[...] = jnp.zeros_like(kt_ref)
        v_ref[...] = jnp.zeros_like(v_ref)

    @pl.when(j > 0)
    def _():
        z = _rms(x_ref[0], g_ref[...]).astype(BF16)

        def proj(c0, width):
            return jnp.dot(z, w_ref[:, c0:c0 + width], preferred_element_type=F32)

        def head_norm(t, gain):
            sq = (t * t).astype(BF16)
            half = bd_ref.shape[0]
            ssum = jnp.concatenate(
                [jnp.dot(sq[:, c:c + half], bd_ref[...], preferred_element_type=F32)
                 for c in range(0, ATT_W, half)], axis=1)
            return t * lax.rsqrt(ssum * (1.0 / HEAD_DIM) + EPS) * gain

        q = head_norm(proj(0, ATT_W), gq_ref[...]) * (HEAD_DIM ** -0.5 * LOG2_E)
        q_ref[0] = q.astype(BF16)
        k = head_norm(proj(ATT_W, ATT_W), gk_ref[...])
        kt_ref[0] = k.T.astype(BF16)
        v_ref[0] = proj(2 * ATT_W, ATT_W).astype(BF16)

        gu = _gelu(proj(3 * ATT_W, G_W))
        gv = _gelu(proj(3 * ATT_W + G_W, G_W))
        row = lax.broadcasted_iota(jnp.int32, (G_BLOCK, G_BLOCK), 0) // CHUNK
        col = lax.broadcasted_iota(jnp.int32, (G_BLOCK, G_BLOCK), 1) // CHUNK
        tri = row >= col
        cols = []
        for g in range(G_GROUPS):
            sl = slice(g * G_DIM, (g + 1) * G_DIM)
            vn = _rms(gv[:, sl], gvn_ref[g:g + 1, :]).astype(BF16)
            wm = jnp.where(tri, ws_ref[g], 0.0).astype(BF16)
            blocks = []
            for n in range(TM // G_BLOCK):
                rs = slice(n * G_BLOCK, (n + 1) * G_BLOCK)
                gate = jnp.dot(wm, vn[rs], preferred_element_type=F32) + bst_ref[:, g:g + 1]
                blocks.append(gu[rs, sl] * gate)
            cols.append(jnp.concatenate(blocks, axis=0))
        gm = jnp.concatenate(cols, axis=1)
        gm_ref[0] = _rms(gm, gout_ref[...]).astype(BF16)


def _inproj(x, g_mix, w_in, gq, gk, bd, gvn, ws, bst, gout):
    b, s, d = x.shape
    nt = s // TM
    xmap = lambda bi, j: (bi, jnp.maximum(j - 1, 0), 0)
    const2 = lambda bi, j: (0, 0)
    return pl.pallas_call(
        _inproj_kernel,
        grid=(b, nt + 1),
        in_specs=[
            pl.BlockSpec((1, TM, d), xmap),
            pl.BlockSpec((1, d), const2),
            pl.BlockSpec(w_in.shape, const2),
            pl.BlockSpec((1, ATT_W), const2),
            pl.BlockSpec((1, ATT_W), const2),
            pl.BlockSpec(bd.shape, const2),
            pl.BlockSpec((G_GROUPS, G_DIM), const2),
            pl.BlockSpec((G_GROUPS, G_BLOCK, G_BLOCK), lambda bi, j: (0, 0, 0)),
            pl.BlockSpec((G_BLOCK, G_GROUPS), const2),
            pl.BlockSpec((1, G_W), const2),
        ],
        out_specs=[
            pl.BlockSpec((1, TM, ATT_W), xmap),
            pl.BlockSpec((1, ATT_W, TM), lambda bi, j: (bi, 0, j)),
            pl.BlockSpec((1, TM, ATT_W), lambda bi, j: (bi, j, 0)),
            pl.BlockSpec((1, TM, G_W), xmap),
        ],
        out_shape=[
            jax.ShapeDtypeStruct((b, s, ATT_W), BF16),
            jax.ShapeDtypeStruct((b, ATT_W, s + LEFT), BF16),
            jax.ShapeDtypeStruct((b, s + LEFT, ATT_W), BF16),
            jax.ShapeDtypeStruct((b, s, G_W), BF16),
        ],
        compiler_params=pltpu.CompilerParams(
            dimension_semantics=("arbitrary", "arbitrary"), vmem_limit_bytes=VMEM_LIMIT),
        name="inproj",
    )(x, g_mix, w_in, gq, gk, bd, gvn, ws, bst, gout)


def _attn_kernel(q_ref, kt_hbm, v_hbm, relvec_ref, gout_ref, o_ref, bias_ref, kt_ref, v_ref, kv_sem):
    b = pl.program_id(0)
    i = pl.program_id(1)
    buf = b % 2
    q0 = pl.multiple_of(i * TQ, TQ)
    win = pl.ds(q0, KWIN)
    lane = lax.broadcasted_iota(jnp.int32, (1, 256), 1)

    def fetch(batch, slot):
        return (pltpu.make_async_copy(kt_hbm.at[batch], kt_ref.at[slot], kv_sem.at[slot]),
                pltpu.make_async_copy(v_hbm.at[batch], v_ref.at[slot], kv_sem.at[slot]))

    @pl.when(i == 0)
    def _():
        @pl.when(b == 0)
        def _():
            for copy in fetch(0, 0):
                copy.start()

        for copy in fetch(b, buf):
            copy.wait()

        @pl.when(b + 1 < pl.num_programs(0))
        def _():
            for copy in fetch(b + 1, 1 - buf):
                copy.start()

    @pl.when((pl.program_id(0) == 0) & (i == 0))
    def _():
        r = lax.broadcasted_iota(jnp.int32, (TQ, KWIN), 0)
        j = lax.broadcasted_iota(jnp.int32, (TQ, KWIN), 1)
        lo = (r // CHUNK) * CHUNK
        in_band = (j >= lo) & (j < lo + LEFT + CHUNK)
        for h in range(N_HEADS):
            shifted = pltpu.roll(jnp.broadcast_to(relvec_ref[h:h + 1, :], (TQ, BIAS_LANES)), 0, 1,
                                 stride=1, stride_axis=0)
            bias_ref[h] = jnp.where(in_band, shifted[:, :KWIN], NEG_INF)

    def tile(has_left_padding):
        if has_left_padding:
            colpos = lax.broadcasted_iota(jnp.int32, (1, KWIN), 1)
            pad_bias = jnp.where(colpos >= LEFT - q0, 0.0, NEG_INF).astype(F32)
        outs = []
        for g in range(2):
            gs = slice(g * 256, (g + 1) * 256)
            qg = q_ref[0, :, gs]
            ktg = kt_ref[buf, gs, win]
            vg = v_ref[buf, win, gs]
            acc = jnp.zeros((TQ, 256), F32)
            for hh in range(4):
                h = 4 * g + hh
                hmask = (lane >= hh * HEAD_DIM) & (lane < (hh + 1) * HEAD_DIM)
                qh = jnp.where(hmask, qg, jnp.zeros_like(qg))
                raw = jnp.dot(qh, ktg, preferred_element_type=F32)
                probs, invs = [], []
                for rows, cols in ((slice(0, HALF_ROWS), slice(0, HALF_COLS)),
                                   (slice(HALF_ROWS, TQ), slice(KWIN - HALF_COLS, KWIN))):
                    s = raw[rows, cols] + bias_ref[h, rows, cols]
                    if has_left_padding:
                        s = s + pad_bias[:, cols]
                    m = jnp.max(s, axis=-1, keepdims=True)
                    e = jnp.exp2(s - m)
                    invs.append(1.0 / jnp.sum(e, axis=-1, keepdims=True))
                    rest = jnp.zeros((HALF_ROWS, KWIN - HALF_COLS), BF16)
                    parts = [e.astype(BF16), rest] if cols.start == 0 else [rest, e.astype(BF16)]
                    probs.append(jnp.concatenate(parts, axis=1))
                p = jnp.concatenate(probs, axis=0)
                pv = jnp.dot(p, vg, preferred_element_type=F32)
                acc = jnp.where(hmask, pv * jnp.concatenate(invs, axis=0), acc)
            outs.append(acc)
        att = jnp.concatenate(outs, axis=1)
        o_ref[0] = _rms(att, gout_ref[...]).astype(BF16)

    pl.when(q0 < LEFT)(lambda: tile(True))
    pl.when(q0 >= LEFT)(lambda: tile(False))


def _attention(q, kt, v, relvec, gout):
    b, s, _ = q.shape
    return pl.pallas_call(
        _attn_kernel,
        grid=(b, s // TQ),
        in_specs=[
            pl.BlockSpec((1, TQ, ATT_W), lambda bi, i: (bi, i, 0)),
            pl.BlockSpec(memory_space=pl.ANY),
            pl.BlockSpec(memory_space=pl.ANY),
            pl.BlockSpec(relvec.shape, lambda bi, i: (0, 0)),
            pl.BlockSpec((1, ATT_W), lambda bi, i: (0, 0)),
        ],
        out_specs=pl.BlockSpec((1, TQ, ATT_W), lambda bi, i: (bi, i, 0)),
        out_shape=jax.ShapeDtypeStruct((b, s, ATT_W), BF16),
        scratch_shapes=[pltpu.VMEM((N_HEADS, TQ, KWIN), F32),
                        pltpu.VMEM((2, ATT_W, s + LEFT), BF16), pltpu.VMEM((2, s + LEFT, ATT_W), BF16),
                        pltpu.SemaphoreType.DMA((2,))],
        compiler_params=pltpu.CompilerParams(
            dimension_semantics=("arbitrary", "arbitrary"), vmem_limit_bytes=VMEM_LIMIT),
        name="attention",
    )(q, kt, v, relvec, gout)


def _outproj_kernel(a_ref, gm_ref, x_ref, wo_ref, gmoe_ref, wr_ref, br_ref,
                    h_ref, xn_ref, slot_ref, route_ref, cnt_ref):
    h = (x_ref[...]
         + jnp.dot(a_ref[...], wo_ref[:ATT_W, :], preferred_element_type=F32)
         + jnp.dot(gm_ref[...], wo_ref[ATT_W:, :], preferred_element_type=F32))
    h_ref[...] = h
    xn = _rms(h, gmoe_ref[...])
    xhi = xn.astype(BF16)
    xlo = (xn - xhi.astype(F32)).astype(BF16)
    xn_ref[...] = xhi

    nt = (((1,), (1,)), ((), ()))
    by_xhi = lax.dot_general(wr_ref[...], xhi, nt, preferred_element_type=F32)
    by_xlo = lax.dot_general(wr_ref[:N_EXPERTS, :], xlo, nt, preferred_element_type=F32)
    logits = by_xhi[:N_EXPERTS] + by_xhi[N_EXPERTS:] + by_xlo + br_ref[...]
    eidx = lax.broadcasted_iota(jnp.int32, (N_EXPERTS, TM), 0)
    vals, idxs = [], []
    cur = logits
    for _ in range(TOP_K):
        m = jnp.max(cur, axis=0, keepdims=True)
        ik = jnp.min(jnp.where(cur == m, eidx, N_EXPERTS), axis=0, keepdims=True)
        vals.append(m)
        idxs.append(ik)
        cur = jnp.where(eidx == ik, -jnp.inf, cur)
    exps = [jnp.exp(v - vals[0]) for v in vals]
    tot = exps[0] + exps[1] + exps[2] + exps[3]
    gates = [e / tot for e in exps]

    onehots = [eidx == ik for ik in idxs]
    member = sum(oh.astype(F32) for oh in onehots).astype(BF16)

    earlier_tok = (lax.broadcasted_iota(jnp.int32, (TM, TM), 0)
                   < lax.broadcasted_iota(jnp.int32, (TM, TM), 1)).astype(BF16)
    earlier_exp = (lax.broadcasted_iota(jnp.int32, (N_EXPERTS, N_EXPERTS), 1)
                   < lax.broadcasted_iota(jnp.int32, (N_EXPERTS, N_EXPERTS), 0)).astype(BF16)
    rank = jnp.dot(member, earlier_tok, preferred_element_type=F32)
    lower = jnp.sum(jnp.dot(earlier_exp, member, preferred_element_type=F32), axis=1, keepdims=True)
    base = rank + lower
    slots = [jnp.sum(jnp.where(oh, base, 0.0), axis=0, keepdims=True) for oh in onehots]
    slot_ref[...] = jnp.concatenate(slots, axis=0).astype(jnp.int32)
    route_ref[...] = jnp.concatenate(slots + gates, axis=0).T
    cnt_ref[0] = jnp.sum(member.astype(F32), axis=1, keepdims=True).astype(jnp.int32)


def _outproj(att, gm, x2, w_out, g_moe, wr_parts, br):
    t, d = x2.shape
    tile = lambda i: (i, 0)
    const = lambda i: (0, 0)
    lanes = lambda i: (0, i)
    return pl.pallas_call(
        _outproj_kernel,
        grid=(t // TM,),
        in_specs=[
            pl.BlockSpec((TM, ATT_W), tile),
            pl.BlockSpec((TM, G_W), tile),
            pl.BlockSpec((TM, d), tile),
            pl.BlockSpec(w_out.shape, const),
            pl.BlockSpec((1, d), const),
            pl.BlockSpec((2 * N_EXPERTS, d), const),
            pl.BlockSpec((N_EXPERTS, 1), const),
        ],
        out_specs=[
            pl.BlockSpec((TM, d), tile),
            pl.BlockSpec((TM, d), tile),
            pl.BlockSpec((TOP_K, TM), lanes),
            pl.BlockSpec((TM, 2 * TOP_K), tile),
            pl.BlockSpec((1, N_EXPERTS, 1), lambda i: (i, 0, 0)),
        ],
        out_shape=[
            jax.ShapeDtypeStruct((t, d), F32),
            jax.ShapeDtypeStruct((t, d), BF16),
            jax.ShapeDtypeStruct((TOP_K, t), jnp.int32),
            jax.ShapeDtypeStruct((t, 2 * TOP_K), F32),
            jax.ShapeDtypeStruct((t // TM, N_EXPERTS, 1), jnp.int32),
        ],
        compiler_params=pltpu.CompilerParams(
            dimension_semantics=("arbitrary",), vmem_limit_bytes=VMEM_LIMIT),
        name="outproj_router",
    )(att, gm, x2, w_out, g_moe, wr_parts, br)


def _dispatch_kernel(cnt_ref, loc_ref, glb_ref, pend_ref, padat_ref, padn_ref, xn_ref, slot_ref, xpad_ref,
                     sorted0_ref, sorted1_ref, zero_ref, sem, zsem):
    i = pl.program_id(0)
    n = pl.num_programs(0)
    sorted_refs = (sorted0_ref, sorted1_ref)

    def drain(b):
        pltpu.make_async_copy(sorted_refs[b], _slab(xpad_ref, 0, N_SLOTS), sem.at[b]).wait()

    @pl.when(i == 0)
    def _():
        zero_ref[...] = jnp.zeros_like(zero_ref)

        def padding(wait):
            def per_expert(e, c):
                n = padn_ref[e]
                for bit in range(TME.bit_length() - 1):
                    size = 1 << bit
                    off = n & ~(2 * size * ROW_SUB - 1)

                    @pl.when((n & (size * ROW_SUB)) != 0)
                    def _():
                        copy = pltpu.make_async_copy(
                            _slab(zero_ref, 0, size), _slab_at(xpad_ref, padat_ref[e] + off, size), zsem)
                        copy.wait() if wait else copy.start()
                return c

            lax.fori_loop(0, N_EXPERTS, per_expert, 0)

        padding(wait=False)
        padding(wait=True)

        def tail_block(blk):
            return pltpu.make_async_copy(zero_ref, _slab(xpad_ref, blk * TME, TME), zsem)

        first_tail = pend_ref[N_EXPERTS - 1] // TME
        n_blocks = xpad_ref.shape[0] // (TME * ROW_SUB)
        lax.fori_loop(first_tail, n_blocks, lambda blk, c: (tail_block(blk).start(), c)[1], 0)
        lax.fori_loop(first_tail, n_blocks, lambda blk, c: (tail_block(blk).wait(), c)[1], 0)

    def send(tile, b, enabled=None):
        _segment_copies(
            tile, cnt_ref, loc_ref, glb_ref,
            lambda loc, glb, size: pltpu.make_async_copy(
                _slab_at(sorted_refs[b], loc, size), _slab_at(xpad_ref, glb, size), sem.at[b]),
            enabled)

    def step(buf):
        pl.when(i >= 2)(lambda: drain(buf))

        xn = xn_ref[...]
        for c in range(N_SLOTS // PCHUNK):
            pos = lax.broadcasted_iota(jnp.int32, (PCHUNK, TM), 0) + c * PCHUNK
            hit = pos == slot_ref[0:1, :]
            for k in range(1, TOP_K):
                hit = hit | (pos == slot_ref[k:k + 1, :])
            rows = jnp.dot(hit.astype(F32).astype(BF16), xn, preferred_element_type=F32)
            for sub in range(ROW_SUB):
                sorted_refs[buf][pl.ds(c * PCHUNK * ROW_SUB + sub, PCHUNK, stride=ROW_SUB), :] = (
                    rows[:, sub * LANES:(sub + 1) * LANES])

        send(i, buf)

        @pl.when(i == n - 1)
        def _():
            pl.when(i >= 1)(lambda: drain(1 - buf))
            drain(buf)

    pl.when(i % 2 == 0)(lambda: step(0))
    pl.when(i % 2 == 1)(lambda: step(1))


def _dispatch(tcnt, tloc, tglb, pad_ends, pad_at, pad_n, xn, slots, n_rows):
    t, d = xn.shape
    return pl.pallas_call(
        _dispatch_kernel,
        grid_spec=pltpu.PrefetchScalarGridSpec(
            num_scalar_prefetch=6,
            grid=(t // TM,),
            in_specs=[pl.BlockSpec((TM, d), lambda i, *_: (i, 0)),
                      pl.BlockSpec((TOP_K, TM), lambda i, *_: (0, i))],
            out_specs=pl.BlockSpec(memory_space=pl.ANY),
            scratch_shapes=[pltpu.VMEM((N_SLOTS * ROW_SUB, LANES), F32),
                            pltpu.VMEM((N_SLOTS * ROW_SUB, LANES), F32),
                            pltpu.VMEM((TME * ROW_SUB, LANES), F32),
                            pltpu.SemaphoreType.DMA((2,)), pltpu.SemaphoreType.DMA],
        ),
        out_shape=jax.ShapeDtypeStruct((n_rows * ROW_SUB, LANES), F32),
        compiler_params=pltpu.CompilerParams(
            dimension_semantics=("arbitrary",), vmem_limit_bytes=VMEM_LIMIT),
        name="dispatch",
    )(tcnt, tloc, tglb, pad_ends, pad_at, pad_n, xn, slots)


def _expert_kernel(be_ref, nvb_ref, run_ref, rexp_ref, nrun_ref, used_ref, x_ref, wi_hbm, bi_ref, wo_hbm,
                   bo_ref, y_ref, wi32_ref, wo32_ref, wib_ref, wob_ref, wsem):
    i = pl.program_id(0)
    run = run_ref[i]
    buf = run % 2

    def fetch(r, b):
        e = rexp_ref[r]
        return (pltpu.make_async_copy(wi_hbm.at[e], wi32_ref.at[b], wsem.at[b]),
                pltpu.make_async_copy(wo_hbm.at[e], wo32_ref.at[b], wsem.at[b]))

    @pl.when(i == 0)
    def _():
        for copy in fetch(0, 0):
            copy.start()

    @pl.when((i == 0) | (run != run_ref[jnp.maximum(i - 1, 0)]))
    def _():
        for copy in fetch(run, buf):
            copy.wait()

        @pl.when(run + 1 < nrun_ref[0])
        def _():
            for copy in fetch(run + 1, 1 - buf):
                copy.start()

        wib_ref[...] = wi32_ref[buf].astype(BF16)
        wob_ref[...] = wo32_ref[buf].astype(BF16)

    def mlp(n_rows):
        x = jnp.concatenate(
            [x_ref[pl.ds(sub, n_rows, stride=ROW_SUB), :].astype(BF16) for sub in range(ROW_SUB)],
            axis=1)
        hdn = jnp.dot(x, wib_ref[...], preferred_element_type=F32) + bi_ref[0]
        gl = jnp.minimum(hdn[:, :D_EXPERT], SWIGLU_LIMIT)
        lin = jnp.clip(hdn[:, D_EXPERT:], -SWIGLU_LIMIT, SWIGLU_LIMIT)
        act = gl * jax.nn.sigmoid(SWIGLU_ALPHA * gl) * (lin + 1.0)
        y = jnp.dot(act.astype(BF16), wob_ref[...], preferred_element_type=F32) + bo_ref[0]
        for sub in range(ROW_SUB):
            y_ref[pl.ds(sub, n_rows, stride=ROW_SUB), :] = y[:, sub * LANES:(sub + 1) * LANES]
        if n_rows < TME:
            y_ref[pl.ds(n_rows * ROW_SUB, (TME - n_rows) * ROW_SUB), :] = jnp.zeros(
                ((TME - n_rows) * ROW_SUB, LANES), F32)

    valid = i < nvb_ref[0]
    quarter = TME // 4
    quarters_used = (used_ref[i] + quarter - 1) // quarter
    for n_quarters in range(1, 5):
        selected = quarters_used <= 1 if n_quarters == 1 else quarters_used == n_quarters
        pl.when(valid & selected)(functools.partial(mlp, n_quarters * quarter))

    @pl.when(i >= nvb_ref[0])
    def _():
        y_ref[...] = jnp.zeros_like(y_ref)


def _experts(block_e, nvb, block_run, run_expert, n_runs, block_used, x_pad, w_in, b_in, w_out, b_out):
    n_rows = x_pad.shape[0] // ROW_SUB
    d = D_MODEL
    nb = n_rows // TME
    rows = lambda i, be, nv, *_: (jnp.minimum(i, nv[0] - 1), 0)
    wsel = lambda i, be, *_: (be[i], 0, 0)
    return pl.pallas_call(
        _expert_kernel,
        grid_spec=pltpu.PrefetchScalarGridSpec(
            num_scalar_prefetch=6,
            grid=(nb,),
            in_specs=[
                pl.BlockSpec((TME * ROW_SUB, LANES), rows),
                pl.BlockSpec(memory_space=pl.ANY),
                pl.BlockSpec((1, 1, 2 * D_EXPERT), wsel),
                pl.BlockSpec(memory_space=pl.ANY),
                pl.BlockSpec((1, 1, d), wsel),
            ],
            out_specs=pl.BlockSpec((TME * ROW_SUB, LANES), lambda i, *_: (i, 0)),
            scratch_shapes=[pltpu.VMEM((2, d, 2 * D_EXPERT), F32), pltpu.VMEM((2, D_EXPERT, d), F32),
                            pltpu.VMEM((d, 2 * D_EXPERT), BF16), pltpu.VMEM((D_EXPERT, d), BF16),
                            pltpu.SemaphoreType.DMA((2,))],
        ),
        out_shape=jax.ShapeDtypeStruct((n_rows * ROW_SUB, LANES), F32),
        compiler_params=pltpu.CompilerParams(
            dimension_semantics=("arbitrary",), vmem_limit_bytes=VMEM_LIMIT),
        name="experts",
    )(block_e, nvb, block_run, run_expert, n_runs, block_used, x_pad, w_in, b_in, w_out, b_out)


def _combine_kernel(cnt_ref, loc_ref, glb_ref, ypad_ref, h_ref, route_ref, o_ref,
                    sorted0_ref, sorted1_ref, sem):
    i = pl.program_id(0)
    n = pl.num_programs(0)
    sorted_refs = (sorted0_ref, sorted1_ref)

    def fetch(tile, b, enabled=None):
        _segment_copies(
            tile, cnt_ref, loc_ref, glb_ref,
            lambda loc, glb, size: pltpu.make_async_copy(
                _slab_at(ypad_ref, glb, size), _slab_at(sorted_refs[b], loc, size), sem.at[b]),
            enabled)

    pl.when(i == 0)(lambda: fetch(0, 0))

    def step(buf):
        pltpu.make_async_copy(_slab(ypad_ref, 0, N_SLOTS), sorted_refs[buf], sem.at[buf]).wait()
        fetch(jnp.minimum(i + 1, n - 1), 1 - buf, enabled=i + 1 < n)

        acc = h_ref[...]
        slot_cols = route_ref[:, :TOP_K].astype(jnp.int32)
        for c in range(N_SLOTS // PCHUNK):
            pos = lax.broadcasted_iota(jnp.int32, (TM, PCHUNK), 1) + c * PCHUNK
            weight = jnp.zeros((TM, PCHUNK), F32)
            for k in range(TOP_K):
                weight = weight + jnp.where(pos == slot_cols[:, k:k + 1],
                                            route_ref[:, TOP_K + k:TOP_K + k + 1], 0.0)
            y = jnp.concatenate(
                [sorted_refs[buf][pl.ds(c * PCHUNK * ROW_SUB + sub, PCHUNK, stride=ROW_SUB), :].astype(BF16)
                 for sub in range(ROW_SUB)], axis=1)
            acc = acc + jnp.dot(weight.astype(BF16), y, preferred_element_type=F32)
        o_ref[...] = acc

    pl.when(i % 2 == 0)(lambda: step(0))
    pl.when(i % 2 == 1)(lambda: step(1))


def _combine(tcnt, tloc, tglb, y_pad, h, route):
    t, d = h.shape
    tile = lambda i, *_: (i, 0)
    return pl.pallas_call(
        _combine_kernel,
        grid_spec=pltpu.PrefetchScalarGridSpec(
            num_scalar_prefetch=3,
            grid=(t // TM,),
            in_specs=[
                pl.BlockSpec(memory_space=pl.ANY),
                pl.BlockSpec((TM, d), tile),
                pl.BlockSpec((TM, 2 * TOP_K), tile),
            ],
            out_specs=pl.BlockSpec((TM, d), tile),
            scratch_shapes=[pltpu.VMEM((N_SLOTS * ROW_SUB, LANES), F32),
                            pltpu.VMEM((N_SLOTS * ROW_SUB, LANES), F32),
                            pltpu.SemaphoreType.DMA((2,))],
        ),
        out_shape=jax.ShapeDtypeStruct((t, d), F32),
        compiler_params=pltpu.CompilerParams(
            dimension_semantics=("arbitrary",), vmem_limit_bytes=VMEM_LIMIT),
        name="combine",
    )(tcnt, tloc, tglb, y_pad, h, route)


def _rel_vector(rel_bias):
    n_heads = rel_bias.shape[0]
    far = LEFT - REL_CLIP
    falling = rel_bias[:, ::-1][:, 1:]
    n_fall = min(KWIN - 1 - far, 2 * REL_CLIP)
    vec = jnp.concatenate(
        [jnp.broadcast_to(rel_bias[:, -1:], (n_heads, far + 1)), falling[:, :n_fall],
         jnp.broadcast_to(rel_bias[:, :1], (n_heads, KWIN - 1 - far - n_fall)),
         jnp.broadcast_to(rel_bias[:, -1:], (n_heads, BIAS_LANES - KWIN))], axis=1)
    return vec.astype(F32) * LOG2_E


def kernel(x, norm_mix_g, w_in_proj, q_norm_g, k_norm_g, rel_bias, gmlp_v_norm_g, gmlp_w_s, gmlp_b_s,
           att_out_norm_g, gmlp_out_norm_g, w_out_proj, norm_moe_g, w_router, b_router,
           w_expert_in, b_expert_in, w_expert_out, b_expert_out):
    b, s, d = x.shape
    t = b * s

    head_of = jnp.arange(MXU_DIM) // HEAD_DIM
    block_diag = (head_of[:, None] == head_of[None, :]).astype(BF16)
    q, kt, v, gm = _inproj(
        x, norm_mix_g.reshape(1, d), w_in_proj.astype(BF16),
        jnp.tile(q_norm_g, N_HEADS).reshape(1, ATT_W), jnp.tile(k_norm_g, N_HEADS).reshape(1, ATT_W),
        block_diag, gmlp_v_norm_g, gmlp_w_s, gmlp_b_s.T, gmlp_out_norm_g.reshape(1, G_W))
    att = _attention(q, kt, v, _rel_vector(rel_bias), att_out_norm_g.reshape(1, ATT_W))

    wr_t = w_router.T
    wr_hi = wr_t.astype(BF16)
    wr_lo = (wr_t - wr_hi.astype(F32)).astype(BF16)
    h, xn, slots, route, tile_cnt = _outproj(
        att.reshape(t, ATT_W), gm.reshape(t, G_W), x.reshape(t, d), w_out_proj.astype(BF16),
        norm_moe_g.reshape(1, d), jnp.concatenate([wr_hi, wr_lo], axis=0), b_router.reshape(N_EXPERTS, 1))

    tcnt = tile_cnt[:, :, 0]
    counts = jnp.sum(tcnt, axis=0)
    padded = ((counts + TME - 1) // TME) * TME
    pad_ends = jnp.cumsum(padded).astype(jnp.int32)
    pad_starts = pad_ends - padded
    tloc = jnp.cumsum(tcnt, axis=1) - tcnt
    tglb = pad_starts[None, :] + jnp.cumsum(tcnt, axis=0) - tcnt
    n_blocks = (t * TOP_K) // TME + N_EXPERTS
    nvb = pad_ends[-1] // TME
    block_start = jnp.arange(n_blocks, dtype=jnp.int32) * TME
    block_e = jnp.minimum(jnp.sum(pad_ends[None, :] <= block_start[:, None], axis=1), N_EXPERTS - 1)
    block_e = jnp.where(jnp.arange(n_blocks) < nvb, block_e, block_e[nvb - 1]).astype(jnp.int32)
    runs = tuple((a * ROW_SUB).reshape(-1).astype(jnp.int32) for a in (tcnt, tloc, tglb))
    present = padded > 0
    run_of_expert = jnp.cumsum(present) - 1
    experts_iota = jnp.arange(N_EXPERTS)
    block_run = jnp.sum(jnp.where(block_e[:, None] == experts_iota[None, :], run_of_expert[None, :], 0),
                        axis=1).astype(jnp.int32)
    run_expert = jnp.sum(jnp.where(present[None, :] & (run_of_expert[None, :] == experts_iota[:, None]),
                                   experts_iota[None, :], 0), axis=1).astype(jnp.int32)
    n_runs = jnp.sum(present).astype(jnp.int32).reshape(1)
    of_block = block_e[:, None] == experts_iota[None, :]
    rows_left = jnp.sum(jnp.where(of_block, (pad_starts + counts)[None, :], 0), axis=1) - block_start
    block_used = jnp.clip(rows_left, 0, TME).astype(jnp.int32)

    pad_at = ((pad_starts + counts) * ROW_SUB).astype(jnp.int32)
    pad_n = ((padded - counts) * ROW_SUB).astype(jnp.int32)
    x_pad = _dispatch(*runs, pad_ends, pad_at, pad_n, xn, slots, n_blocks * TME)
    y_pad = _experts(block_e, nvb.reshape(1), block_run, run_expert, n_runs, block_used, x_pad, w_expert_in,
                     b_expert_in.reshape(N_EXPERTS, 1, 2 * D_EXPERT), w_expert_out,
                     b_expert_out.reshape(N_EXPERTS, 1, d))
    out = _combine(*runs, y_pad, h, route)
    return out.reshape(b, s, d)
```
